```python
import math
import jax, jax.numpy as jnp
from jax import lax
import numpy as np

D_MODEL = 4096
BATCH = 2
SEQ = 8192
DEPTH = 1
DEC_BATCH = 2
DEC_SEQ = 4096
PAST_LEN = 128

HEAD_DIM = 128
N_HEADS = 16
ATTN_WIDTH = N_HEADS * HEAD_DIM
DILATED_PATTERNS = ((128, 1), (512, 4), (2048, 16))
SIDE_KEYS = 64
BLK = SIDE_KEYS
N_BUCKETS = 32
REL_MAX_DIST = 1024
NEG = -1e30
SSM_WIDTH = 2048
SSM_GROUP = 16
SSM_GROUPS = SSM_WIDTH // SSM_GROUP
SSM_STATE = 64
SSM_CHUNK = 128
DT_MIN = 0.001
DT_MAX = 0.1
EPS = 1e-6
IN_WIDTH = 4 * ATTN_WIDTH + 2 * SSM_WIDTH + 2 * D_MODEL
SPLITS = [ATTN_WIDTH, 2 * ATTN_WIDTH, 3 * ATTN_WIDTH, 4 * ATTN_WIDTH,
          4 * ATTN_WIDTH + SSM_WIDTH, 4 * ATTN_WIDTH + 2 * SSM_WIDTH,
          4 * ATTN_WIDTH + 2 * SSM_WIDTH + D_MODEL]

kernel_name = "hybrid_dilated_attn_s5_gated_encoder"


def _rmsnorm(x, g):
    xf = x.astype(jnp.float32)
    y = xf * lax.rsqrt(jnp.mean(xf * xf, axis=-1, keepdims=True) + EPS)
    return (y * g.astype(jnp.float32)).astype(x.dtype)


def _head_rms(t, g):
    tf = t.astype(jnp.float32)
    return tf * lax.rsqrt(jnp.mean(tf * tf, axis=-1, keepdims=True) + EPS) * g.astype(jnp.float32)


def _rel_bucket(rel):
    half = N_BUCKETS // 2
    exact = half // 2
    base = jnp.where(rel > 0, half, 0)
    n = jnp.abs(rel)
    nf = jnp.maximum(n, 1).astype(jnp.float32)
    large = exact + (jnp.log(nf / exact) / math.log(REL_MAX_DIST / exact) * (half - exact)).astype(jnp.int32)
    large = jnp.minimum(large, half - 1)
    return base + jnp.where(n < exact, n, large)


def _dilated_window_attn(q, k, v, dil, rel_bias):
    bt, s, h, hd = q.shape
    L = s // dil
    nb = -(-L // BLK)
    lp = nb * BLK
    n_sub = bt * dil

    def to_sub(t):
        return t.reshape(bt, L, dil, h, hd).transpose(0, 2, 1, 3, 4).reshape(n_sub, L, h, hd)

    def key_blocks(t):
        tp = jnp.pad(t, ((0, 0), (BLK, lp - L + BLK), (0, 0), (0, 0))).reshape(n_sub, nb + 2, BLK, h, hd)
        return jnp.concatenate([tp[:, :-2], tp[:, 1:-1], tp[:, 2:]], axis=2)

    qb = jnp.pad(to_sub(q), ((0, 0), (0, lp - L), (0, 0), (0, 0))).reshape(n_sub, nb, BLK, h, hd)
    kb = key_blocks(to_sub(k))
    vb = key_blocks(to_sub(v))
    logits = jnp.einsum('nbqhd,nbkhd->nbhqk', qb, kb)
    qi = jnp.arange(BLK)[:, None]
    kj = jnp.arange(3 * BLK)[None, :]
    rel = kj - BLK - qi
    bias = rel_bias[_rel_bucket(rel * dil)].astype(jnp.float32).transpose(2, 0, 1)
    key_pos = jnp.arange(nb)[:, None] * BLK + kj - BLK
    allowed = (jnp.abs(rel) <= SIDE_KEYS)[None] & ((key_pos >= 0) & (key_pos < L))[:, None, :]
    logits = jnp.where(allowed[None, :, None], logits + bias[None, None], NEG)
    m = jnp.max(logits, axis=-1, keepdims=True)
    p = jnp.exp(logits - m)
    den = jnp.sum(p, axis=-1)
    den_t = den.transpose(0, 1, 3, 2)
    o = jnp.einsum('nbhqk,nbkhd->nbqhd', p, vb) / den_t[..., None]
    lse = m[..., 0].transpose(0, 1, 3, 2) + jnp.log(den_t)

    def from_sub(t):
        t = t.reshape(bt, dil, lp, *t.shape[3:])[:, :, :L]
        t = jnp.moveaxis(t, 1, 2)
        return t.reshape(bt, s, *t.shape[3:])

    return from_sub(o), from_sub(lse)


def _attention_branch(q, k, v, q_gain, k_gain, rel_bias):
    qf = _head_rms(q, q_gain) * (HEAD_DIM ** -0.5)
    kf = _head_rms(k, k_gain)
    vf = v.astype(jnp.float32)
    outs, lses = [], []
    for _, dil in DILATED_PATTERNS:
        o, l = _dilated_window_attn(qf, kf, vf, dil, rel_bias)
        outs.append(o)
        lses.append(l)
    w = jax.nn.softmax(jnp.stack(lses), axis=0)
    return jnp.einsum('gbsh,gbshd->bshd', w, jnp.stack(outs))


def _cplx_combine(e1, e2):
    a1r, a1i, b1r, b1i = e1
    a2r, a2i, b2r, b2i = e2
    return (a2r * a1r - a2i * a1i,
            a2r * a1i + a2i * a1r,
            a2r * b1r - a2i * b1i + b2r,
            a2r * b1i + a2i * b1r + b2i)


def _s5_scan(u, lam_re, lam_im, log_dt, b_re, b_im, c_re, c_im):
    bt, s, g, hc = u.shape
    dt = jnp.exp(log_dt.astype(jnp.float32))[:, None]
    lr = lam_re.astype(jnp.float32)
    li = lam_im.astype(jnp.float32)
    mag = jnp.exp(lr * dt)
    ab_re = mag * jnp.cos(li * dt)
    ab_im = mag * jnp.sin(li * dt)
    den = lr * lr + li * li
    f_re = ((ab_re - 1.0) * lr + ab_im * li) / den
    f_im = (ab_im * lr - (ab_re - 1.0) * li) / den
    br = b_re.astype(jnp.float32)
    bi = b_im.astype(jnp.float32)
    bb_re = f_re[..., None] * br - f_im[..., None] * bi
    bb_im = f_re[..., None] * bi + f_im[..., None] * br
    cr = c_re.astype(jnp.float32)
    ci = c_im.astype(jnp.float32)
    n_chunks = s // SSM_CHUNK
    uc = u.reshape(bt, n_chunks, SSM_CHUNK, g, hc).transpose(1, 0, 2, 3, 4)

    def chunk(carry, u_c):
        h_re, h_im = carry
        x_re = jnp.einsum('blgh,gph->blgp', u_c, bb_re)
        x_im = jnp.einsum('blgh,gph->blgp', u_c, bb_im)
        a_re = jnp.broadcast_to(ab_re, x_re.shape)
        a_im = jnp.broadcast_to(ab_im, x_im.shape)
        A_re, A_im, X_re, X_im = lax.associative_scan(_cplx_combine, (a_re, a_im, x_re, x_im), axis=1)
        s_re = X_re + A_re * h_re[:, None] - A_im * h_im[:, None]
        s_im = X_im + A_re * h_im[:, None] + A_im * h_re[:, None]
        y = jnp.einsum('blgp,ghp->blgh', s_re, cr) - jnp.einsum('blgp,ghp->blgh', s_im, ci)
        return (s_re[:, -1], s_im[:, -1]), y

    h0 = jnp.zeros((bt, g, SSM_STATE), jnp.float32)
    _, y = lax.scan(chunk, (h0, h0), uc)
    return y.transpose(1, 0, 2, 3, 4).reshape(bt, s, g, hc)


def _layer(x, rel_bias, norm_gain, w_in, q_gain, k_gain, lam_re, lam_im, log_dt,
           b_re, b_im, c_re, c_im, d_skip, w_glu, w_attn_proj, w_ssm_proj, w_out):
    bt, s, _ = x.shape
    h = _rmsnorm(x, norm_gain)
    proj = h @ w_in
    q, k, v, g_a, u, g_b, m_a, m_b = jnp.split(proj, SPLITS, axis=-1)
    shp = (bt, s, N_HEADS, HEAD_DIM)
    o = _attention_branch(q.reshape(shp), k.reshape(shp), v.reshape(shp), q_gain, k_gain, rel_bias)
    o = o.reshape(bt, s, ATTN_WIDTH).astype(x.dtype)
    z_a = (o * jax.nn.silu(g_a)) @ w_attn_proj
    uf = u.astype(jnp.float32)
    ug = uf.reshape(bt, s, SSM_GROUPS, SSM_GROUP)
    y_f = _s5_scan(ug, lam_re[0], lam_im[0], log_dt[0], b_re[0], b_im[0], c_re[0], c_im[0])
    y_b = jnp.flip(_s5_scan(jnp.flip(ug, axis=1), lam_re[1], lam_im[1], log_dt[1],
                            b_re[1], b_im[1], c_re[1], c_im[1]), axis=1)
    y = (y_f + y_b).reshape(bt, s, SSM_WIDTH) + d_skip.astype(jnp.float32) * uf
    y = jax.nn.gelu(y).astype(x.dtype)
    gl = y @ w_glu
    y = gl[..., :SSM_WIDTH] * jax.nn.sigmoid(gl[..., SSM_WIDTH:])
    z_b = (y * jax.nn.silu(g_b)) @ w_ssm_proj
    merged = jax.nn.sigmoid(m_a) * z_a + jax.nn.sigmoid(m_b) * z_b
    return x + merged @ w_out


def setup_inputs(seed: int = 0) -> dict:
    key = jax.random.key(seed)
    ks = jax.random.split(key, 20)
    f32 = jnp.float32
    nrm = lambda kk, shape, sc: jax.random.normal(kk, shape, f32) * sc
    G, P, Hc = SSM_GROUPS, SSM_STATE, SSM_GROUP
    lam_im = jnp.broadcast_to(math.pi * jnp.arange(P, dtype=f32), (DEPTH, 2, G, P)) + nrm(ks[5], (DEPTH, 2, G, P), 0.01)
    log_dt = jax.random.uniform(ks[6], (DEPTH, 2, G), f32, math.log(DT_MIN), math.log(DT_MAX))
    return {
        "x_prompt": nrm(ks[0], (BATCH, SEQ, D_MODEL), 1.0),
        "x_sample": nrm(ks[1], (DEC_BATCH, DEC_SEQ, D_MODEL), 1.0),
        "rel_bias": nrm(ks[2], (N_BUCKETS, N_HEADS), 0.1),
        "norm_gain": 1.0 + nrm(ks[3], (DEPTH, D_MODEL), 0.02),
        "w_in": nrm(ks[4], (DEPTH, D_MODEL, IN_WIDTH), D_MODEL ** -0.5),
        "q_gain": 1.0 + nrm(ks[7], (DEPTH, HEAD_DIM), 0.02),
        "k_gain": 1.0 + nrm(ks[8], (DEPTH, HEAD_DIM), 0.02),
        "lam_re": -0.5 + nrm(ks[9], (DEPTH, 2, G, P), 0.01),
        "lam_im": lam_im,
        "log_dt": log_dt,
        "b_re": nrm(ks[10], (DEPTH, 2, G, P, Hc), (2 * Hc) ** -0.5),
        "b_im": nrm(ks[11], (DEPTH, 2, G, P, Hc), (2 * Hc) ** -0.5),
        "c_re": nrm(ks[12], (DEPTH, 2, G, Hc, P), P ** -0.5),
        "c_im": nrm(ks[13], (DEPTH, 2, G, Hc, P), P ** -0.5),
        "d_skip": nrm(ks[14], (DEPTH, SSM_WIDTH), 1.0),
        "w_glu": nrm(ks[15], (DEPTH, SSM_WIDTH, 2 * SSM_WIDTH), SSM_WIDTH ** -0.5),
        "w_attn_proj": nrm(ks[16], (DEPTH, ATTN_WIDTH, D_MODEL), ATTN_WIDTH ** -0.5),
        "w_ssm_proj": nrm(ks[17], (DEPTH, SSM_WIDTH, D_MODEL), SSM_WIDTH ** -0.5),
        "w_out": nrm(ks[18], (DEPTH, D_MODEL, D_MODEL), D_MODEL ** -0.5),
    }


def reference(x_prompt, x_sample, rel_bias, norm_gain, w_in, q_gain, k_gain, lam_re, lam_im, log_dt,
              b_re, b_im, c_re, c_im, d_skip, w_glu, w_attn_proj, w_ssm_proj, w_out):
    y_prompt = x_prompt
    y_sample = x_sample
    for l in range(DEPTH):
        params = (rel_bias, norm_gain[l], w_in[l], q_gain[l], k_gain[l], lam_re[l], lam_im[l], log_dt[l],
                  b_re[l], b_im[l], c_re[l], c_im[l], d_skip[l], w_glu[l], w_attn_proj[l],
                  w_ssm_proj[l], w_out[l])
        y_prompt = _layer(y_prompt, *params)
        y_sample = _layer(y_sample, *params)
    return (y_prompt, y_sample)
```

```python
import functools
import math

import jax
import jax.numpy as jnp
from jax import lax
from jax.experimental import pallas as pl
from jax.experimental.pallas import tpu as pltpu

F32 = jnp.float32
BF16 = jnp.bfloat16

D_MODEL = 4096
N_HEADS = 16
HEAD_DIM = 128
ATTN_WIDTH = N_HEADS * HEAD_DIM
DILATIONS = (1, 4, 16)
SIDE_KEYS = 64
N_BUCKETS = 32
REL_MAX_DIST = 1024
NEG = -1e30
EPS = 1e-6
SSM_WIDTH = 2048
SSM_GROUP = 16
SSM_GROUPS = SSM_WIDTH // SSM_GROUP
SSM_STATE = 64
SSM_BLOCK = 16
SSM_ROWS = SSM_BLOCK * SSM_GROUP

COL_Q = 0
COL_K = ATTN_WIDTH
COL_V = 2 * ATTN_WIDTH
COL_GA = 3 * ATTN_WIDTH
COL_U = 4 * ATTN_WIDTH
COL_GB = COL_U + SSM_WIDTH
COL_MA = COL_GB + SSM_WIDTH
COL_MB = COL_MA + D_MODEL
P1_WIDTH = COL_U
P2_WIDTH = SSM_WIDTH + 2 * D_MODEL

LANES = 128
Q_BLOCK = 128
K_BLOCK = 2 * Q_BLOCK
ATTN_TILE = Q_BLOCK * DILATIONS[-1]
HALO = SIDE_KEYS * DILATIONS[-1]

VMEM_LIMIT = 56 * 1024 * 1024


def _params(*sem):
    return pltpu.CompilerParams(dimension_semantics=sem, vmem_limit_bytes=VMEM_LIMIT)


def _prenorm_kernel(x_ref, g_ref, o_ref):
    x = x_ref[...]
    ms = jnp.mean(x * x, axis=-1, keepdims=True)
    o_ref[...] = (x * lax.rsqrt(ms + EPS) * g_ref[...]).astype(o_ref.dtype)


def _prenorm(x2, gain, tm=512):
    t, d = x2.shape
    return pl.pallas_call(
        _prenorm_kernel,
        grid=(t // tm,),
        in_specs=[pl.BlockSpec((tm, d), lambda i: (i, 0)), pl.BlockSpec((1, d), lambda i: (0, 0))],
        out_specs=pl.BlockSpec((tm, d), lambda i: (i, 0)),
        out_shape=jax.ShapeDtypeStruct((t, d), BF16),
        compiler_params=_params("parallel"),
        name="prenorm",
    )(x2, gain.reshape(1, d))


def _proj_kernel(h_ref, w_ref, g_ref, o_ref, *, n_norm_tiles):
    acc = jnp.dot(h_ref[...], w_ref[...], preferred_element_type=F32)
    j = pl.program_id(0)

    @pl.when(j < n_norm_tiles)
    def _():
        for hh in range(acc.shape[1] // HEAD_DIM):
            sl = slice(hh * HEAD_DIM, (hh + 1) * HEAD_DIM)
            blk = acc[:, sl]
            ms = jnp.mean(blk * blk, axis=-1, keepdims=True)
            o_ref[:, sl] = blk * lax.rsqrt(ms + EPS) * g_ref[:, sl]

    @pl.when(j >= n_norm_tiles)
    def _():
        o_ref[...] = acc


def _proj(h, w, col0, width, gain_row, n_norm_tiles, name, tm=512, tn=2048):
    t, k = h.shape
    if gain_row is None:
        gain_row = jnp.zeros((1, width), F32)
    return pl.pallas_call(
        functools.partial(_proj_kernel, n_norm_tiles=n_norm_tiles),
        grid=(width // tn, t // tm),
        in_specs=[
            pl.BlockSpec((tm, k), lambda j, i: (i, 0)),
            pl.BlockSpec((k, tn), lambda j, i: (0, col0 // tn + j)),
            pl.BlockSpec((1, tn), lambda j, i: (0, j)),
        ],
        out_specs=pl.BlockSpec((tm, tn), lambda j, i: (i, j)),
        out_shape=jax.ShapeDtypeStruct((t, width), F32),
        compiler_params=_params("parallel", "parallel"),
        name=name,
    )(h, w, gain_row)


def _proj_u_kernel(w_ref, h_ref, o_ref):
    o_ref[...] = lax.dot_general(w_ref[...], h_ref[...], (((1,), (1,)), ((), ())),
                                 preferred_element_type=F32)


def _proj_u(h, wu_t, b, s, tmt=256):
    ntau = s // SSM_BLOCK
    h3 = h.reshape(b, ntau, SSM_BLOCK * D_MODEL)
    return pl.pallas_call(
        _proj_u_kernel,
        grid=(b, SSM_BLOCK, ntau // tmt),
        in_specs=[
            pl.BlockSpec((SSM_WIDTH, D_MODEL), lambda bb, i, tt: (0, 0)),
            pl.BlockSpec((None, tmt, D_MODEL), lambda bb, i, tt: (bb, tt, i)),
        ],
        out_specs=pl.BlockSpec((None, None, SSM_WIDTH, tmt), lambda bb, i, tt: (bb, i, 0, tt)),
        out_shape=jax.ShapeDtypeStruct((b, SSM_BLOCK, SSM_WIDTH, ntau), F32),
        compiler_params=_params("parallel", "parallel", "parallel"),
        name="proj_u",
    )(wu_t, h3)


def _rel_bucket(rel):
    half = N_BUCKETS // 2
    exact = half // 2
    base = jnp.where(rel > 0, half, 0)
    n = jnp.abs(rel)
    nf = jnp.maximum(n, 1).astype(F32)
    large = exact + (jnp.log(nf / exact) / math.log(REL_MAX_DIST / exact) * (half - exact)).astype(jnp.int32)
    large = jnp.minimum(large, half - 1)
    return base + jnp.where(n < exact, n, large)


def _bias_tables(rel_bias):
    qi = jnp.arange(Q_BLOCK)[:, None]
    kj = jnp.arange(K_BLOCK)[None, :]
    rel = kj - SIDE_KEYS - qi
    inside = jnp.abs(rel) <= SIDE_KEYS
    tabs = []
    for d in DILATIONS:
        bias = rel_bias[_rel_bucket(rel * d)].astype(F32)
        bias = jnp.where(inside[..., None], bias, NEG)
        tabs.append(jnp.transpose(bias, (2, 0, 1)))
    return jnp.stack(tabs)


def _band_rows(prev_ref, cur_ref, next_ref, d, r, blk, nblk):
    lq = Q_BLOCK * nblk
    lo = Q_BLOCK * blk - SIDE_KEYS
    hi = lo + K_BLOCK
    pieces = []
    if lo < 0:
        pieces.append(prev_ref[pl.ds(HALO - SIDE_KEYS * d + r, SIDE_KEYS, stride=d), :])
    c0, c1 = max(lo, 0), min(hi, lq)
    pieces.append(cur_ref[pl.ds(r + d * c0, c1 - c0, stride=d), :])
    if hi > lq:
        pieces.append(next_ref[pl.ds(r, SIDE_KEYS, stride=d), :])
    rows = pieces[0] if len(pieces) == 1 else jnp.concatenate(pieces, axis=0)
    return rows.astype(BF16)


def _attn_kernel(q_ref, kp_ref, kc_ref, kn_ref, vp_ref, vc_ref, vn_ref, ga_ref, bias_ref, o_ref,
                 obuf, lbuf):
    t = pl.program_id(1)
    lane = lax.broadcasted_iota(jnp.int32, (Q_BLOCK, K_BLOCK), 1)
    before_start = (lane < SIDE_KEYS) & (t == 0)
    after_end = (lane >= K_BLOCK - SIDE_KEYS) & (t == pl.num_programs(1) - 1)
    for pi, d in enumerate(DILATIONS):
        nblk = ATTN_TILE // (d * Q_BLOCK)
        bias = bias_ref[pi]
        for r in range(d):
            for blk in range(nblk):
                rows = pl.ds(r + d * Q_BLOCK * blk, Q_BLOCK, stride=d)
                q = q_ref[rows, :].astype(BF16)
                k = _band_rows(kp_ref, kc_ref, kn_ref, d, r, blk, nblk)
                v = _band_rows(vp_ref, vc_ref, vn_ref, d, r, blk, nblk)
                s = lax.dot_general(q, k, (((1,), (1,)), ((), ())), preferred_element_type=F32) + bias
                if blk == 0:
                    s = jnp.where(before_start, NEG, s)
                if blk == nblk - 1:
                    s = jnp.where(after_end, NEG, s)
                m = jnp.max(s, axis=-1, keepdims=True)
                p = jnp.exp(s - m)
                den = jnp.sum(p, axis=-1, keepdims=True)
                o = jnp.dot(p.astype(BF16), v, preferred_element_type=F32) / den
                obuf[pi, rows, :] = o
                lbuf[pi, rows, :] = jnp.broadcast_to(m + jnp.log(den), (Q_BLOCK, HEAD_DIM))
    l0, l1, l2 = lbuf[0], lbuf[1], lbuf[2]
    mx = jnp.maximum(jnp.maximum(l0, l1), l2)
    e0, e1, e2 = jnp.exp(l0 - mx), jnp.exp(l1 - mx), jnp.exp(l2 - mx)
    o = (e0 * obuf[0] + e1 * obuf[1] + e2 * obuf[2]) / (e0 + e1 + e2)
    ga = ga_ref[...]
    o_ref[...] = (o * (ga * jax.nn.sigmoid(ga))).astype(o_ref.dtype)


def _attention(p1, bias_tabs, b, s):
    p1 = p1.reshape(b, s, P1_WIDTH)
    nt = s // ATTN_TILE
    per = ATTN_TILE // HALO
    n_halo = s // HALO

    def tile(col):
        return pl.BlockSpec((None, ATTN_TILE, HEAD_DIM), lambda bb, t, h: (bb, t, col // HEAD_DIM + h))

    def prev(col):
        return pl.BlockSpec((None, HALO, HEAD_DIM),
                            lambda bb, t, h: (bb, jnp.maximum(t * per - 1, 0), col // HEAD_DIM + h))

    def nxt(col):
        return pl.BlockSpec((None, HALO, HEAD_DIM),
                            lambda bb, t, h: (bb, jnp.minimum((t + 1) * per, n_halo - 1), col // HEAD_DIM + h))

    return pl.pallas_call(
        _attn_kernel,
        grid=(b, nt, N_HEADS),
        in_specs=[tile(COL_Q), prev(COL_K), tile(COL_K), nxt(COL_K), prev(COL_V), tile(COL_V), nxt(COL_V),
                  tile(COL_GA),
                  pl.BlockSpec((len(DILATIONS), None, Q_BLOCK, K_BLOCK), lambda bb, t, h: (0, h, 0, 0))],
        out_specs=pl.BlockSpec((None, ATTN_TILE, HEAD_DIM), lambda bb, t, h: (bb, t, h)),
        out_shape=jax.ShapeDtypeStruct((b, s, ATTN_WIDTH), BF16),
        scratch_shapes=[pltpu.VMEM((len(DILATIONS), ATTN_TILE, HEAD_DIM), F32),
                        pltpu.VMEM((len(DILATIONS), ATTN_TILE, HEAD_DIM), F32)],
        compiler_params=_params("parallel", "parallel", "parallel"),
        name="attention",
    )(p1, p1, p1, p1, p1, p1, p1, p1, bias_tabs)


def _ssm_weights(lam_re, lam_im, log_dt, b_re, b_im, c_re, c_im):
    hi = lax.Precision.HIGHEST
    ks = jnp.arange(SSM_BLOCK + 1, dtype=F32)[:, None, None]
    mats = []
    for dirn in range(2):
        dt = jnp.exp(log_dt[dirn].astype(F32))[:, None]
        lr = lam_re[dirn].astype(F32)
        li = lam_im[dirn].astype(F32)
        mag = jnp.exp(lr * dt)
        ab_re = mag * jnp.cos(li * dt)
        ab_im = mag * jnp.sin(li * dt)
        den = lr * lr + li * li
        f_re = ((ab_re - 1.0) * lr + ab_im * li) / den
        f_im = (ab_im * lr - (ab_re - 1.0) * li) / den
        br = b_re[dirn].astype(F32)
        bi = b_im[dirn].astype(F32)
        bb_re = f_re[..., None] * br - f_im[..., None] * bi
        bb_im = f_re[..., None] * bi + f_im[..., None] * br
        cr = c_re[dirn].astype(F32)
        ci = c_im[dirn].astype(F32)
        pmag = jnp.exp(ks * (lr * dt)[None])
        pw_re = pmag * jnp.cos(ks * (li * dt)[None])
        pw_im = pmag * jnp.sin(ks * (li * dt)[None])
        ca_re = cr[None] * pw_re[:, :, None, :] - ci[None] * pw_im[:, :, None, :]
        ca_im = cr[None] * pw_im[:, :, None, :] + ci[None] * pw_re[:, :, None, :]
        kern = (jnp.einsum('kghp,gpi->kghi', ca_re[:SSM_BLOCK], bb_re, precision=hi)
                - jnp.einsum('kghp,gpi->kghi', ca_im[:SSM_BLOCK], bb_im, precision=hi))
        pb_re = pw_re[..., None] * bb_re[None] - pw_im[..., None] * bb_im[None]
        pb_im = pw_re[..., None] * bb_im[None] + pw_im[..., None] * bb_re[None]
        mats.append((kern, ca_re, ca_im, pb_re, pb_im, pw_re[SSM_BLOCK], pw_im[SSM_BLOCK]))

    g = SSM_GROUPS
    jj = jnp.arange(SSM_BLOCK)[:, None]
    ii = jnp.arange(SSM_BLOCK)[None, :]

    def toeplitz(kern, lag, keep):
        kk = kern[jnp.clip(lag, 0, SSM_BLOCK - 1)]
        kk = jnp.where(keep[:, :, None, None, None], kk, 0.0)
        return jnp.transpose(kk, (2, 0, 3, 1, 4))

    t0 = toeplitz(mats[0][0], jj - ii, jj >= ii) + toeplitz(mats[1][0], ii - jj, ii >= jj)
    t0 = t0.reshape(g, SSM_ROWS, SSM_ROWS)

    steps = jnp.arange(SSM_BLOCK)
    wz_parts, cc_parts, a_parts = [], [], []
    for dirn in range(2):
        _, ca_re, ca_im, pb_re, pb_im, a_re, a_im = mats[dirn]
        pow_in = (SSM_BLOCK - 1 - steps) if dirn == 0 else steps
        pow_out = (steps + 1) if dirn == 0 else (SSM_BLOCK - steps)
        for pb in (pb_re, pb_im):
            wz_parts.append(jnp.transpose(pb[pow_in], (1, 2, 0, 3)).reshape(g, SSM_STATE, SSM_ROWS))
        for ca, sign in ((ca_re, 1.0), (ca_im, -1.0)):
            cc_parts.append(sign * jnp.transpose(ca[pow_out], (1, 0, 2, 3)).reshape(g, SSM_ROWS, SSM_STATE))
        a_parts += [a_re, a_im]
    wz = jnp.concatenate(wz_parts, axis=1)
    cc = jnp.concatenate(cc_parts, axis=2)
    a16 = jnp.concatenate(a_parts, axis=1)[..., None]
    return wz.astype(BF16), t0.astype(BF16), cc.astype(BF16), a16


def _shift(chunks, s, reverse, lane):
    nc = len(chunks)
    if s >= LANES:
        k = s // LANES
        if reverse:
            return [chunks[c + k] if c + k < nc else None for c in range(nc)]
        return [chunks[c - k] if c - k >= 0 else None for c in range(nc)]
    amt = (LANES - s) if reverse else s
    rolled = [None if ch is None else pltpu.roll(ch, amt, 1) for ch in chunks]
    own = (lane < LANES - s) if reverse else (lane >= s)
    out = []
    for c in range(nc):
        nb = c + 1 if reverse else c - 1
        other = rolled[nb] if 0 <= nb < nc else None
        mine = rolled[c]
        if mine is None and other is None:
            out.append(None)
        else:
            zero = jnp.zeros((SSM_STATE, LANES), F32)
            out.append(jnp.where(own, zero if mine is None else mine, zero if other is None else other))
    return out


def _gelu_tanh(x):
    return 0.5 * x * (1.0 + jnp.tanh(math.sqrt(2.0 / math.pi) * (x + 0.044715 * (x * x * x))))


def _ssm_kernel(u_ref, wz_ref, t0_ref, cc_ref, a_ref, d_ref, y_ref, *, gc, ntau):
    nc = ntau // LANES
    lane = lax.broadcasted_iota(jnp.int32, (SSM_STATE, LANES), 1)
    for gi in range(gc):
        rows = slice(SSM_GROUP * gi, SSM_GROUP * (gi + 1))
        xf = u_ref[:, rows, :].reshape(SSM_ROWS, ntau)
        x = xf.astype(BF16)
        z = jnp.dot(wz_ref[gi], x, preferred_element_type=F32)
        a = a_ref[gi]
        states = []
        for dirn in range(2):
            reverse = dirn == 1
            base = 2 * SSM_STATE * dirn
            zr = [z[base:base + SSM_STATE, LANES * c:LANES * (c + 1)] for c in range(nc)]
            zi = [z[base + SSM_STATE:base + 2 * SSM_STATE, LANES * c:LANES * (c + 1)] for c in range(nc)]
            ar = jnp.broadcast_to(a[base:base + SSM_STATE], (SSM_STATE, LANES))
            ai = jnp.broadcast_to(a[base + SSM_STATE:base + 2 * SSM_STATE], (SSM_STATE, LANES))
            s = 1
            while s < ntau:
                sr = _shift(zr, s, reverse, lane)
                si = _shift(zi, s, reverse, lane)
                zr = [zr[c] if sr[c] is None else zr[c] + (ar * sr[c] - ai * si[c]) for c in range(nc)]
                zi = [zi[c] if sr[c] is None else zi[c] + (ar * si[c] + ai * sr[c]) for c in range(nc)]
                ar, ai = ar * ar - ai * ai, 2.0 * (ar * ai)
                s *= 2
            zero = jnp.zeros((SSM_STATE, LANES), F32)
            for part in (zr, zi):
                ex = _shift(part, 1, reverse, lane)
                states.append(jnp.concatenate([zero if e is None else e for e in ex], axis=1))
        h = jnp.concatenate(states, axis=0).astype(BF16)
        y = jnp.dot(t0_ref[gi], x, preferred_element_type=F32) + jnp.dot(cc_ref[gi], h, preferred_element_type=F32)
        y = _gelu_tanh(y + d_ref[gi] * xf)
        y_ref[:, rows, :] = y.astype(y_ref.dtype).reshape(SSM_BLOCK, SSM_GROUP, ntau)


def _ssm(u_t, wz, t0, cc, a16, d_rows, gc=4):
    b, _, _, ntau = u_t.shape
    wspec = pl.BlockSpec((gc, SSM_ROWS, SSM_ROWS), lambda bb, g: (g, 0, 0))
    vspec = pl.BlockSpec((gc, SSM_ROWS, 1), lambda bb, g: (g, 0, 0))
    uspec = pl.BlockSpec((None, SSM_BLOCK, SSM_GROUP * gc, ntau), lambda bb, g: (bb, 0, g, 0))
    return pl.pallas_call(
        functools.partial(_ssm_kernel, gc=gc, ntau=ntau),
        grid=(b, SSM_GROUPS // gc),
        in_specs=[uspec, wspec, wspec, wspec, vspec, vspec],
        out_specs=uspec,
        out_shape=jax.ShapeDtypeStruct(u_t.shape, BF16),
        compiler_params=_params("parallel", "parallel"),
        name="ssm",
    )(u_t, wz, t0, cc, a16, d_rows)


def _glu_kernel(y_ref, w1_ref, w2_ref, gb_ref, o_ref):
    yt = y_ref[...].T
    gl1 = jnp.dot(yt, w1_ref[...], preferred_element_type=F32)
    gl2 = jnp.dot(yt, w2_ref[...], preferred_element_type=F32)
    gb = gb_ref[...]
    o_ref[...] = (gl1 * jax.nn.sigmoid(gl2) * (gb * jax.nn.sigmoid(gb))).astype(o_ref.dtype)


def _glu(y_t, w_glu, p2, b, s, tmt=256, tn=1024):
    ntau = s // SSM_BLOCK
    nj = SSM_WIDTH // tn
    p2v = p2.reshape(b, ntau, SSM_BLOCK * P2_WIDTH)
    out = pl.pallas_call(
        _glu_kernel,
        grid=(nj, b, SSM_BLOCK, ntau // tmt),
        in_specs=[
            pl.BlockSpec((None, None, SSM_WIDTH, tmt), lambda j, bb, i, tt: (bb, i, 0, tt)),
            pl.BlockSpec((SSM_WIDTH, tn), lambda j, bb, i, tt: (0, j)),
            pl.BlockSpec((SSM_WIDTH, tn), lambda j, bb, i, tt: (0, nj + j)),
            pl.BlockSpec((None, tmt, tn), lambda j, bb, i, tt: (bb, tt, i * (P2_WIDTH // tn) + j)),
        ],
        out_specs=pl.BlockSpec((None, tmt, tn), lambda j, bb, i, tt: (bb, tt, i * nj + j)),
        out_shape=jax.ShapeDtypeStruct((b, ntau, SSM_BLOCK * SSM_WIDTH), BF16),
        compiler_params=_params("parallel", "parallel", "parallel", "parallel"),
        name="glu",
    )(y_t, w_glu, w_glu, p2v)
    return out.reshape(b * s, SSM_WIDTH)


def _merge_kernel(a_ref, y_ref, wa_ref, wb_ref, ma_ref, mb_ref, o_ref):
    za = jnp.dot(a_ref[...], wa_ref[...], preferred_element_type=F32)
    zb = jnp.dot(y_ref[...], wb_ref[...], preferred_element_type=F32)
    o_ref[...] = (jax.nn.sigmoid(ma_ref[...]) * za + jax.nn.sigmoid(mb_ref[...]) * zb).astype(o_ref.dtype)


def _merge(a, y2, w_attn_proj, w_ssm_proj, p2, tm=512, tn=1024):
    t = a.shape[0]
    ma0 = (COL_MA - COL_GB) // tn
    mb0 = (COL_MB - COL_GB) // tn
    return pl.pallas_call(
        _merge_kernel,
        grid=(D_MODEL // tn, t // tm),
        in_specs=[
            pl.BlockSpec((tm, ATTN_WIDTH), lambda j, i: (i, 0)),
            pl.BlockSpec((tm, SSM_WIDTH), lambda j, i: (i, 0)),
            pl.BlockSpec((ATTN_WIDTH, tn), lambda j, i: (0, j)),
            pl.BlockSpec((SSM_WIDTH, tn), lambda j, i: (0, j)),
            pl.BlockSpec((tm, tn), lambda j, i: (i, ma0 + j)),
            pl.BlockSpec((tm, tn), lambda j, i: (i, mb0 + j)),
        ],
        out_specs=pl.BlockSpec((tm, tn), lambda j, i: (i, j)),
        out_shape=jax.ShapeDtypeStruct((t, D_MODEL), BF16),
        compiler_params=_params("parallel", "parallel"),
        name="merge",
    )(a, y2, w_attn_proj, w_ssm_proj, p2, p2)


def _out_kernel(m_ref, w_ref, x_ref, o_ref):
    o_ref[...] = x_ref[...] + jnp.dot(m_ref[...], w_ref[...], preferred_element_type=F32)


def _out_proj(merged, w_out, x2, tm=512, tn=1024):
    t = merged.shape[0]
    return pl.pallas_call(
        _out_kernel,
        grid=(D_MODEL // tn, t // tm),
        in_specs=[
            pl.BlockSpec((tm, D_MODEL), lambda j, i: (i, 0)),
            pl.BlockSpec((D_MODEL, tn), lambda j, i: (0, j)),
            pl.BlockSpec((tm, tn), lambda j, i: (i, j)),
        ],
        out_specs=pl.BlockSpec((tm, tn), lambda j, i: (i, j)),
        out_shape=jax.ShapeDtypeStruct((t, D_MODEL), F32),
        compiler_params=_params("parallel", "parallel"),
        name="out_proj",
    )(merged, w_out, x2)


def _layer(x, w):
    b, s, d = x.shape
    x2 = x.reshape(b * s, d)
    h = _prenorm(x2, w["norm_gain"])
    p1 = _proj(h, w["w_in"], COL_Q, P1_WIDTH, w["qk_gain_row"], 2, "proj_attn")
    p2 = _proj(h, w["w_in"], COL_GB, P2_WIDTH, None, 0, "proj_gates")
    u_t = _proj_u(h, w["wu_t"], b, s)
    a = _attention(p1, w["bias_tabs"], b, s).reshape(b * s, ATTN_WIDTH)
    y_t = _ssm(u_t, w["wz"], w["t0"], w["cc"], w["a16"], w["d_rows"])
    y2 = _glu(y_t, w["w_glu"], p2, b, s)
    merged = _merge(a, y2, w["w_attn_proj"], w["w_ssm_proj"], p2)
    return _out_proj(merged, w["w_out"], x2).reshape(b, s, d)


def _prepare(rel_bias, norm_gain, w_in, q_gain, k_gain, lam_re, lam_im, log_dt, b_re, b_im, c_re, c_im,
             d_skip, w_glu, w_attn_proj, w_ssm_proj, w_out):
    wz, t0, cc, a16 = _ssm_weights(lam_re, lam_im, log_dt, b_re, b_im, c_re, c_im)
    gains = jnp.concatenate([jnp.tile(q_gain.astype(F32) * (HEAD_DIM ** -0.5), N_HEADS),
                             jnp.tile(k_gain.astype(F32), N_HEADS),
                             jnp.zeros((P1_WIDTH - 2 * ATTN_WIDTH,), F32)])
    d_rows = jnp.tile(d_skip.astype(F32).reshape(SSM_GROUPS, 1, SSM_GROUP), (1, SSM_BLOCK, 1))
    return {
        "norm_gain": norm_gain.astype(F32),
        "w_in": w_in.astype(BF16),
        "wu_t": w_in[:, COL_U:COL_U + SSM_WIDTH].T.astype(BF16),
        "qk_gain_row": gains.reshape(1, P1_WIDTH),
        "bias_tabs": _bias_tables(rel_bias),
        "wz": wz, "t0": t0, "cc": cc, "a16": a16,
        "d_rows": d_rows.reshape(SSM_GROUPS, SSM_ROWS, 1),
        "w_glu": w_glu.astype(BF16),
        "w_attn_proj": w_attn_proj.astype(BF16),
        "w_ssm_proj": w_ssm_proj.astype(BF16),
        "w_out": w_out.astype(BF16),
    }


def kernel(x_prompt, x_sample, rel_bias, norm_gain, w_in, q_gain, k_gain, lam_re, lam_im, log_dt, b_re, b_im,
           c_re, c_im, d_skip, w_glu, w_attn_proj, w_ssm_proj, w_out):
    y_prompt, y_sample = x_prompt, x_sample
    for l in range(norm_gain.shape[0]):
        w = _prepare(rel_bias, norm_gain[l], w_in[l], q_gain[l], k_gain[l], lam_re[l], lam_im[l], log_dt[l],
                     b_re[l], b_im[l], c_re[l], c_im[l], d_skip[l], w_glu[l], w_attn_proj[l], w_ssm_proj[l],
                     w_out[l])
        y_prompt = _layer(y_prompt, w)
        y_sample = _layer(y_sample, w)
    return (y_prompt, y_sample)
```

```python
import functools
import math

import jax
import jax.numpy as jnp
from jax import lax
from jax.experimental import pallas as pl
from jax.experimental.pallas import tpu as pltpu

F32 = jnp.float32
BF16 = jnp.bfloat16

D_MODEL = 4096
N_HEADS = 16
HEAD_DIM = 128
ATTN_WIDTH = N_HEADS * HEAD_DIM
DILATIONS = (1, 4, 16)
SIDE_KEYS = 64
N_BUCKETS = 32
REL_MAX_DIST = 1024
NEG = -1e30
EPS = 1e-6
SSM_WIDTH = 2048
SSM_GROUP = 16
SSM_GROUPS = SSM_WIDTH // SSM_GROUP
SSM_STATE = 64
SSM_BLOCK = 16
SSM_ROWS = SSM_BLOCK * SSM_GROUP

COL_Q = 0
COL_K = ATTN_WIDTH
COL_V = 2 * ATTN_WIDTH
COL_GA = 3 * ATTN_WIDTH
COL_U = 4 * ATTN_WIDTH
COL_GB = COL_U + SSM_WIDTH
COL_MA = COL_GB + SSM_WIDTH
COL_MB = COL_MA + D_MODEL
P1_WIDTH = COL_U
P2_WIDTH = SSM_WIDTH + 2 * D_MODEL

LANES = 128
Q_BLOCK = 128
K_BLOCK = 2 * Q_BLOCK
ATTN_TILE = Q_BLOCK * DILATIONS[-1]
HALO = SIDE_KEYS * DILATIONS[-1]

VMEM_LIMIT = 56 * 1024 * 1024


def _params(*sem):
    return pltpu.CompilerParams(dimension_semantics=sem, vmem_limit_bytes=VMEM_LIMIT)


def _prenorm_kernel(x_ref, g_ref, perm_ref, o_ref, op_ref):
    x = x_ref[...]
    ms = jnp.mean(x * x, axis=-1, keepdims=True)
    h = (x * lax.rsqrt(ms + EPS) * g_ref[...]).astype(BF16)
    o_ref[...] = h
    hp = jnp.dot(perm_ref[...], h, preferred_element_type=F32).astype(BF16)
    op_ref[...] = hp.reshape(op_ref.shape)


def _prenorm(x, gain, tm=512):
    b, s, d = x.shape
    return pl.pallas_call(
        _prenorm_kernel,
        grid=(b, s // tm),
        in_specs=[pl.BlockSpec((None, tm, d), lambda bb, t: (bb, t, 0)),
                  pl.BlockSpec((1, d), lambda bb, t: (0, 0)),
                  pl.BlockSpec((tm, tm), lambda bb, t: (0, 0))],
        out_specs=[pl.BlockSpec((None, tm, d), lambda bb, t: (bb, t, 0)),
                   pl.BlockSpec((None, SSM_BLOCK, tm // SSM_BLOCK, d), lambda bb, t: (bb, 0, t, 0))],
        out_shape=[jax.ShapeDtypeStruct((b, s, d), BF16),
                   jax.ShapeDtypeStruct((b, SSM_BLOCK, s // SSM_BLOCK, d), BF16)],
        compiler_params=_params("parallel", "parallel"),
        name="prenorm",
    )(x, gain.reshape(1, d), _step_permutation(tm))


def _proj_kernel(h_ref, w_ref, g_ref, o_ref, *, n_norm_tiles):
    acc = jnp.dot(h_ref[...], w_ref[...], preferred_element_type=F32)
    j = pl.program_id(0)

    @pl.when(j < n_norm_tiles)
    def _():
        for hh in range(acc.shape[1] // HEAD_DIM):
            sl = slice(hh * HEAD_DIM, (hh + 1) * HEAD_DIM)
            blk = acc[:, sl]
            ms = jnp.mean(blk * blk, axis=-1, keepdims=True)
            o_ref[:, sl] = blk * lax.rsqrt(ms + EPS) * g_ref[:, sl]

    @pl.when(j >= n_norm_tiles)
    def _():
        o_ref[...] = acc


def _proj(h, w, col0, width, gain_row, n_norm_tiles, name, tm=512, tn=2048):
    t, k = h.shape
    if gain_row is None:
        gain_row = jnp.zeros((1, width), F32)
    return pl.pallas_call(
        functools.partial(_proj_kernel, n_norm_tiles=n_norm_tiles),
        grid=(width // tn, t // tm),
        in_specs=[
            pl.BlockSpec((tm, k), lambda j, i: (i, 0)),
            pl.BlockSpec((k, tn), lambda j, i: (0, col0 // tn + j)),
            pl.BlockSpec((1, tn), lambda j, i: (0, j)),
        ],
        out_specs=pl.BlockSpec((tm, tn), lambda j, i: (i, j)),
        out_shape=jax.ShapeDtypeStruct((t, width), F32),
        compiler_params=_params("parallel", "parallel"),
        name=name,
    )(h, w, gain_row)


def _proj_u_kernel(w_ref, h_ref, o_ref):
    o_ref[...] = lax.dot_general(w_ref[...], h_ref[...], (((1,), (1,)), ((), ())),
                                 preferred_element_type=F32)


def _proj_u(h_steps, wu_t, tmt=256):
    b, _, ntau, _ = h_steps.shape
    return pl.pallas_call(
        _proj_u_kernel,
        grid=(b, SSM_BLOCK, ntau // tmt),
        in_specs=[
            pl.BlockSpec((SSM_WIDTH, D_MODEL), lambda bb, i, tt: (0, 0)),
            pl.BlockSpec((None, None, tmt, D_MODEL), lambda bb, i, tt: (bb, i, tt, 0)),
        ],
        out_specs=pl.BlockSpec((None, None, SSM_WIDTH, tmt), lambda bb, i, tt: (bb, i, 0, tt)),
        out_shape=jax.ShapeDtypeStruct((b, SSM_BLOCK, SSM_WIDTH, ntau), F32),
        compiler_params=_params("parallel", "parallel", "parallel"),
        name="proj_u",
    )(wu_t, h_steps)


def _rel_bucket(rel):
    half = N_BUCKETS // 2
    exact = half // 2
    base = jnp.where(rel > 0, half, 0)
    n = jnp.abs(rel)
    nf = jnp.maximum(n, 1).astype(F32)
    large = exact + (jnp.log(nf / exact) / math.log(REL_MAX_DIST / exact) * (half - exact)).astype(jnp.int32)
    large = jnp.minimum(large, half - 1)
    return base + jnp.where(n < exact, n, large)


def _bias_tables(rel_bias):
    qi = jnp.arange(Q_BLOCK)[:, None]
    kj = jnp.arange(K_BLOCK)[None, :]
    rel = kj - SIDE_KEYS - qi
    inside = jnp.abs(rel) <= SIDE_KEYS
    tabs = []
    for d in DILATIONS:
        bias = rel_bias[_rel_bucket(rel * d)].astype(F32)
        bias = jnp.where(inside[..., None], bias, NEG)
        tabs.append(jnp.transpose(bias, (2, 0, 1)))
    return jnp.stack(tabs)


def _band_rows(prev_ref, cur_ref, next_ref, d, r, blk, nblk):
    lq = Q_BLOCK * nblk
    lo = Q_BLOCK * blk - SIDE_KEYS
    hi = lo + K_BLOCK
    pieces = []
    if lo < 0:
        pieces.append(prev_ref[pl.ds(HALO - SIDE_KEYS * d + r, SIDE_KEYS, stride=d), :])
    c0, c1 = max(lo, 0), min(hi, lq)
    pieces.append(cur_ref[pl.ds(r + d * c0, c1 - c0, stride=d), :])
    if hi > lq:
        pieces.append(next_ref[pl.ds(r, SIDE_KEYS, stride=d), :])
    rows = pieces[0] if len(pieces) == 1 else jnp.concatenate(pieces, axis=0)
    return rows.astype(BF16)


def _attn_kernel(q_ref, kp_ref, kc_ref, kn_ref, vp_ref, vc_ref, vn_ref, ga_ref, bias_ref, o_ref,
                 obuf, lbuf):
    t = pl.program_id(1)
    lane = lax.broadcasted_iota(jnp.int32, (Q_BLOCK, K_BLOCK), 1)
    before_start = (lane < SIDE_KEYS) & (t == 0)
    after_end = (lane >= K_BLOCK - SIDE_KEYS) & (t == pl.num_programs(1) - 1)
    for pi, d in enumerate(DILATIONS):
        nblk = ATTN_TILE // (d * Q_BLOCK)
        bias = bias_ref[pi]
        for r in range(d):
            for blk in range(nblk):
                rows = pl.ds(r + d * Q_BLOCK * blk, Q_BLOCK, stride=d)
                q = q_ref[rows, :].astype(BF16)
                k = _band_rows(kp_ref, kc_ref, kn_ref, d, r, blk, nblk)
                v = _band_rows(vp_ref, vc_ref, vn_ref, d, r, blk, nblk)
                s = lax.dot_general(q, k, (((1,), (1,)), ((), ())), preferred_element_type=F32) + bias
                if blk == 0:
                    s = jnp.where(before_start, NEG, s)
                if blk == nblk - 1:
                    s = jnp.where(after_end, NEG, s)
                m = jnp.max(s, axis=-1, keepdims=True)
                p = jnp.exp(s - m)
                den = jnp.sum(p, axis=-1, keepdims=True)
                o = jnp.dot(p.astype(BF16), v, preferred_element_type=F32) / den
                obuf[pi, rows, :] = o
                lbuf[pi, rows, :] = jnp.broadcast_to(m + jnp.log(den), (Q_BLOCK, HEAD_DIM))
    for i in range(SSM_BLOCK):
        rows = pl.ds(i, ATTN_TILE // SSM_BLOCK, stride=SSM_BLOCK)
        l0, l1, l2 = lbuf[0, rows, :], lbuf[1, rows, :], lbuf[2, rows, :]
        mx = jnp.maximum(jnp.maximum(l0, l1), l2)
        e0, e1, e2 = jnp.exp(l0 - mx), jnp.exp(l1 - mx), jnp.exp(l2 - mx)
        o = (e0 * obuf[0, rows, :] + e1 * obuf[1, rows, :] + e2 * obuf[2, rows, :]) / (e0 + e1 + e2)
        ga = ga_ref[rows, :]
        o_ref[i] = (o * (ga * jax.nn.sigmoid(ga))).astype(o_ref.dtype)


def _attention(p1, bias_tabs, b, s):
    p1 = p1.reshape(b, s, P1_WIDTH)
    nt = s // ATTN_TILE
    per = ATTN_TILE // HALO
    n_halo = s // HALO

    def tile(col):
        return pl.BlockSpec((None, ATTN_TILE, HEAD_DIM), lambda bb, t, h: (bb, t, col // HEAD_DIM + h))

    def prev(col):
        return pl.BlockSpec((None, HALO, HEAD_DIM),
                            lambda bb, t, h: (bb, jnp.maximum(t * per - 1, 0), col // HEAD_DIM + h))

    def nxt(col):
        return pl.BlockSpec((None, HALO, HEAD_DIM),
                            lambda bb, t, h: (bb, jnp.minimum((t + 1) * per, n_halo - 1), col // HEAD_DIM + h))

    return pl.pallas_call(
        _attn_kernel,
        grid=(b, nt, N_HEADS),
        in_specs=[tile(COL_Q), prev(COL_K), tile(COL_K), nxt(COL_K), prev(COL_V), tile(COL_V), nxt(COL_V),
                  tile(COL_GA),
                  pl.BlockSpec((len(DILATIONS), None, Q_BLOCK, K_BLOCK), lambda bb, t, h: (0, h, 0, 0))],
        out_specs=pl.BlockSpec((None, SSM_BLOCK, ATTN_TILE // SSM_BLOCK, HEAD_DIM),
                               lambda bb, t, h: (bb, 0, t, h)),
        out_shape=jax.ShapeDtypeStruct((b, SSM_BLOCK, s // SSM_BLOCK, ATTN_WIDTH), BF16),
        scratch_shapes=[pltpu.VMEM((len(DILATIONS), ATTN_TILE, HEAD_DIM), F32),
                        pltpu.VMEM((len(DILATIONS), ATTN_TILE, HEAD_DIM), F32)],
        compiler_params=_params("parallel", "parallel", "parallel"),
        name="attention",
    )(p1, p1, p1, p1, p1, p1, p1, p1, bias_tabs)


def _ssm_prep_kernel(arow_ref, acol_ref, c_ref, bw_ref, wz_ref, t0_ref, cc_ref, *, gc):
    nb, hc, p = SSM_BLOCK, SSM_GROUP, SSM_STATE
    blk_of_lane = lax.broadcasted_iota(jnp.int32, (hc, SSM_ROWS), 1) // hc
    blk_of_lane_p = lax.broadcasted_iota(jnp.int32, (p, SSM_ROWS), 1) // hc

    def body(gi, carry):
        lag_kernels = []
        for dirn in range(2):
            ar = arow_ref[gi, 2 * dirn:2 * dirn + 1, :]
            ai = arow_ref[gi, 2 * dirn + 1:2 * dirn + 2, :]
            cr = c_ref[gi, 2 * dirn]
            ci = c_ref[gi, 2 * dirn + 1]
            pr = jnp.ones((1, p), F32)
            pi = jnp.zeros((1, p), F32)
            ca_re, ca_im = [], []
            for _ in range(nb + 1):
                ca_re.append(cr * pr - ci * pi)
                ca_im.append(cr * pi + ci * pr)
                pr, pi = pr * ar - pi * ai, pr * ai + pi * ar
            bre = bw_ref[gi, 2 * dirn]
            bim = bw_ref[gi, 2 * dirn + 1]
            lag_kernels.append(
                jnp.dot(jnp.concatenate(ca_re[:nb], axis=0).astype(BF16), bre.astype(BF16),
                        preferred_element_type=F32)
                - jnp.dot(jnp.concatenate(ca_im[:nb], axis=0).astype(BF16), bim.astype(BF16),
                          preferred_element_type=F32))
            order = range(1, nb + 1) if dirn == 0 else range(nb, 0, -1)
            c0 = 2 * p * dirn
            cc_ref[gi, :, c0:c0 + p] = jnp.concatenate([ca_re[k] for k in order], axis=0).astype(BF16)
            cc_ref[gi, :, c0 + p:c0 + 2 * p] = (-jnp.concatenate([ca_im[k] for k in order], axis=0)).astype(BF16)
            acr = acol_ref[gi, c0:c0 + p, :]
            aci = acol_ref[gi, c0 + p:c0 + 2 * p, :]
            qr = jnp.ones((p, 1), F32)
            qi = jnp.zeros((p, 1), F32)
            powers = []
            for _ in range(nb):
                powers.append((qr, qi))
                qr, qi = qr * acr - qi * aci, qr * aci + qi * acr
            per = jnp.zeros((p, SSM_ROWS), F32)
            pei = jnp.zeros((p, SSM_ROWS), F32)
            for i in range(nb):
                e = nb - 1 - i if dirn == 0 else i
                per = jnp.where(blk_of_lane_p == i, powers[e][0], per)
                pei = jnp.where(blk_of_lane_p == i, powers[e][1], pei)
            wz_ref[gi, c0:c0 + p, :] = (per * bre - pei * bim).astype(BF16)
            wz_ref[gi, c0 + p:c0 + 2 * p, :] = (per * bim + pei * bre).astype(BF16)
        kf, kb = lag_kernels
        diag = kf[0:hc, :] + kb[0:hc, :]
        for j in range(nb):
            acc = jnp.zeros((hc, SSM_ROWS), F32)
            for i in range(nb):
                if i < j:
                    src = kf[hc * (j - i):hc * (j - i + 1), :]
                elif i > j:
                    src = kb[hc * (i - j):hc * (i - j + 1), :]
                else:
                    src = diag
                acc = jnp.where(blk_of_lane == i, src, acc)
            t0_ref[gi, hc * j:hc * (j + 1), :] = acc.astype(BF16)
        return carry

    lax.fori_loop(0, gc, body, 0)


def _ssm_weights(lam_re, lam_im, log_dt, b_re, b_im, c_re, c_im, gc=8):
    g, p, hc = SSM_GROUPS, SSM_STATE, SSM_GROUP
    dt = jnp.exp(log_dt.astype(F32))[..., None]
    lr = lam_re.astype(F32)
    li = lam_im.astype(F32)
    mag = jnp.exp(lr * dt)
    ab_re = mag * jnp.cos(li * dt)
    ab_im = mag * jnp.sin(li * dt)
    den = lr * lr + li * li
    f_re = ((ab_re - 1.0) * lr + ab_im * li) / den
    f_im = (ab_im * lr - (ab_re - 1.0) * li) / den
    br = b_re.astype(F32)
    bi = b_im.astype(F32)
    bb_re = f_re[..., None] * br - f_im[..., None] * bi
    bb_im = f_re[..., None] * bi + f_im[..., None] * br
    mag16 = jnp.exp(SSM_BLOCK * (lr * dt))
    a16_re = mag16 * jnp.cos(SSM_BLOCK * (li * dt))
    a16_im = mag16 * jnp.sin(SSM_BLOCK * (li * dt))

    def pack(re, im):
        return jnp.stack([re[0], im[0], re[1], im[1]], axis=1)

    arow = pack(ab_re, ab_im)
    acol = arow.reshape(g, 4 * p, 1)
    a16 = pack(a16_re, a16_im).reshape(g, 4 * p, 1)
    cmat = pack(c_re.astype(F32), c_im.astype(F32))
    bw = jnp.tile(pack(bb_re, bb_im), (1, 1, 1, SSM_BLOCK))
    wspec = pl.BlockSpec((gc, SSM_ROWS, SSM_ROWS), lambda i: (i, 0, 0))
    wshape = jax.ShapeDtypeStruct((g, SSM_ROWS, SSM_ROWS), BF16)
    wz, t0, cc = pl.pallas_call(
        functools.partial(_ssm_prep_kernel, gc=gc),
        grid=(g // gc,),
        in_specs=[pl.BlockSpec((gc, 4, p), lambda i: (i, 0, 0)),
                  pl.BlockSpec((gc, 4 * p, 1), lambda i: (i, 0, 0)),
                  pl.BlockSpec((gc, 4, hc, p), lambda i: (i, 0, 0, 0)),
                  pl.BlockSpec((gc, 4, p, SSM_ROWS), lambda i: (i, 0, 0, 0))],
        out_specs=[wspec, wspec, wspec],
        out_shape=[wshape, wshape, wshape],
        compiler_params=_params("parallel"),
        name="ssm_prep",
    )(arow, acol, cmat, bw)
    return wz, t0, cc, a16


def _shift(chunks, s, reverse, lane):
    nc = len(chunks)
    if s >= LANES:
        k = s // LANES
        if reverse:
            return [chunks[c + k] if c + k < nc else None for c in range(nc)]
        return [chunks[c - k] if c - k >= 0 else None for c in range(nc)]
    amt = (LANES - s) if reverse else s
    rolled = [None if ch is None else pltpu.roll(ch, amt, 1) for ch in chunks]
    own = (lane < LANES - s) if reverse else (lane >= s)
    out = []
    for c in range(nc):
        nb = c + 1 if reverse else c - 1
        other = rolled[nb] if 0 <= nb < nc else None
        mine = rolled[c]
        if mine is None and other is None:
            out.append(None)
        else:
            zero = jnp.zeros((SSM_STATE, LANES), F32)
            out.append(jnp.where(own, zero if mine is None else mine, zero if other is None else other))
    return out


def _gelu_tanh(x):
    return 0.5 * x * (1.0 + jnp.tanh(math.sqrt(2.0 / math.pi) * (x + 0.044715 * (x * x * x))))


def _ssm_kernel(u_ref, wz_ref, t0_ref, cc_ref, a_ref, d_ref, y_ref, *, gc, ntau):
    nc = ntau // LANES
    lane = lax.broadcasted_iota(jnp.int32, (SSM_STATE, LANES), 1)
    for gi in range(gc):
        rows = slice(SSM_GROUP * gi, SSM_GROUP * (gi + 1))
        xf = u_ref[:, rows, :].reshape(SSM_ROWS, ntau)
        x = xf.astype(BF16)
        z = jnp.dot(wz_ref[gi], x, preferred_element_type=F32)
        a = a_ref[gi]
        states = []
        for dirn in range(2):
            reverse = dirn == 1
            base = 2 * SSM_STATE * dirn
            zr = [z[base:base + SSM_STATE, LANES * c:LANES * (c + 1)] for c in range(nc)]
            zi = [z[base + SSM_STATE:base + 2 * SSM_STATE, LANES * c:LANES * (c + 1)] for c in range(nc)]
            ar = jnp.broadcast_to(a[base:base + SSM_STATE], (SSM_STATE, LANES))
            ai = jnp.broadcast_to(a[base + SSM_STATE:base + 2 * SSM_STATE], (SSM_STATE, LANES))
            s = 1
            while s < ntau:
                sr = _shift(zr, s, reverse, lane)
                si = _shift(zi, s, reverse, lane)
                zr = [zr[c] if sr[c] is None else zr[c] + (ar * sr[c] - ai * si[c]) for c in range(nc)]
                zi = [zi[c] if sr[c] is None else zi[c] + (ar * si[c] + ai * sr[c]) for c in range(nc)]
                ar, ai = ar * ar - ai * ai, 2.0 * (ar * ai)
                s *= 2
            zero = jnp.zeros((SSM_STATE, LANES), F32)
            for part in (zr, zi):
                ex = _shift(part, 1, reverse, lane)
                states.append(jnp.concatenate([zero if e is None else e for e in ex], axis=1))
        h = jnp.concatenate(states, axis=0).astype(BF16)
        y = jnp.dot(t0_ref[gi], x, preferred_element_type=F32) + jnp.dot(cc_ref[gi], h, preferred_element_type=F32)
        y = _gelu_tanh(y + d_ref[gi] * xf)
        y_ref[:, rows, :] = y.astype(y_ref.dtype).reshape(SSM_BLOCK, SSM_GROUP, ntau)


def _ssm(u_t, wz, t0, cc, a16, d_rows, gc=4):
    b, _, _, ntau = u_t.shape
    wspec = pl.BlockSpec((gc, SSM_ROWS, SSM_ROWS), lambda bb, g: (g, 0, 0))
    vspec = pl.BlockSpec((gc, SSM_ROWS, 1), lambda bb, g: (g, 0, 0))
    uspec = pl.BlockSpec((None, SSM_BLOCK, SSM_GROUP * gc, ntau), lambda bb, g: (bb, 0, g, 0))
    return pl.pallas_call(
        functools.partial(_ssm_kernel, gc=gc, ntau=ntau),
        grid=(b, SSM_GROUPS // gc),
        in_specs=[uspec, wspec, wspec, wspec, vspec, vspec],
        out_specs=uspec,
        out_shape=jax.ShapeDtypeStruct(u_t.shape, BF16),
        compiler_params=_params("parallel", "parallel"),
        name="ssm",
    )(u_t, wz, t0, cc, a16, d_rows)


def _glu_kernel(y_ref, w1_ref, w2_ref, gb_ref, o_ref):
    yt = y_ref[...].T
    gl1 = jnp.dot(yt, w1_ref[...], preferred_element_type=F32)
    gl2 = jnp.dot(yt, w2_ref[...], preferred_element_type=F32)
    gb = gb_ref[...]
    o_ref[...] = (gl1 * jax.nn.sigmoid(gl2) * (gb * jax.nn.sigmoid(gb))).astype(o_ref.dtype)


def _glu(y_t, w_glu, p2, tmt=256, tn=1024):
    b, _, _, ntau = y_t.shape
    nj = SSM_WIDTH // tn
    p2v = p2.reshape(b, SSM_BLOCK, ntau, P2_WIDTH)
    out = pl.pallas_call(
        _glu_kernel,
        grid=(nj, b, SSM_BLOCK, ntau // tmt),
        in_specs=[
            pl.BlockSpec((None, None, SSM_WIDTH, tmt), lambda j, bb, i, tt: (bb, i, 0, tt)),
            pl.BlockSpec((SSM_WIDTH, tn), lambda j, bb, i, tt: (0, j)),
            pl.BlockSpec((SSM_WIDTH, tn), lambda j, bb, i, tt: (0, nj + j)),
            pl.BlockSpec((None, None, tmt, tn), lambda j, bb, i, tt: (bb, i, tt, j)),
        ],
        out_specs=pl.BlockSpec((None, None, tmt, tn), lambda j, bb, i, tt: (bb, i, tt, j)),
        out_shape=jax.ShapeDtypeStruct((b, SSM_BLOCK, ntau, SSM_WIDTH), BF16),
        compiler_params=_params("parallel", "parallel", "parallel", "parallel"),
        name="glu",
    )(y_t, w_glu, w_glu, p2v)
    return out.reshape(b * SSM_BLOCK * ntau, SSM_WIDTH)


def _merge_kernel(a_ref, y_ref, wa_ref, wb_ref, ma_ref, mb_ref, unperm_ref, o_ref):
    rows = o_ref.shape[0]
    za = jnp.dot(a_ref[...].reshape(rows, ATTN_WIDTH), wa_ref[...], preferred_element_type=F32)
    zb = jnp.dot(y_ref[...].reshape(rows, SSM_WIDTH), wb_ref[...], preferred_element_type=F32)
    ma = ma_ref[...].reshape(rows, -1)
    mb = mb_ref[...].reshape(rows, -1)
    merged = (jax.nn.sigmoid(ma) * za + jax.nn.sigmoid(mb) * zb).astype(BF16)
    o_ref[...] = jnp.dot(unperm_ref[...], merged, preferred_element_type=F32).astype(o_ref.dtype)


def _step_permutation(tm):
    n = tm // SSM_BLOCK
    r = jnp.arange(tm)
    src = SSM_BLOCK * (r % n) + r // n
    return (src[:, None] == jnp.arange(tm)[None, :]).astype(BF16)


def _merge(a, y2, w_attn_proj, w_ssm_proj, p2, b, s, tm=512, tn=1024):
    ntau = s // SSM_BLOCK
    n = tm // SSM_BLOCK
    ma0 = (COL_MA - COL_GB) // tn
    mb0 = (COL_MB - COL_GB) // tn

    def steps(arr, width):
        return arr.reshape(b, SSM_BLOCK, ntau, width)

    return pl.pallas_call(
        _merge_kernel,
        grid=(D_MODEL // tn, b, s // tm),
        in_specs=[
            pl.BlockSpec((None, SSM_BLOCK, n, ATTN_WIDTH), lambda j, bb, t: (bb, 0, t, 0)),
            pl.BlockSpec((None, SSM_BLOCK, n, SSM_WIDTH), lambda j, bb, t: (bb, 0, t, 0)),
            pl.BlockSpec((ATTN_WIDTH, tn), lambda j, bb, t: (0, j)),
            pl.BlockSpec((SSM_WIDTH, tn), lambda j, bb, t: (0, j)),
            pl.BlockSpec((None, SSM_BLOCK, n, tn), lambda j, bb, t: (bb, 0, t, ma0 + j)),
            pl.BlockSpec((None, SSM_BLOCK, n, tn), lambda j, bb, t: (bb, 0, t, mb0 + j)),
            pl.BlockSpec((tm, tm), lambda j, bb, t: (0, 0)),
        ],
        out_specs=pl.BlockSpec((tm, tn), lambda j, bb, t: (bb * (s // tm) + t, j)),
        out_shape=jax.ShapeDtypeStruct((b * s, D_MODEL), BF16),
        compiler_params=_params("parallel", "parallel", "parallel"),
        name="merge",
    )(steps(a, ATTN_WIDTH), steps(y2, SSM_WIDTH), w_attn_proj, w_ssm_proj,
      steps(p2, P2_WIDTH), steps(p2, P2_WIDTH), _step_permutation(tm).T)


def _out_kernel(m_ref, w_ref, x_ref, o_ref):
    o_ref[...] = x_ref[...] + jnp.dot(m_ref[...], w_ref[...], preferred_element_type=F32)


def _out_proj(merged, w_out, x2, tm=512, tn=1024):
    t = merged.shape[0]
    return pl.pallas_call(
        _out_kernel,
        grid=(D_MODEL // tn, t // tm),
        in_specs=[
            pl.BlockSpec((tm, D_MODEL), lambda j, i: (i, 0)),
            pl.BlockSpec((D_MODEL, tn), lambda j, i: (0, j)),
            pl.BlockSpec((tm, tn), lambda j, i: (i, j)),
        ],
        out_specs=pl.BlockSpec((tm, tn), lambda j, i: (i, j)),
        out_shape=jax.ShapeDtypeStruct((t, D_MODEL), F32),
        compiler_params=_params("parallel", "parallel"),
        name="out_proj",
    )(merged, w_out, x2)


def _layer(x, w):
    b, s, d = x.shape
    h, h_steps = _prenorm(x, w["norm_gain"])
    p1 = _proj(h.reshape(b * s, d), w["w_in"], COL_Q, P1_WIDTH, w["qk_gain_row"], 2, "proj_attn")
    p2 = _proj(h_steps.reshape(b * s, d), w["w_in"], COL_GB, P2_WIDTH, None, 0, "proj_gates")
    u_t = _proj_u(h_steps, w["wu_t"])
    a = _attention(p1, w["bias_tabs"], b, s).reshape(b * s, ATTN_WIDTH)
    y_t = _ssm(u_t, w["wz"], w["t0"], w["cc"], w["a16"], w["d_rows"])
    y2 = _glu(y_t, w["w_glu"], p2)
    merged = _merge(a, y2, w["w_attn_proj"], w["w_ssm_proj"], p2, b, s)
    return _out_proj(merged, w["w_out"], x.reshape(b * s, d)).reshape(b, s, d)


def _prepare(rel_bias, norm_gain, w_in, q_gain, k_gain, lam_re, lam_im, log_dt, b_re, b_im, c_re, c_im,
             d_skip, w_glu, w_attn_proj, w_ssm_proj, w_out):
    wz, t0, cc, a16 = _ssm_weights(lam_re, lam_im, log_dt, b_re, b_im, c_re, c_im)
    gains = jnp.concatenate([jnp.tile(q_gain.astype(F32) * (HEAD_DIM ** -0.5), N_HEADS),
                             jnp.tile(k_gain.astype(F32), N_HEADS),
                             jnp.zeros((P1_WIDTH - 2 * ATTN_WIDTH,), F32)])
    d_rows = jnp.tile(d_skip.astype(F32).reshape(SSM_GROUPS, 1, SSM_GROUP), (1, SSM_BLOCK, 1))
    return {
        "norm_gain": norm_gain.astype(F32),
        "w_in": w_in.astype(BF16),
        "wu_t": w_in[:, COL_U:COL_U + SSM_WIDTH].T.astype(BF16),
        "qk_gain_row": gains.reshape(1, P1_WIDTH),
        "bias_tabs": _bias_tables(rel_bias),
        "wz": wz, "t0": t0, "cc": cc, "a16": a16,
        "d_rows": d_rows.reshape(SSM_GROUPS, SSM_ROWS, 1),
        "w_glu": w_glu.astype(BF16),
        "w_attn_proj": w_attn_proj.astype(BF16),
        "w_ssm_proj": w_ssm_proj.astype(BF16),
        "w_out": w_out.astype(BF16),
    }


def kernel(x_prompt, x_sample, rel_bias, norm_gain, w_in, q_gain, k_gain, lam_re, lam_im, log_dt, b_re, b_im,
           c_re, c_im, d_skip, w_glu, w_attn_proj, w_ssm_proj, w_out):
    y_prompt, y_sample = x_prompt, x_sample
    for l in range(norm_gain.shape[0]):
        w = _prepare(rel_bias, norm_gain[l], w_in[l], q_gain[l], k_gain[l], lam_re[l], lam_im[l], log_dt[l],
                     b_re[l], b_im[l], c_re[l], c_im[l], d_skip[l], w_glu[l], w_attn_proj[l], w_ssm_proj[l],
                     w_out[l])
        y_prompt = _layer(y_prompt, w)
        y_sample = _layer(y_sample, w)
    return (y_prompt, y_sample)
```

```python
import functools
import math

import jax
import jax.numpy as jnp
from jax import lax
from jax.experimental import pallas as pl
from jax.experimental.pallas import tpu as pltpu

F32 = jnp.float32
BF16 = jnp.bfloat16

D_MODEL = 4096
N_HEADS = 16
HEAD_DIM = 128
ATTN_WIDTH = N_HEADS * HEAD_DIM
SIDE_KEYS = 64
N_BUCKETS = 32
REL_MAX_DIST = 1024
NEG = -1e30
EPS = 1e-6
SSM_WIDTH = 2048
SSM_GROUP = 16
SSM_GROUPS = SSM_WIDTH // SSM_GROUP
SSM_STATE = 64
SSM_BLOCK = 16
SSM_ROWS = SSM_BLOCK * SSM_GROUP

COL_Q = 0
COL_K = ATTN_WIDTH
COL_V = 2 * ATTN_WIDTH
COL_GA = 3 * ATTN_WIDTH
COL_U = 4 * ATTN_WIDTH
COL_GB = COL_U + SSM_WIDTH
COL_MA = COL_GB + SSM_WIDTH
COL_MB = COL_MA + D_MODEL
P1_WIDTH = COL_U
P2_WIDTH = SSM_WIDTH + 2 * D_MODEL

LANES = 128
PLANES = SSM_BLOCK
ATTN_GEOM = ((1, 8, 24), (4, 16, 48), (16, 64, 192))
ATTN_ROWS = 128
HALO_ROWS = SIDE_KEYS

VMEM_LIMIT = 56 * 1024 * 1024


def _params(*sem):
    return pltpu.CompilerParams(dimension_semantics=sem, vmem_limit_bytes=VMEM_LIMIT)


def _prenorm_kernel(x_ref, g_ref, perm_ref, o_ref):
    x = x_ref[...]
    ms = jnp.mean(x * x, axis=-1, keepdims=True)
    h = (x * lax.rsqrt(ms + EPS) * g_ref[...]).astype(BF16)
    hp = jnp.dot(perm_ref[...], h, preferred_element_type=F32).astype(BF16)
    o_ref[...] = hp.reshape(o_ref.shape)


def _prenorm(x, gain, tm=512):
    b, s, d = x.shape
    return pl.pallas_call(
        _prenorm_kernel,
        grid=(b, s // tm),
        in_specs=[pl.BlockSpec((None, tm, d), lambda bb, t: (bb, t, 0)),
                  pl.BlockSpec((1, d), lambda bb, t: (0, 0)),
                  pl.BlockSpec((tm, tm), lambda bb, t: (0, 0))],
        out_specs=pl.BlockSpec((None, SSM_BLOCK, tm // SSM_BLOCK, d), lambda bb, t: (bb, 0, t, 0)),
        out_shape=jax.ShapeDtypeStruct((b, SSM_BLOCK, s // SSM_BLOCK, d), BF16),
        compiler_params=_params("parallel", "parallel"),
        name="prenorm",
    )(x, gain.reshape(1, d), _step_permutation(tm))


def _proj_kernel(h_ref, w_ref, g_ref, o_ref, *, n_norm_tiles):
    acc = jnp.dot(h_ref[...], w_ref[...], preferred_element_type=F32)
    j = pl.program_id(0)

    @pl.when(j < n_norm_tiles)
    def _():
        for hh in range(acc.shape[1] // HEAD_DIM):
            sl = slice(hh * HEAD_DIM, (hh + 1) * HEAD_DIM)
            blk = acc[:, sl]
            ms = jnp.mean(blk * blk, axis=-1, keepdims=True)
            o_ref[:, sl] = blk * lax.rsqrt(ms + EPS) * g_ref[:, sl]

    @pl.when(j >= n_norm_tiles)
    def _():
        o_ref[...] = acc


def _proj(h, w, col0, width, gain_row, n_norm_tiles, name, tm=512, tn=2048):
    t, k = h.shape
    if gain_row is None:
        gain_row = jnp.zeros((1, width), F32)
    return pl.pallas_call(
        functools.partial(_proj_kernel, n_norm_tiles=n_norm_tiles),
        grid=(width // tn, t // tm),
        in_specs=[
            pl.BlockSpec((tm, k), lambda j, i: (i, 0)),
            pl.BlockSpec((k, tn), lambda j, i: (0, col0 // tn + j)),
            pl.BlockSpec((1, tn), lambda j, i: (0, j)),
        ],
        out_specs=pl.BlockSpec((tm, tn), lambda j, i: (i, j)),
        out_shape=jax.ShapeDtypeStruct((t, width), F32),
        compiler_params=_params("parallel", "parallel"),
        name=name,
    )(h, w, gain_row)


def _proj_u_kernel(w_ref, h_ref, o_ref):
    o_ref[...] = lax.dot_general(w_ref[...], h_ref[...], (((1,), (1,)), ((), ())),
                                 preferred_element_type=F32)


def _proj_u(h_steps, wu_t, tmt=256):
    b, _, ntau, _ = h_steps.shape
    return pl.pallas_call(
        _proj_u_kernel,
        grid=(b, SSM_BLOCK, ntau // tmt),
        in_specs=[
            pl.BlockSpec((SSM_WIDTH, D_MODEL), lambda bb, i, tt: (0, 0)),
            pl.BlockSpec((None, None, tmt, D_MODEL), lambda bb, i, tt: (bb, i, tt, 0)),
        ],
        out_specs=pl.BlockSpec((None, None, SSM_WIDTH, tmt), lambda bb, i, tt: (bb, i, 0, tt)),
        out_shape=jax.ShapeDtypeStruct((b, SSM_BLOCK, SSM_WIDTH, ntau), F32),
        compiler_params=_params("parallel", "parallel", "parallel"),
        name="proj_u",
    )(wu_t, h_steps)


def _rel_bucket(rel):
    half = N_BUCKETS // 2
    exact = half // 2
    base = jnp.where(rel > 0, half, 0)
    n = jnp.abs(rel)
    nf = jnp.maximum(n, 1).astype(F32)
    large = exact + (jnp.log(nf / exact) / math.log(REL_MAX_DIST / exact) * (half - exact)).astype(jnp.int32)
    large = jnp.minimum(large, half - 1)
    return base + jnp.where(n < exact, n, large)


def _bias_tables(rel_bias):
    buckets = jnp.arange(N_BUCKETS)
    tabs = []
    for d, nq, nk in ATTN_GEOM:
        g = PLANES // d
        koff = (nk - nq) // 2
        aq = jnp.arange(g)[:, None, None, None]
        tq = jnp.arange(nq)[None, :, None, None]
        ak = jnp.arange(g)[None, None, :, None]
        tk = jnp.arange(nk)[None, None, None, :]
        rel = (g * ((tk - koff) - tq) + (ak - aq)).reshape(g * nq, g * nk)
        onehot = (_rel_bucket(rel * d)[..., None] == buckets).astype(F32)
        bias = jnp.einsum('qkb,bh->hqk', onehot, rel_bias.astype(F32), precision=lax.Precision.HIGHEST)
        bias = jnp.where((jnp.abs(rel) <= SIDE_KEYS)[None], bias, NEG)
        trow = jnp.arange(g * nk) % nk - koff
        before = (trow < 0)[None, None, :]
        after = (trow >= nq)[None, None, :]
        tabs.append(jnp.stack([bias, jnp.where(before, NEG, bias), jnp.where(after, NEG, bias)], axis=1))
    return tabs


def _attn_kernel(q_ref, kp_ref, kc_ref, kn_ref, vp_ref, vc_ref, vn_ref, ga_ref, t1_ref, t4_ref, t16_ref, o_ref,
                 kbuf, vbuf, obuf, lbuf):
    t = pl.program_id(2)
    first = t == 0
    last = t == pl.num_programs(2) - 1
    for buf, (p_ref, c_ref, n_ref) in ((kbuf, (kp_ref, kc_ref, kn_ref)), (vbuf, (vp_ref, vc_ref, vn_ref))):
        buf[:, 0:HALO_ROWS, :] = p_ref[...]
        buf[:, HALO_ROWS:HALO_ROWS + ATTN_ROWS, :] = c_ref[...]
        buf[:, HALO_ROWS + ATTN_ROWS:, :] = n_ref[...]

    for pi, ((d, nq, nk), tab_ref) in enumerate(zip(ATTN_GEOM, (t1_ref, t4_ref, t16_ref))):
        g = PLANES // d
        nblk = ATTN_ROWS // nq
        k0 = HALO_ROWS - (nk - nq) // 2

        def block(it, carry, pi=pi, d=d, nq=nq, nk=nk, g=g, nblk=nblk, k0=k0, tab_ref=tab_ref):
            r = it // nblk
            blk = it % nblk
            q0 = pl.multiple_of(blk * nq, nq)
            kk0 = pl.multiple_of(blk * nq + k0, 8)
            q = jnp.concatenate([q_ref[r + d * a, pl.ds(q0, nq), :] for a in range(g)], axis=0).astype(BF16)
            k = jnp.concatenate([kbuf[r + d * a, pl.ds(kk0, nk), :] for a in range(g)], axis=0).astype(BF16)
            v = jnp.concatenate([vbuf[r + d * a, pl.ds(kk0, nk), :] for a in range(g)], axis=0).astype(BF16)
            variant = jnp.where(first & (blk == 0), 1, jnp.where(last & (blk == nblk - 1), 2, 0))
            s = lax.dot_general(q, k, (((1,), (1,)), ((), ())), preferred_element_type=F32) + tab_ref[variant]
            m = jnp.max(s, axis=-1, keepdims=True)
            p = jnp.exp(s - m)
            den = jnp.sum(p, axis=-1, keepdims=True)
            o = jnp.dot(p.astype(BF16), v, preferred_element_type=F32) / den
            lse = jnp.broadcast_to(m + jnp.log(den), o.shape)
            for a in range(g):
                obuf[pi, r + d * a, pl.ds(q0, nq), :] = o[a * nq:(a + 1) * nq]
                lbuf[pi, r + d * a, pl.ds(q0, nq), :] = lse[a * nq:(a + 1) * nq]
            return carry

        lax.fori_loop(0, d * nblk, block, 0)

    def combine(i, carry):
        l0, l1, l2 = lbuf[0, i], lbuf[1, i], lbuf[2, i]
        mx = jnp.maximum(jnp.maximum(l0, l1), l2)
        e0, e1, e2 = jnp.exp(l0 - mx), jnp.exp(l1 - mx), jnp.exp(l2 - mx)
        o = (e0 * obuf[0, i] + e1 * obuf[1, i] + e2 * obuf[2, i]) / (e0 + e1 + e2)
        ga = ga_ref[i]
        o_ref[i] = (o * (ga * jax.nn.sigmoid(ga))).astype(o_ref.dtype)
        return carry

    lax.fori_loop(0, PLANES, combine, 0)


def _attention(p1, bias_tabs, b, s):
    nrow = s // PLANES
    p1 = p1.reshape(b, PLANES, nrow, P1_WIDTH)
    per = ATTN_ROWS // HALO_ROWS
    n_halo = nrow // HALO_ROWS

    def tile(col):
        return pl.BlockSpec((None, PLANES, ATTN_ROWS, HEAD_DIM), lambda bb, h, t: (bb, 0, t, col // HEAD_DIM + h))

    def prev(col):
        return pl.BlockSpec((None, PLANES, HALO_ROWS, HEAD_DIM),
                            lambda bb, h, t: (bb, 0, jnp.maximum(t * per - 1, 0), col // HEAD_DIM + h))

    def nxt(col):
        return pl.BlockSpec((None, PLANES, HALO_ROWS, HEAD_DIM),
                            lambda bb, h, t: (bb, 0, jnp.minimum((t + 1) * per, n_halo - 1), col // HEAD_DIM + h))

    def table(tab):
        return pl.BlockSpec((None,) + tab.shape[1:], lambda bb, h, t: (h, 0, 0, 0))

    halo_buf = pltpu.VMEM((PLANES, ATTN_ROWS + 2 * HALO_ROWS, HEAD_DIM), F32)
    part_buf = pltpu.VMEM((len(ATTN_GEOM), PLANES, ATTN_ROWS, HEAD_DIM), F32)
    return pl.pallas_call(
        _attn_kernel,
        grid=(b, N_HEADS, nrow // ATTN_ROWS),
        in_specs=[tile(COL_Q), prev(COL_K), tile(COL_K), nxt(COL_K), prev(COL_V), tile(COL_V), nxt(COL_V),
                  tile(COL_GA)] + [table(tab) for tab in bias_tabs],
        out_specs=pl.BlockSpec((None, PLANES, ATTN_ROWS, HEAD_DIM), lambda bb, h, t: (bb, 0, t, h)),
        out_shape=jax.ShapeDtypeStruct((b, PLANES, nrow, ATTN_WIDTH), BF16),
        scratch_shapes=[halo_buf, halo_buf, part_buf, part_buf],
        compiler_params=_params("parallel", "parallel", "parallel"),
        name="attention",
    )(p1, p1, p1, p1, p1, p1, p1, p1, *bias_tabs)


def _ssm_prep_kernel(arow_ref, acol_ref, c_ref, bw_ref, wz_ref, t0_ref, cc_ref, *, gc):
    nb, hc, p = SSM_BLOCK, SSM_GROUP, SSM_STATE
    blk_of_lane = lax.broadcasted_iota(jnp.int32, (hc, SSM_ROWS), 1) // hc
    blk_of_lane_p = lax.broadcasted_iota(jnp.int32, (p, SSM_ROWS), 1) // hc

    def body(gi, carry):
        lag_kernels = []
        for dirn in range(2):
            ar = arow_ref[gi, 2 * dirn:2 * dirn + 1, :]
            ai = arow_ref[gi, 2 * dirn + 1:2 * dirn + 2, :]
            cr = c_ref[gi, 2 * dirn]
            ci = c_ref[gi, 2 * dirn + 1]
            pr = jnp.ones((1, p), F32)
            pi = jnp.zeros((1, p), F32)
            ca_re, ca_im = [], []
            for _ in range(nb + 1):
                ca_re.append(cr * pr - ci * pi)
                ca_im.append(cr * pi + ci * pr)
                pr, pi = pr * ar - pi * ai, pr * ai + pi * ar
            bre = bw_ref[gi, 2 * dirn]
            bim = bw_ref[gi, 2 * dirn + 1]
            lag_kernels.append(
                jnp.dot(jnp.concatenate(ca_re[:nb], axis=0).astype(BF16), bre.astype(BF16),
                        preferred_element_type=F32)
                - jnp.dot(jnp.concatenate(ca_im[:nb], axis=0).astype(BF16), bim.astype(BF16),
                          preferred_element_type=F32))
            order = range(1, nb + 1) if dirn == 0 else range(nb, 0, -1)
            c0 = 2 * p * dirn
            cc_ref[gi, :, c0:c0 + p] = jnp.concatenate([ca_re[k] for k in order], axis=0).astype(BF16)
            cc_ref[gi, :, c0 + p:c0 + 2 * p] = (-jnp.concatenate([ca_im[k] for k in order], axis=0)).astype(BF16)
            acr = acol_ref[gi, c0:c0 + p, :]
            aci = acol_ref[gi, c0 + p:c0 + 2 * p, :]
            qr = jnp.ones((p, 1), F32)
            qi = jnp.zeros((p, 1), F32)
            powers = []
            for _ in range(nb):
                powers.append((qr, qi))
                qr, qi = qr * acr - qi * aci, qr * aci + qi * acr
            per = jnp.zeros((p, SSM_ROWS), F32)
            pei = jnp.zeros((p, SSM_ROWS), F32)
            for i in range(nb):
                e = nb - 1 - i if dirn == 0 else i
                per = jnp.where(blk_of_lane_p == i, powers[e][0], per)
                pei = jnp.where(blk_of_lane_p == i, powers[e][1], pei)
            wz_ref[gi, c0:c0 + p, :] = (per * bre - pei * bim).astype(BF16)
            wz_ref[gi, c0 + p:c0 + 2 * p, :] = (per * bim + pei * bre).astype(BF16)
        kf, kb = lag_kernels
        diag = kf[0:hc, :] + kb[0:hc, :]
        for j in range(nb):
            acc = jnp.zeros((hc, SSM_ROWS), F32)
            for i in range(nb):
                if i < j:
                    src = kf[hc * (j - i):hc * (j - i + 1), :]
                elif i > j:
                    src = kb[hc * (i - j):hc * (i - j + 1), :]
                else:
                    src = diag
                acc = jnp.where(blk_of_lane == i, src, acc)
            t0_ref[gi, hc * j:hc * (j + 1), :] = acc.astype(BF16)
        return carry

    lax.fori_loop(0, gc, body, 0)


def _ssm_weights(lam_re, lam_im, log_dt, b_re, b_im, c_re, c_im, gc=8):
    g, p, hc = SSM_GROUPS, SSM_STATE, SSM_GROUP
    dt = jnp.exp(log_dt.astype(F32))[..., None]
    lr = lam_re.astype(F32)
    li = lam_im.astype(F32)
    mag = jnp.exp(lr * dt)
    ab_re = mag * jnp.cos(li * dt)
    ab_im = mag * jnp.sin(li * dt)
    den = lr * lr + li * li
    f_re = ((ab_re - 1.0) * lr + ab_im * li) / den
    f_im = (ab_im * lr - (ab_re - 1.0) * li) / den
    br = b_re.astype(F32)
    bi = b_im.astype(F32)
    bb_re = f_re[..., None] * br - f_im[..., None] * bi
    bb_im = f_re[..., None] * bi + f_im[..., None] * br
    mag16 = jnp.exp(SSM_BLOCK * (lr * dt))
    a16_re = mag16 * jnp.cos(SSM_BLOCK * (li * dt))
    a16_im = mag16 * jnp.sin(SSM_BLOCK * (li * dt))

    def pack(re, im):
        return jnp.stack([re[0], im[0], re[1], im[1]], axis=1)

    arow = pack(ab_re, ab_im)
    acol = arow.reshape(g, 4 * p, 1)
    a16 = pack(a16_re, a16_im).reshape(g, 4 * p, 1)
    cmat = pack(c_re.astype(F32), c_im.astype(F32))
    bw = jnp.tile(pack(bb_re, bb_im), (1, 1, 1, SSM_BLOCK))
    wspec = pl.BlockSpec((gc, SSM_ROWS, SSM_ROWS), lambda i: (i, 0, 0))
    wshape = jax.ShapeDtypeStruct((g, SSM_ROWS, SSM_ROWS), BF16)
    wz, t0, cc = pl.pallas_call(
        functools.partial(_ssm_prep_kernel, gc=gc),
        grid=(g // gc,),
        in_specs=[pl.BlockSpec((gc, 4, p), lambda i: (i, 0, 0)),
                  pl.BlockSpec((gc, 4 * p, 1), lambda i: (i, 0, 0)),
                  pl.BlockSpec((gc, 4, hc, p), lambda i: (i, 0, 0, 0)),
                  pl.BlockSpec((gc, 4, p, SSM_ROWS), lambda i: (i, 0, 0, 0))],
        out_specs=[wspec, wspec, wspec],
        out_shape=[wshape, wshape, wshape],
        compiler_params=_params("parallel"),
        name="ssm_prep",
    )(arow, acol, cmat, bw)
    return wz, t0, cc, a16


def _shift(chunks, s, reverse, lane):
    nc = len(chunks)
    if s >= LANES:
        k = s // LANES
        if reverse:
            return [chunks[c + k] if c + k < nc else None for c in range(nc)]
        return [chunks[c - k] if c - k >= 0 else None for c in range(nc)]
    amt = (LANES - s) if reverse else s
    rolled = [None if ch is None else pltpu.roll(ch, amt, 1) for ch in chunks]
    own = (lane < LANES - s) if reverse else (lane >= s)
    out = []
    for c in range(nc):
        nb = c + 1 if reverse else c - 1
        other = rolled[nb] if 0 <= nb < nc else None
        mine = rolled[c]
        if mine is None and other is None:
            out.append(None)
        else:
            zero = jnp.zeros((SSM_STATE, LANES), F32)
            out.append(jnp.where(own, zero if mine is None else mine, zero if other is None else other))
    return out


def _gelu_tanh(x):
    return 0.5 * x * (1.0 + jnp.tanh(math.sqrt(2.0 / math.pi) * (x + 0.044715 * (x * x * x))))


def _ssm_kernel(u_ref, wz_ref, t0_ref, cc_ref, a_ref, d_ref, y_ref, *, gc, ntau):
    nc = ntau // LANES
    lane = lax.broadcasted_iota(jnp.int32, (SSM_STATE, LANES), 1)
    for gi in range(gc):
        rows = slice(SSM_GROUP * gi, SSM_GROUP * (gi + 1))
        xf = u_ref[:, rows, :].reshape(SSM_ROWS, ntau)
        x = xf.astype(BF16)
        z = jnp.dot(wz_ref[gi], x, preferred_element_type=F32)
        a = a_ref[gi]
        states = []
        for dirn in range(2):
            reverse = dirn == 1
            base = 2 * SSM_STATE * dirn
            zr = [z[base:base + SSM_STATE, LANES * c:LANES * (c + 1)] for c in range(nc)]
            zi = [z[base + SSM_STATE:base + 2 * SSM_STATE, LANES * c:LANES * (c + 1)] for c in range(nc)]
            ar = jnp.broadcast_to(a[base:base + SSM_STATE], (SSM_STATE, LANES))
            ai = jnp.broadcast_to(a[base + SSM_STATE:base + 2 * SSM_STATE], (SSM_STATE, LANES))
            s = 1
            while s < ntau:
                sr = _shift(zr, s, reverse, lane)
                si = _shift(zi, s, reverse, lane)
                zr = [zr[c] if sr[c] is None else zr[c] + (ar * sr[c] - ai * si[c]) for c in range(nc)]
                zi = [zi[c] if sr[c] is None else zi[c] + (ar * si[c] + ai * sr[c]) for c in range(nc)]
                ar, ai = ar * ar - ai * ai, 2.0 * (ar * ai)
                s *= 2
            zero = jnp.zeros((SSM_STATE, LANES), F32)
            for part in (zr, zi):
                ex = _shift(part, 1, reverse, lane)
                states.append(jnp.concatenate([zero if e is None else e for e in ex], axis=1))
        h = jnp.concatenate(states, axis=0).astype(BF16)
        y = jnp.dot(t0_ref[gi], x, preferred_element_type=F32) + jnp.dot(cc_ref[gi], h, preferred_element_type=F32)
        y = _gelu_tanh(y + d_ref[gi] * xf)
        y_ref[:, rows, :] = y.astype(y_ref.dtype).reshape(SSM_BLOCK, SSM_GROUP, ntau)


def _ssm(u_t, wz, t0, cc, a16, d_rows, gc=4):
    b, _, _, ntau = u_t.shape
    wspec = pl.BlockSpec((gc, SSM_ROWS, SSM_ROWS), lambda bb, g: (g, 0, 0))
    vspec = pl.BlockSpec((gc, SSM_ROWS, 1), lambda bb, g: (g, 0, 0))
    uspec = pl.BlockSpec((None, SSM_BLOCK, SSM_GROUP * gc, ntau), lambda bb, g: (bb, 0, g, 0))
    return pl.pallas_call(
        functools.partial(_ssm_kernel, gc=gc, ntau=ntau),
        grid=(b, SSM_GROUPS // gc),
        in_specs=[uspec, wspec, wspec, wspec, vspec, vspec],
        out_specs=uspec,
        out_shape=jax.ShapeDtypeStruct(u_t.shape, BF16),
        compiler_params=_params("parallel", "parallel"),
        name="ssm",
    )(u_t, wz, t0, cc, a16, d_rows)


def _glu_kernel(y_ref, w1_ref, w2_ref, gb_ref, o_ref):
    yt = y_ref[...].T
    gl1 = jnp.dot(yt, w1_ref[...], preferred_element_type=F32)
    gl2 = jnp.dot(yt, w2_ref[...], preferred_element_type=F32)
    gb = gb_ref[...]
    o_ref[...] = (gl1 * jax.nn.sigmoid(gl2) * (gb * jax.nn.sigmoid(gb))).astype(o_ref.dtype)


def _glu(y_t, w_glu, p2, tmt=256, tn=2048):
    b, _, _, ntau = y_t.shape
    nj = SSM_WIDTH // tn
    p2v = p2.reshape(b, SSM_BLOCK, ntau, P2_WIDTH)
    out = pl.pallas_call(
        _glu_kernel,
        grid=(nj, b, SSM_BLOCK, ntau // tmt),
        in_specs=[
            pl.BlockSpec((None, None, SSM_WIDTH, tmt), lambda j, bb, i, tt: (bb, i, 0, tt)),
            pl.BlockSpec((SSM_WIDTH, tn), lambda j, bb, i, tt: (0, j)),
            pl.BlockSpec((SSM_WIDTH, tn), lambda j, bb, i, tt: (0, nj + j)),
            pl.BlockSpec((None, None, tmt, tn), lambda j, bb, i, tt: (bb, i, tt, j)),
        ],
        out_specs=pl.BlockSpec((None, None, tmt, tn), lambda j, bb, i, tt: (bb, i, tt, j)),
        out_shape=jax.ShapeDtypeStruct((b, SSM_BLOCK, ntau, SSM_WIDTH), BF16),
        compiler_params=_params("parallel", "parallel", "parallel", "parallel"),
        name="glu",
    )(y_t, w_glu, w_glu, p2v)
    return out.reshape(b * SSM_BLOCK * ntau, SSM_WIDTH)


def _merge_kernel(a_ref, y_ref, wa_ref, wb_ref, ma_ref, mb_ref, unperm_ref, o_ref):
    rows = o_ref.shape[0]
    za = jnp.dot(a_ref[...].reshape(rows, ATTN_WIDTH), wa_ref[...], preferred_element_type=F32)
    zb = jnp.dot(y_ref[...].reshape(rows, SSM_WIDTH), wb_ref[...], preferred_element_type=F32)
    ma = ma_ref[...].reshape(rows, -1)
    mb = mb_ref[...].reshape(rows, -1)
    merged = (jax.nn.sigmoid(ma) * za + jax.nn.sigmoid(mb) * zb).astype(BF16)
    o_ref[...] = jnp.dot(unperm_ref[...], merged, preferred_element_type=F32).astype(o_ref.dtype)


def _step_permutation(tm):
    n = tm // SSM_BLOCK
    r = jnp.arange(tm)
    src = SSM_BLOCK * (r % n) + r // n
    return (src[:, None] == jnp.arange(tm)[None, :]).astype(BF16)


def _merge(a, y2, w_attn_proj, w_ssm_proj, p2, b, s, tm=512, tn=1024):
    ntau = s // SSM_BLOCK
    n = tm // SSM_BLOCK
    ma0 = (COL_MA - COL_GB) // tn
    mb0 = (COL_MB - COL_GB) // tn

    def steps(arr, width):
        return arr.reshape(b, SSM_BLOCK, ntau, width)

    return pl.pallas_call(
        _merge_kernel,
        grid=(D_MODEL // tn, b, s // tm),
        in_specs=[
            pl.BlockSpec((None, SSM_BLOCK, n, ATTN_WIDTH), lambda j, bb, t: (bb, 0, t, 0)),
            pl.BlockSpec((None, SSM_BLOCK, n, SSM_WIDTH), lambda j, bb, t: (bb, 0, t, 0)),
            pl.BlockSpec((ATTN_WIDTH, tn), lambda j, bb, t: (0, j)),
            pl.BlockSpec((SSM_WIDTH, tn), lambda j, bb, t: (0, j)),
            pl.BlockSpec((None, SSM_BLOCK, n, tn), lambda j, bb, t: (bb, 0, t, ma0 + j)),
            pl.BlockSpec((None, SSM_BLOCK, n, tn), lambda j, bb, t: (bb, 0, t, mb0 + j)),
            pl.BlockSpec((tm, tm), lambda j, bb, t: (0, 0)),
        ],
        out_specs=pl.BlockSpec((tm, tn), lambda j, bb, t: (bb * (s // tm) + t, j)),
        out_shape=jax.ShapeDtypeStruct((b * s, D_MODEL), BF16),
        compiler_params=_params("parallel", "parallel", "parallel"),
        name="merge",
    )(steps(a, ATTN_WIDTH), steps(y2, SSM_WIDTH), w_attn_proj, w_ssm_proj,
      steps(p2, P2_WIDTH), steps(p2, P2_WIDTH), _step_permutation(tm).T)


def _out_kernel(m_ref, w_ref, x_ref, o_ref):
    o_ref[...] = x_ref[...] + jnp.dot(m_ref[...], w_ref[...], preferred_element_type=F32)


def _out_proj(merged, w_out, x2, tm=512, tn=1024):
    t = merged.shape[0]
    return pl.pallas_call(
        _out_kernel,
        grid=(D_MODEL // tn, t // tm),
        in_specs=[
            pl.BlockSpec((tm, D_MODEL), lambda j, i: (i, 0)),
            pl.BlockSpec((D_MODEL, tn), lambda j, i: (0, j)),
            pl.BlockSpec((tm, tn), lambda j, i: (i, j)),
        ],
        out_specs=pl.BlockSpec((tm, tn), lambda j, i: (i, j)),
        out_shape=jax.ShapeDtypeStruct((t, D_MODEL), F32),
        compiler_params=_params("parallel", "parallel"),
        name="out_proj",
    )(merged, w_out, x2)


def _layer(x, w):
    b, s, d = x.shape
    h_steps = _prenorm(x, w["norm_gain"])
    h2 = h_steps.reshape(b * s, d)
    p1 = _proj(h2, w["w_in"], COL_Q, P1_WIDTH, w["qk_gain_row"], 2, "proj_attn")
    p2 = _proj(h2, w["w_in"], COL_GB, P2_WIDTH, None, 0, "proj_gates")
    u_t = _proj_u(h_steps, w["wu_t"])
    a = _attention(p1, w["bias_tabs"], b, s).reshape(b * s, ATTN_WIDTH)
    y_t = _ssm(u_t, w["wz"], w["t0"], w["cc"], w["a16"], w["d_rows"])
    y2 = _glu(y_t, w["w_glu"], p2)
    merged = _merge(a, y2, w["w_attn_proj"], w["w_ssm_proj"], p2, b, s)
    return _out_proj(merged, w["w_out"], x.reshape(b * s, d)).reshape(b, s, d)


def _prepare(rel_bias, norm_gain, w_in, q_gain, k_gain, lam_re, lam_im, log_dt, b_re, b_im, c_re, c_im,
             d_skip, w_glu, w_attn_proj, w_ssm_proj, w_out):
    wz, t0, cc, a16 = _ssm_weights(lam_re, lam_im, log_dt, b_re, b_im, c_re, c_im)
    gains = jnp.concatenate([jnp.tile(q_gain.astype(F32) * (HEAD_DIM ** -0.5), N_HEADS),
                             jnp.tile(k_gain.astype(F32), N_HEADS),
                             jnp.zeros((P1_WIDTH - 2 * ATTN_WIDTH,), F32)])
    d_rows = jnp.tile(d_skip.astype(F32).reshape(SSM_GROUPS, 1, SSM_GROUP), (1, SSM_BLOCK, 1))
    return {
        "norm_gain": norm_gain.astype(F32),
        "w_in": w_in.astype(BF16),
        "wu_t": w_in[:, COL_U:COL_U + SSM_WIDTH].T.astype(BF16),
        "qk_gain_row": gains.reshape(1, P1_WIDTH),
        "bias_tabs": _bias_tables(rel_bias),
        "wz": wz, "t0": t0, "cc": cc, "a16": a16,
        "d_rows": d_rows.reshape(SSM_GROUPS, SSM_ROWS, 1),
        "w_glu": w_glu.astype(BF16),
        "w_attn_proj": w_attn_proj.astype(BF16),
        "w_ssm_proj": w_ssm_proj.astype(BF16),
        "w_out": w_out.astype(BF16),
    }


def kernel(x_prompt, x_sample, rel_bias, norm_gain, w_in, q_gain, k_gain, lam_re, lam_im, log_dt, b_re, b_im,
           c_re, c_im, d_skip, w_glu, w_attn_proj, w_ssm_proj, w_out):
    y_prompt, y_sample = x_prompt, x_sample
    for l in range(norm_gain.shape[0]):
        w = _prepare(rel_bias, norm_gain[l], w_in[l], q_gain[l], k_gain[l], lam_re[l], lam_im[l], log_dt[l],
                     b_re[l], b_im[l], c_re[l], c_im[l], d_skip[l], w_glu[l], w_attn_proj[l], w_ssm_proj[l],
                     w_out[l])
        y_prompt = _layer(y_prompt, w)
        y_sample = _layer(y_sample, w)
    return (y_prompt, y_sample)
```

```python
import functools
import math

import jax
import jax.numpy as jnp
from jax import lax
from jax.experimental import pallas as pl
from jax.experimental.pallas import tpu as pltpu

F32 = jnp.float32
BF16 = jnp.bfloat16

D_MODEL = 4096
N_HEADS = 16
HEAD_DIM = 128
ATTN_WIDTH = N_HEADS * HEAD_DIM
SIDE_KEYS = 64
N_BUCKETS = 32
REL_MAX_DIST = 1024
NEG = -1e30
LOG2E = math.log2(math.e)
EPS = 1e-6
SSM_WIDTH = 2048
SSM_GROUP = 16
SSM_GROUPS = SSM_WIDTH // SSM_GROUP
SSM_STATE = 64
SSM_BLOCK = 16
SSM_ROWS = SSM_BLOCK * SSM_GROUP

COL_Q = 0
COL_K = ATTN_WIDTH
COL_V = 2 * ATTN_WIDTH
COL_GA = 3 * ATTN_WIDTH
COL_U = 4 * ATTN_WIDTH
COL_GB = COL_U + SSM_WIDTH
COL_MA = COL_GB + SSM_WIDTH
COL_MB = COL_MA + D_MODEL
P1_WIDTH = COL_U
P2_WIDTH = SSM_WIDTH + 2 * D_MODEL

LANES = 128
PLANES = SSM_BLOCK
ATTN_GEOM = ((1, 8, 24), (4, 16, 48), (16, 64, 192))
ATTN_BATCH = (4, 8, 8)
ATTN_ROWS = 128
HALO_ROWS = SIDE_KEYS

VMEM_LIMIT = 56 * 1024 * 1024


def _params(*sem):
    return pltpu.CompilerParams(dimension_semantics=sem, vmem_limit_bytes=VMEM_LIMIT)


def _prenorm_kernel(x_ref, g_ref, perm_ref, o_ref):
    x = x_ref[...]
    ms = jnp.mean(x * x, axis=-1, keepdims=True)
    h = (x * lax.rsqrt(ms + EPS) * g_ref[...]).astype(BF16)
    hp = jnp.dot(perm_ref[...], h, preferred_element_type=F32).astype(BF16)
    o_ref[...] = hp.reshape(o_ref.shape)


def _prenorm(x, gain, tm=512):
    b, s, d = x.shape
    return pl.pallas_call(
        _prenorm_kernel,
        grid=(b, s // tm),
        in_specs=[pl.BlockSpec((None, tm, d), lambda bb, t: (bb, t, 0)),
                  pl.BlockSpec((1, d), lambda bb, t: (0, 0)),
                  pl.BlockSpec((tm, tm), lambda bb, t: (0, 0))],
        out_specs=pl.BlockSpec((None, SSM_BLOCK, tm // SSM_BLOCK, d), lambda bb, t: (bb, 0, t, 0)),
        out_shape=jax.ShapeDtypeStruct((b, SSM_BLOCK, s // SSM_BLOCK, d), BF16),
        compiler_params=_params("parallel", "parallel"),
        name="prenorm",
    )(x, gain.reshape(1, d), _step_permutation(tm))


def _proj_kernel(h_ref, w_ref, g_ref, o_ref, *, n_norm_tiles):
    acc = jnp.dot(h_ref[...], w_ref[...], preferred_element_type=F32)
    j = pl.program_id(0)

    @pl.when(j < n_norm_tiles)
    def _():
        for hh in range(acc.shape[1] // HEAD_DIM):
            sl = slice(hh * HEAD_DIM, (hh + 1) * HEAD_DIM)
            blk = acc[:, sl]
            ms = jnp.mean(blk * blk, axis=-1, keepdims=True)
            o_ref[:, sl] = blk * lax.rsqrt(ms + EPS) * g_ref[:, sl]

    @pl.when(j >= n_norm_tiles)
    def _():
        o_ref[...] = acc


def _proj(h, w, col0, width, gain_row, n_norm_tiles, name, tm=512, tn=2048):
    t, k = h.shape
    if gain_row is None:
        gain_row = jnp.zeros((1, width), F32)
    return pl.pallas_call(
        functools.partial(_proj_kernel, n_norm_tiles=n_norm_tiles),
        grid=(width // tn, t // tm),
        in_specs=[
            pl.BlockSpec((tm, k), lambda j, i: (i, 0)),
            pl.BlockSpec((k, tn), lambda j, i: (0, col0 // tn + j)),
            pl.BlockSpec((1, tn), lambda j, i: (0, j)),
        ],
        out_specs=pl.BlockSpec((tm, tn), lambda j, i: (i, j)),
        out_shape=jax.ShapeDtypeStruct((t, width), F32),
        compiler_params=_params("parallel", "parallel"),
        name=name,
    )(h, w, gain_row)


def _proj_u_kernel(w_ref, h_ref, o_ref):
    o_ref[...] = lax.dot_general(w_ref[...], h_ref[...], (((1,), (1,)), ((), ())),
                                 preferred_element_type=F32)


def _proj_u(h_steps, wu_t, tmt=256):
    b, _, ntau, _ = h_steps.shape
    return pl.pallas_call(
        _proj_u_kernel,
        grid=(b, SSM_BLOCK, ntau // tmt),
        in_specs=[
            pl.BlockSpec((SSM_WIDTH, D_MODEL), lambda bb, i, tt: (0, 0)),
            pl.BlockSpec((None, None, tmt, D_MODEL), lambda bb, i, tt: (bb, i, tt, 0)),
        ],
        out_specs=pl.BlockSpec((None, None, SSM_WIDTH, tmt), lambda bb, i, tt: (bb, i, 0, tt)),
        out_shape=jax.ShapeDtypeStruct((b, SSM_BLOCK, SSM_WIDTH, ntau), F32),
        compiler_params=_params("parallel", "parallel", "parallel"),
        name="proj_u",
    )(wu_t, h_steps)


def _rel_bucket(rel):
    half = N_BUCKETS // 2
    exact = half // 2
    base = jnp.where(rel > 0, half, 0)
    n = jnp.abs(rel)
    nf = jnp.maximum(n, 1).astype(F32)
    large = exact + (jnp.log(nf / exact) / math.log(REL_MAX_DIST / exact) * (half - exact)).astype(jnp.int32)
    large = jnp.minimum(large, half - 1)
    return base + jnp.where(n < exact, n, large)


def _bias_tables(rel_bias):
    buckets = jnp.arange(N_BUCKETS)
    tabs = []
    for d, nq, nk in ATTN_GEOM:
        g = PLANES // d
        koff = (nk - nq) // 2
        aq = jnp.arange(g)[:, None, None, None]
        tq = jnp.arange(nq)[None, :, None, None]
        ak = jnp.arange(g)[None, None, :, None]
        tk = jnp.arange(nk)[None, None, None, :]
        rel = (g * ((tk - koff) - tq) + (ak - aq)).reshape(g * nq, g * nk)
        onehot = (_rel_bucket(rel * d)[..., None] == buckets).astype(F32)
        bias = jnp.einsum('qkb,bh->hqk', onehot, rel_bias.astype(F32), precision=lax.Precision.HIGHEST)
        bias = jnp.where((jnp.abs(rel) <= SIDE_KEYS)[None], bias, NEG)
        trow = jnp.arange(g * nk) % nk - koff
        before = (trow < 0)[None, None, :]
        after = (trow >= nq)[None, None, :]
        bias = bias * LOG2E
        tabs.append(jnp.stack([bias, jnp.where(before, NEG, bias), jnp.where(after, NEG, bias)], axis=1))
    return tabs


def _attn_kernel(q_ref, kp_ref, kc_ref, kn_ref, vp_ref, vc_ref, vn_ref, ga_ref, t1_ref, t4_ref, t16_ref, o_ref,
                 kbuf, vbuf, obuf, lbuf):
    t = pl.program_id(2)
    first = t == 0
    last = t == pl.num_programs(2) - 1
    for buf, (p_ref, c_ref, n_ref) in ((kbuf, (kp_ref, kc_ref, kn_ref)), (vbuf, (vp_ref, vc_ref, vn_ref))):
        buf[:, 0:HALO_ROWS, :] = p_ref[...]
        buf[:, HALO_ROWS:HALO_ROWS + ATTN_ROWS, :] = c_ref[...]
        buf[:, HALO_ROWS + ATTN_ROWS:, :] = n_ref[...]

    for pi, ((d, nq, nk), tab_ref) in enumerate(zip(ATTN_GEOM, (t1_ref, t4_ref, t16_ref))):
        g = PLANES // d
        nblk = ATTN_ROWS // nq
        k0 = HALO_ROWS - (nk - nq) // 2
        gsz = ATTN_BATCH[pi]

        def rows_of(ref, r, start, n):
            return jnp.concatenate([ref[r + d * a, start:start + n, :] for a in range(g)], axis=0).astype(BF16)

        def bias_of(blk):
            if blk == 0:
                return jnp.where(first, tab_ref[1], tab_ref[0])
            if blk == nblk - 1:
                return jnp.where(last, tab_ref[2], tab_ref[0])
            return tab_ref[0]

        blocks = [(r, blk) for r in range(d) for blk in range(nblk)]
        for g0 in range(0, len(blocks), gsz):
            grp = blocks[g0:g0 + gsz]
            q = jnp.stack([rows_of(q_ref, r, blk * nq, nq) for r, blk in grp])
            k = jnp.stack([rows_of(kbuf, r, blk * nq + k0, nk) for r, blk in grp])
            v = jnp.stack([rows_of(vbuf, r, blk * nq + k0, nk) for r, blk in grp])
            if all(0 < blk < nblk - 1 for _, blk in grp):
                bias = tab_ref[0][None]
            else:
                bias = jnp.stack([bias_of(blk) for _, blk in grp])
            s = jnp.einsum('bqd,bkd->bqk', q, k, preferred_element_type=F32) + bias
            m = jnp.max(s, axis=-1, keepdims=True)
            p = jnp.exp2(s - m)
            den = jnp.sum(p, axis=-1, keepdims=True)
            o = jnp.einsum('bqk,bkd->bqd', p.astype(BF16), v, preferred_element_type=F32) / den
            lse = jnp.broadcast_to(m + jnp.log2(den), o.shape)
            for bi, (r, blk) in enumerate(grp):
                for a in range(g):
                    rows = slice(blk * nq, (blk + 1) * nq)
                    obuf[pi, r + d * a, rows, :] = o[bi, a * nq:(a + 1) * nq]
                    lbuf[pi, r + d * a, rows, :] = lse[bi, a * nq:(a + 1) * nq]

    for i in range(PLANES):
        l0, l1, l2 = lbuf[0, i], lbuf[1, i], lbuf[2, i]
        mx = jnp.maximum(jnp.maximum(l0, l1), l2)
        e0, e1, e2 = jnp.exp2(l0 - mx), jnp.exp2(l1 - mx), jnp.exp2(l2 - mx)
        o = (e0 * obuf[0, i] + e1 * obuf[1, i] + e2 * obuf[2, i]) / (e0 + e1 + e2)
        ga = ga_ref[i]
        o_ref[i] = (o * (ga * jax.nn.sigmoid(ga))).astype(o_ref.dtype)


def _attention(p1, bias_tabs, b, s):
    nrow = s // PLANES
    p1 = p1.reshape(b, PLANES, nrow, P1_WIDTH)
    per = ATTN_ROWS // HALO_ROWS
    n_halo = nrow // HALO_ROWS

    def tile(col):
        return pl.BlockSpec((None, PLANES, ATTN_ROWS, HEAD_DIM), lambda bb, h, t: (bb, 0, t, col // HEAD_DIM + h))

    def prev(col):
        return pl.BlockSpec((None, PLANES, HALO_ROWS, HEAD_DIM),
                            lambda bb, h, t: (bb, 0, jnp.maximum(t * per - 1, 0), col // HEAD_DIM + h))

    def nxt(col):
        return pl.BlockSpec((None, PLANES, HALO_ROWS, HEAD_DIM),
                            lambda bb, h, t: (bb, 0, jnp.minimum((t + 1) * per, n_halo - 1), col // HEAD_DIM + h))

    def table(tab):
        return pl.BlockSpec((None,) + tab.shape[1:], lambda bb, h, t: (h, 0, 0, 0))

    halo_buf = pltpu.VMEM((PLANES, ATTN_ROWS + 2 * HALO_ROWS, HEAD_DIM), F32)
    part_buf = pltpu.VMEM((len(ATTN_GEOM), PLANES, ATTN_ROWS, HEAD_DIM), F32)
    return pl.pallas_call(
        _attn_kernel,
        grid=(b, N_HEADS, nrow // ATTN_ROWS),
        in_specs=[tile(COL_Q), prev(COL_K), tile(COL_K), nxt(COL_K), prev(COL_V), tile(COL_V), nxt(COL_V),
                  tile(COL_GA)] + [table(tab) for tab in bias_tabs],
        out_specs=pl.BlockSpec((None, PLANES, ATTN_ROWS, HEAD_DIM), lambda bb, h, t: (bb, 0, t, h)),
        out_shape=jax.ShapeDtypeStruct((b, PLANES, nrow, ATTN_WIDTH), BF16),
        scratch_shapes=[halo_buf, halo_buf, part_buf, part_buf],
        compiler_params=_params("parallel", "parallel", "parallel"),
        name="attention",
    )(p1, p1, p1, p1, p1, p1, p1, p1, *bias_tabs)


def _ssm_prep_kernel(arow_ref, acol_ref, c_ref, bw_ref, wz_ref, t0_ref, cc_ref, *, gc):
    nb, hc, p = SSM_BLOCK, SSM_GROUP, SSM_STATE
    blk_of_lane = lax.broadcasted_iota(jnp.int32, (hc, SSM_ROWS), 1) // hc
    blk_of_lane_p = lax.broadcasted_iota(jnp.int32, (p, SSM_ROWS), 1) // hc

    def body(gi, carry):
        lag_kernels = []
        for dirn in range(2):
            ar = arow_ref[gi, 2 * dirn:2 * dirn + 1, :]
            ai = arow_ref[gi, 2 * dirn + 1:2 * dirn + 2, :]
            cr = c_ref[gi, 2 * dirn]
            ci = c_ref[gi, 2 * dirn + 1]
            pr = jnp.ones((1, p), F32)
            pi = jnp.zeros((1, p), F32)
            ca_re, ca_im = [], []
            for _ in range(nb + 1):
                ca_re.append(cr * pr - ci * pi)
                ca_im.append(cr * pi + ci * pr)
                pr, pi = pr * ar - pi * ai, pr * ai + pi * ar
            bre = bw_ref[gi, 2 * dirn]
            bim = bw_ref[gi, 2 * dirn + 1]
            lag_kernels.append(
                jnp.dot(jnp.concatenate(ca_re[:nb], axis=0).astype(BF16), bre.astype(BF16),
                        preferred_element_type=F32)
                - jnp.dot(jnp.concatenate(ca_im[:nb], axis=0).astype(BF16), bim.astype(BF16),
                          preferred_element_type=F32))
            order = range(1, nb + 1) if dirn == 0 else range(nb, 0, -1)
            c0 = 2 * p * dirn
            cc_ref[gi, :, c0:c0 + p] = jnp.concatenate([ca_re[k] for k in order], axis=0).astype(BF16)
            cc_ref[gi, :, c0 + p:c0 + 2 * p] = (-jnp.concatenate([ca_im[k] for k in order], axis=0)).astype(BF16)
            acr = acol_ref[gi, c0:c0 + p, :]
            aci = acol_ref[gi, c0 + p:c0 + 2 * p, :]
            qr = jnp.ones((p, 1), F32)
            qi = jnp.zeros((p, 1), F32)
            powers = []
            for _ in range(nb):
                powers.append((qr, qi))
                qr, qi = qr * acr - qi * aci, qr * aci + qi * acr
            per = jnp.zeros((p, SSM_ROWS), F32)
            pei = jnp.zeros((p, SSM_ROWS), F32)
            for i in range(nb):
                e = nb - 1 - i if dirn == 0 else i
                per = jnp.where(blk_of_lane_p == i, powers[e][0], per)
                pei = jnp.where(blk_of_lane_p == i, powers[e][1], pei)
            wz_ref[gi, c0:c0 + p, :] = (per * bre - pei * bim).astype(BF16)
            wz_ref[gi, c0 + p:c0 + 2 * p, :] = (per * bim + pei * bre).astype(BF16)
        kf, kb = lag_kernels
        diag = kf[0:hc, :] + kb[0:hc, :]
        for j in range(nb):
            acc = jnp.zeros((hc, SSM_ROWS), F32)
            for i in range(nb):
                if i < j:
                    src = kf[hc * (j - i):hc * (j - i + 1), :]
                elif i > j:
                    src = kb[hc * (i - j):hc * (i - j + 1), :]
                else:
                    src = diag
                acc = jnp.where(blk_of_lane == i, src, acc)
            t0_ref[gi, hc * j:hc * (j + 1), :] = acc.astype(BF16)
        return carry

    lax.fori_loop(0, gc, body, 0)


def _ssm_weights(lam_re, lam_im, log_dt, b_re, b_im, c_re, c_im, gc=8):
    g, p, hc = SSM_GROUPS, SSM_STATE, SSM_GROUP
    dt = jnp.exp(log_dt.astype(F32))[..., None]
    lr = lam_re.astype(F32)
    li = lam_im.astype(F32)
    mag = jnp.exp(lr * dt)
    ab_re = mag * jnp.cos(li * dt)
    ab_im = mag * jnp.sin(li * dt)
    den = lr * lr + li * li
    f_re = ((ab_re - 1.0) * lr + ab_im * li) / den
    f_im = (ab_im * lr - (ab_re - 1.0) * li) / den
    br = b_re.astype(F32)
    bi = b_im.astype(F32)
    bb_re = f_re[..., None] * br - f_im[..., None] * bi
    bb_im = f_re[..., None] * bi + f_im[..., None] * br
    mag16 = jnp.exp(SSM_BLOCK * (lr * dt))
    a16_re = mag16 * jnp.cos(SSM_BLOCK * (li * dt))
    a16_im = mag16 * jnp.sin(SSM_BLOCK * (li * dt))

    def pack(re, im):
        return jnp.stack([re[0], im[0], re[1], im[1]], axis=1)

    arow = pack(ab_re, ab_im)
    acol = arow.reshape(g, 4 * p, 1)
    a16 = pack(a16_re, a16_im).reshape(g, 4 * p, 1)
    cmat = pack(c_re.astype(F32), c_im.astype(F32))
    bw = jnp.tile(pack(bb_re, bb_im), (1, 1, 1, SSM_BLOCK))
    wspec = pl.BlockSpec((gc, SSM_ROWS, SSM_ROWS), lambda i: (i, 0, 0))
    wshape = jax.ShapeDtypeStruct((g, SSM_ROWS, SSM_ROWS), BF16)
    wz, t0, cc = pl.pallas_call(
        functools.partial(_ssm_prep_kernel, gc=gc),
        grid=(g // gc,),
        in_specs=[pl.BlockSpec((gc, 4, p), lambda i: (i, 0, 0)),
                  pl.BlockSpec((gc, 4 * p, 1), lambda i: (i, 0, 0)),
                  pl.BlockSpec((gc, 4, hc, p), lambda i: (i, 0, 0, 0)),
                  pl.BlockSpec((gc, 4, p, SSM_ROWS), lambda i: (i, 0, 0, 0))],
        out_specs=[wspec, wspec, wspec],
        out_shape=[wshape, wshape, wshape],
        compiler_params=_params("parallel"),
        name="ssm_prep",
    )(arow, acol, cmat, bw)
    return wz, t0, cc, a16


def _shift(chunks, s, reverse, lane):
    nc = len(chunks)
    if s >= LANES:
        k = s // LANES
        if reverse:
            return [chunks[c + k] if c + k < nc else None for c in range(nc)]
        return [chunks[c - k] if c - k >= 0 else None for c in range(nc)]
    amt = (LANES - s) if reverse else s
    rolled = [None if ch is None else pltpu.roll(ch, amt, 1) for ch in chunks]
    own = (lane < LANES - s) if reverse else (lane >= s)
    out = []
    for c in range(nc):
        nb = c + 1 if reverse else c - 1
        other = rolled[nb] if 0 <= nb < nc else None
        mine = rolled[c]
        if mine is None and other is None:
            out.append(None)
        else:
            zero = jnp.zeros((SSM_STATE, LANES), F32)
            out.append(jnp.where(own, zero if mine is None else mine, zero if other is None else other))
    return out


def _gelu_tanh(x):
    return 0.5 * x * (1.0 + jnp.tanh(math.sqrt(2.0 / math.pi) * (x + 0.044715 * (x * x * x))))


def _ssm_kernel(u_ref, wz_ref, t0_ref, cc_ref, a_ref, d_ref, y_ref, *, gc, ntau):
    nc = ntau // LANES
    lane = lax.broadcasted_iota(jnp.int32, (SSM_STATE, LANES), 1)
    for gi in range(gc):
        rows = slice(SSM_GROUP * gi, SSM_GROUP * (gi + 1))
        xf = u_ref[:, rows, :].reshape(SSM_ROWS, ntau)
        x = xf.astype(BF16)
        z = jnp.dot(wz_ref[gi], x, preferred_element_type=F32)
        a = a_ref[gi]
        states = []
        for dirn in range(2):
            reverse = dirn == 1
            base = 2 * SSM_STATE * dirn
            zr = [z[base:base + SSM_STATE, LANES * c:LANES * (c + 1)] for c in range(nc)]
            zi = [z[base + SSM_STATE:base + 2 * SSM_STATE, LANES * c:LANES * (c + 1)] for c in range(nc)]
            ar = jnp.broadcast_to(a[base:base + SSM_STATE], (SSM_STATE, LANES))
            ai = jnp.broadcast_to(a[base + SSM_STATE:base + 2 * SSM_STATE], (SSM_STATE, LANES))
            s = 1
            while s < ntau:
                sr = _shift(zr, s, reverse, lane)
                si = _shift(zi, s, reverse, lane)
                zr = [zr[c] if sr[c] is None else zr[c] + (ar * sr[c] - ai * si[c]) for c in range(nc)]
                zi = [zi[c] if sr[c] is None else zi[c] + (ar * si[c] + ai * sr[c]) for c in range(nc)]
                ar, ai = ar * ar - ai * ai, 2.0 * (ar * ai)
                s *= 2
            zero = jnp.zeros((SSM_STATE, LANES), F32)
            for part in (zr, zi):
                ex = _shift(part, 1, reverse, lane)
                states.append(jnp.concatenate([zero if e is None else e for e in ex], axis=1))
        h = jnp.concatenate(states, axis=0).astype(BF16)
        y = jnp.dot(t0_ref[gi], x, preferred_element_type=F32) + jnp.dot(cc_ref[gi], h, preferred_element_type=F32)
        y = _gelu_tanh(y + d_ref[gi] * xf)
        y_ref[:, rows, :] = y.astype(y_ref.dtype).reshape(SSM_BLOCK, SSM_GROUP, ntau)


def _ssm(u_t, wz, t0, cc, a16, d_rows, gc=4):
    b, _, _, ntau = u_t.shape
    wspec = pl.BlockSpec((gc, SSM_ROWS, SSM_ROWS), lambda bb, g: (g, 0, 0))
    vspec = pl.BlockSpec((gc, SSM_ROWS, 1), lambda bb, g: (g, 0, 0))
    uspec = pl.BlockSpec((None, SSM_BLOCK, SSM_GROUP * gc, ntau), lambda bb, g: (bb, 0, g, 0))
    return pl.pallas_call(
        functools.partial(_ssm_kernel, gc=gc, ntau=ntau),
        grid=(b, SSM_GROUPS // gc),
        in_specs=[uspec, wspec, wspec, wspec, vspec, vspec],
        out_specs=uspec,
        out_shape=jax.ShapeDtypeStruct(u_t.shape, BF16),
        compiler_params=_params("parallel", "parallel"),
        name="ssm",
    )(u_t, wz, t0, cc, a16, d_rows)


def _glu_kernel(y_ref, w1_ref, w2_ref, gb_ref, o_ref):
    yt = y_ref[...].T
    gl1 = jnp.dot(yt, w1_ref[...], preferred_element_type=F32)
    gl2 = jnp.dot(yt, w2_ref[...], preferred_element_type=F32)
    gb = gb_ref[...]
    o_ref[...] = (gl1 * jax.nn.sigmoid(gl2) * (gb * jax.nn.sigmoid(gb))).astype(o_ref.dtype)


def _glu(y_t, w_glu, p2, tmt=256, tn=2048):
    b, _, _, ntau = y_t.shape
    nj = SSM_WIDTH // tn
    p2v = p2.reshape(b, SSM_BLOCK, ntau, P2_WIDTH)
    out = pl.pallas_call(
        _glu_kernel,
        grid=(nj, b, SSM_BLOCK, ntau // tmt),
        in_specs=[
            pl.BlockSpec((None, None, SSM_WIDTH, tmt), lambda j, bb, i, tt: (bb, i, 0, tt)),
            pl.BlockSpec((SSM_WIDTH, tn), lambda j, bb, i, tt: (0, j)),
            pl.BlockSpec((SSM_WIDTH, tn), lambda j, bb, i, tt: (0, nj + j)),
            pl.BlockSpec((None, None, tmt, tn), lambda j, bb, i, tt: (bb, i, tt, j)),
        ],
        out_specs=pl.BlockSpec((None, None, tmt, tn), lambda j, bb, i, tt: (bb, i, tt, j)),
        out_shape=jax.ShapeDtypeStruct((b, SSM_BLOCK, ntau, SSM_WIDTH), BF16),
        compiler_params=_params("parallel", "parallel", "parallel", "parallel"),
        name="glu",
    )(y_t, w_glu, w_glu, p2v)
    return out.reshape(b * SSM_BLOCK * ntau, SSM_WIDTH)


def _merge_kernel(a_ref, y_ref, wa_ref, wb_ref, ma_ref, mb_ref, unperm_ref, o_ref):
    rows = o_ref.shape[0]
    za = jnp.dot(a_ref[...].reshape(rows, ATTN_WIDTH), wa_ref[...], preferred_element_type=F32)
    zb = jnp.dot(y_ref[...].reshape(rows, SSM_WIDTH), wb_ref[...], preferred_element_type=F32)
    ma = ma_ref[...].reshape(rows, -1)
    mb = mb_ref[...].reshape(rows, -1)
    merged = (jax.nn.sigmoid(ma) * za + jax.nn.sigmoid(mb) * zb).astype(BF16)
    o_ref[...] = jnp.dot(unperm_ref[...], merged, preferred_element_type=F32).astype(o_ref.dtype)


def _step_permutation(tm):
    n = tm // SSM_BLOCK
    r = jnp.arange(tm)
    src = SSM_BLOCK * (r % n) + r // n
    return (src[:, None] == jnp.arange(tm)[None, :]).astype(BF16)


def _merge(a, y2, w_attn_proj, w_ssm_proj, p2, b, s, tm=512, tn=1024):
    ntau = s // SSM_BLOCK
    n = tm // SSM_BLOCK
    ma0 = (COL_MA - COL_GB) // tn
    mb0 = (COL_MB - COL_GB) // tn

    def steps(arr, width):
        return arr.reshape(b, SSM_BLOCK, ntau, width)

    return pl.pallas_call(
        _merge_kernel,
        grid=(D_MODEL // tn, b, s // tm),
        in_specs=[
            pl.BlockSpec((None, SSM_BLOCK, n, ATTN_WIDTH), lambda j, bb, t: (bb, 0, t, 0)),
            pl.BlockSpec((None, SSM_BLOCK, n, SSM_WIDTH), lambda j, bb, t: (bb, 0, t, 0)),
            pl.BlockSpec((ATTN_WIDTH, tn), lambda j, bb, t: (0, j)),
            pl.BlockSpec((SSM_WIDTH, tn), lambda j, bb, t: (0, j)),
            pl.BlockSpec((None, SSM_BLOCK, n, tn), lambda j, bb, t: (bb, 0, t, ma0 + j)),
            pl.BlockSpec((None, SSM_BLOCK, n, tn), lambda j, bb, t: (bb, 0, t, mb0 + j)),
            pl.BlockSpec((tm, tm), lambda j, bb, t: (0, 0)),
        ],
        out_specs=pl.BlockSpec((tm, tn), lambda j, bb, t: (bb * (s // tm) + t, j)),
        out_shape=jax.ShapeDtypeStruct((b * s, D_MODEL), BF16),
        compiler_params=_params("parallel", "parallel", "parallel"),
        name="merge",
    )(steps(a, ATTN_WIDTH), steps(y2, SSM_WIDTH), w_attn_proj, w_ssm_proj,
      steps(p2, P2_WIDTH), steps(p2, P2_WIDTH), _step_permutation(tm).T)


def _out_kernel(m_ref, w_ref, x_ref, o_ref):
    o_ref[...] = x_ref[...] + jnp.dot(m_ref[...], w_ref[...], preferred_element_type=F32)


def _out_proj(merged, w_out, x2, tm=512, tn=1024):
    t = merged.shape[0]
    return pl.pallas_call(
        _out_kernel,
        grid=(D_MODEL // tn, t // tm),
        in_specs=[
            pl.BlockSpec((tm, D_MODEL), lambda j, i: (i, 0)),
            pl.BlockSpec((D_MODEL, tn), lambda j, i: (0, j)),
            pl.BlockSpec((tm, tn), lambda j, i: (i, j)),
        ],
        out_specs=pl.BlockSpec((tm, tn), lambda j, i: (i, j)),
        out_shape=jax.ShapeDtypeStruct((t, D_MODEL), F32),
        compiler_params=_params("parallel", "parallel"),
        name="out_proj",
    )(merged, w_out, x2)


def _layer(x, w):
    b, s, d = x.shape
    h_steps = _prenorm(x, w["norm_gain"])
    h2 = h_steps.reshape(b * s, d)
    p1 = _proj(h2, w["w_in"], COL_Q, P1_WIDTH, w["qk_gain_row"], 2, "proj_attn")
    p2 = _proj(h2, w["w_in"], COL_GB, P2_WIDTH, None, 0, "proj_gates")
    u_t = _proj_u(h_steps, w["wu_t"])
    a = _attention(p1, w["bias_tabs"], b, s).reshape(b * s, ATTN_WIDTH)
    y_t = _ssm(u_t, w["wz"], w["t0"], w["cc"], w["a16"], w["d_rows"])
    y2 = _glu(y_t, w["w_glu"], p2)
    merged = _merge(a, y2, w["w_attn_proj"], w["w_ssm_proj"], p2, b, s)
    return _out_proj(merged, w["w_out"], x.reshape(b * s, d)).reshape(b, s, d)


def _prepare(rel_bias, norm_gain, w_in, q_gain, k_gain, lam_re, lam_im, log_dt, b_re, b_im, c_re, c_im,
             d_skip, w_glu, w_attn_proj, w_ssm_proj, w_out):
    wz, t0, cc, a16 = _ssm_weights(lam_re, lam_im, log_dt, b_re, b_im, c_re, c_im)
    gains = jnp.concatenate([jnp.tile(q_gain.astype(F32) * (HEAD_DIM ** -0.5 * LOG2E), N_HEADS),
                             jnp.tile(k_gain.astype(F32), N_HEADS),
                             jnp.zeros((P1_WIDTH - 2 * ATTN_WIDTH,), F32)])
    d_rows = jnp.tile(d_skip.astype(F32).reshape(SSM_GROUPS, 1, SSM_GROUP), (1, SSM_BLOCK, 1))
    return {
        "norm_gain": norm_gain.astype(F32),
        "w_in": w_in.astype(BF16),
        "wu_t": w_in[:, COL_U:COL_U + SSM_WIDTH].T.astype(BF16),
        "qk_gain_row": gains.reshape(1, P1_WIDTH),
        "bias_tabs": _bias_tables(rel_bias),
        "wz": wz, "t0": t0, "cc": cc, "a16": a16,
        "d_rows": d_rows.reshape(SSM_GROUPS, SSM_ROWS, 1),
        "w_glu": w_glu.astype(BF16),
        "w_attn_proj": w_attn_proj.astype(BF16),
        "w_ssm_proj": w_ssm_proj.astype(BF16),
        "w_out": w_out.astype(BF16),
    }


def kernel(x_prompt, x_sample, rel_bias, norm_gain, w_in, q_gain, k_gain, lam_re, lam_im, log_dt, b_re, b_im,
           c_re, c_im, d_skip, w_glu, w_attn_proj, w_ssm_proj, w_out):
    y_prompt, y_sample = x_prompt, x_sample
    for l in range(norm_gain.shape[0]):
        w = _prepare(rel_bias, norm_gain[l], w_in[l], q_gain[l], k_gain[l], lam_re[l], lam_im[l], log_dt[l],
                     b_re[l], b_im[l], c_re[l], c_im[l], d_skip[l], w_glu[l], w_attn_proj[l], w_ssm_proj[l],
                     w_out[l])
        y_prompt = _layer(y_prompt, w)
        y_sample = _layer(y_sample, w)
    return (y_prompt, y_sample)
```

```python
import functools
import math

import jax
import jax.numpy as jnp
from jax import lax
from jax.experimental import pallas as pl
from jax.experimental.pallas import tpu as pltpu

F32 = jnp.float32
BF16 = jnp.bfloat16

D_MODEL = 4096
N_HEADS = 16
HEAD_DIM = 128
ATTN_WIDTH = N_HEADS * HEAD_DIM
SIDE_KEYS = 64
N_BUCKETS = 32
REL_MAX_DIST = 1024
NEG = -1e30
LOG2E = math.log2(math.e)
EPS = 1e-6
SSM_WIDTH = 2048
SSM_GROUP = 16
SSM_GROUPS = SSM_WIDTH // SSM_GROUP
SSM_STATE = 64
SSM_BLOCK = 16
SSM_ROWS = SSM_BLOCK * SSM_GROUP

COL_Q = 0
COL_K = ATTN_WIDTH
COL_V = 2 * ATTN_WIDTH
COL_GA = 3 * ATTN_WIDTH
COL_U = 4 * ATTN_WIDTH
COL_GB = COL_U + SSM_WIDTH
COL_MA = COL_GB + SSM_WIDTH
COL_MB = COL_MA + D_MODEL
P1_WIDTH = COL_U
P2_WIDTH = SSM_WIDTH + 2 * D_MODEL

LANES = 128
PLANES = SSM_BLOCK
PERM_TILE = 256
EPILOGUE_CHUNK = 512
ATTN_GEOM = ((1, 8, 24), (4, 16, 48), (16, 64, 192))
ATTN_BATCH = (4, 8, 8)
ATTN_ROWS = 128
HALO_ROWS = SIDE_KEYS

VMEM_BYTES_V7X = 64 * 1024 * 1024
VMEM_LIMIT = VMEM_BYTES_V7X - 4 * 1024 * 1024


def _params(*sem):
    return pltpu.CompilerParams(dimension_semantics=sem, vmem_limit_bytes=VMEM_LIMIT)


def _prenorm_kernel(x_ref, g_ref, perm_ref, o_ref):
    x = x_ref[...]
    ms = jnp.mean(x * x, axis=-1, keepdims=True)
    h = (x * lax.rsqrt(ms + EPS) * g_ref[...]).astype(BF16)
    hp = jnp.dot(perm_ref[...], h, preferred_element_type=F32).astype(BF16)
    o_ref[...] = hp.reshape(o_ref.shape)


def _prenorm(x, gain, tm=PERM_TILE):
    b, s, d = x.shape
    return pl.pallas_call(
        _prenorm_kernel,
        grid=(b, s // tm),
        in_specs=[pl.BlockSpec((None, tm, d), lambda bb, t: (bb, t, 0)),
                  pl.BlockSpec((1, d), lambda bb, t: (0, 0)),
                  pl.BlockSpec((tm, tm), lambda bb, t: (0, 0))],
        out_specs=pl.BlockSpec((None, SSM_BLOCK, tm // SSM_BLOCK, d), lambda bb, t: (bb, 0, t, 0)),
        out_shape=jax.ShapeDtypeStruct((b, SSM_BLOCK, s // SSM_BLOCK, d), BF16),
        compiler_params=_params("parallel", "parallel"),
        name="prenorm",
    )(x, gain.reshape(1, d), _step_permutation(tm))


def _proj_kernel(h_ref, w_ref, g_ref, o_ref, wb_ref, *, n_norm_tiles):
    j = pl.program_id(0)

    @pl.when(pl.program_id(1) == 0)
    def _():
        wb_ref[...] = w_ref[...].astype(BF16)

    @pl.when(j < n_norm_tiles)
    def _():
        def issue(cols):
            return jnp.dot(h_ref[...], wb_ref[:, cols], preferred_element_type=F32)

        def finish(cols, acc):
            for hh in range(acc.shape[1] // HEAD_DIM):
                blk = acc[:, hh * HEAD_DIM:(hh + 1) * HEAD_DIM]
                sl = slice(cols.start + hh * HEAD_DIM, cols.start + (hh + 1) * HEAD_DIM)
                ms = jnp.mean(blk * blk, axis=-1, keepdims=True)
                o_ref[:, sl] = blk * lax.rsqrt(ms + EPS) * g_ref[:, sl]

        _column_pipeline(o_ref.shape[1], EPILOGUE_CHUNK, issue, finish)

    @pl.when(j >= n_norm_tiles)
    def _():
        o_ref[...] = jnp.dot(h_ref[...], wb_ref[...], preferred_element_type=F32)


def _proj(h, w, col0, width, gain_row, n_norm_cols, name, tm=512, tn=1024):
    t, k = h.shape
    if gain_row is None:
        gain_row = jnp.zeros((1, width), F32)
    return pl.pallas_call(
        functools.partial(_proj_kernel, n_norm_tiles=n_norm_cols // tn),
        grid=(width // tn, t // tm),
        in_specs=[
            pl.BlockSpec((tm, k), lambda j, i: (i, 0)),
            pl.BlockSpec((k, tn), lambda j, i: (0, col0 // tn + j)),
            pl.BlockSpec((1, tn), lambda j, i: (0, j)),
        ],
        out_specs=pl.BlockSpec((tm, tn), lambda j, i: (i, j)),
        out_shape=jax.ShapeDtypeStruct((t, width), F32),
        scratch_shapes=[pltpu.VMEM((k, tn), BF16)],
        compiler_params=_params("parallel", "arbitrary"),
        name=name,
    )(h, w, gain_row)


def _proj_u_kernel(w_ref, h_ref, o_ref):
    o_ref[...] = lax.dot_general(w_ref[...], h_ref[...], (((1,), (1,)), ((), ())),
                                 preferred_element_type=F32)


def _proj_u(h_steps, wu_t, tmt=256):
    b, _, ntau, _ = h_steps.shape
    return pl.pallas_call(
        _proj_u_kernel,
        grid=(b, SSM_BLOCK, ntau // tmt),
        in_specs=[
            pl.BlockSpec((SSM_WIDTH, D_MODEL), lambda bb, i, tt: (0, 0)),
            pl.BlockSpec((None, None, tmt, D_MODEL), lambda bb, i, tt: (bb, i, tt, 0)),
        ],
        out_specs=pl.BlockSpec((None, None, SSM_WIDTH, tmt), lambda bb, i, tt: (bb, i, 0, tt)),
        out_shape=jax.ShapeDtypeStruct((b, SSM_BLOCK, SSM_WIDTH, ntau), F32),
        compiler_params=_params("parallel", "parallel", "parallel"),
        name="proj_u",
    )(wu_t, h_steps)


def _rel_bucket(rel):
    half = N_BUCKETS // 2
    exact = half // 2
    base = jnp.where(rel > 0, half, 0)
    n = jnp.abs(rel)
    nf = jnp.maximum(n, 1).astype(F32)
    large = exact + (jnp.log(nf / exact) / math.log(REL_MAX_DIST / exact) * (half - exact)).astype(jnp.int32)
    large = jnp.minimum(large, half - 1)
    return base + jnp.where(n < exact, n, large)


def _bias_tables(rel_bias):
    buckets = jnp.arange(N_BUCKETS)
    tabs = []
    for d, nq, nk in ATTN_GEOM:
        g = PLANES // d
        koff = (nk - nq) // 2
        aq = jnp.arange(g)[:, None, None, None]
        tq = jnp.arange(nq)[None, :, None, None]
        ak = jnp.arange(g)[None, None, :, None]
        tk = jnp.arange(nk)[None, None, None, :]
        rel = (g * ((tk - koff) - tq) + (ak - aq)).reshape(g * nq, g * nk)
        onehot = (_rel_bucket(rel * d)[..., None] == buckets).astype(F32)
        bias = jnp.einsum('qkb,bh->hqk', onehot, rel_bias.astype(F32), precision=lax.Precision.HIGHEST)
        bias = jnp.where((jnp.abs(rel) <= SIDE_KEYS)[None], bias, NEG)
        trow = jnp.arange(g * nk) % nk - koff
        before = (trow < 0)[None, None, :]
        after = (trow >= nq)[None, None, :]
        bias = bias * LOG2E
        tabs.append(jnp.stack([bias, jnp.where(before, NEG, bias), jnp.where(after, NEG, bias)], axis=1))
    return tabs


def _attn_kernel(q_ref, kp_ref, kc_ref, kn_ref, vp_ref, vc_ref, vn_ref, ga_ref, t1_ref, t4_ref, t16_ref, o_ref,
                 obuf, lbuf):
    t = pl.program_id(2)
    first = t == 0
    last = t == pl.num_programs(2) - 1
    k_refs = (kp_ref, kc_ref, kn_ref)
    v_refs = (vp_ref, vc_ref, vn_ref)
    ones = jnp.ones((HEAD_DIM,), BF16)

    def halo_rows(refs, plane, start, n):
        pieces = []
        r0 = -HALO_ROWS
        for ref in refs:
            r1 = r0 + ref.shape[1]
            lo, hi = max(start, r0), min(start + n, r1)
            if lo < hi:
                pieces.append(ref[plane, lo - r0:hi - r0, :])
            r0 = r1
        return pieces

    def logits(pi, grp):
        d, nq, nk = ATTN_GEOM[pi]
        tab_ref = (t1_ref, t4_ref, t16_ref)[pi]
        g = PLANES // d
        nblk = ATTN_ROWS // nq
        k0 = -((nk - nq) // 2)

        def rows_of(refs, r, start, n):
            pieces = [piece for a in range(g) for piece in halo_rows(refs, r + d * a, start, n)]
            return jnp.concatenate(pieces, axis=0).astype(BF16)

        def bias_of(blk):
            if blk == 0:
                return jnp.where(first, tab_ref[1], tab_ref[0])
            if blk == nblk - 1:
                return jnp.where(last, tab_ref[2], tab_ref[0])
            return tab_ref[0]

        q = jnp.stack([jnp.concatenate([q_ref[r + d * a, blk * nq:(blk + 1) * nq, :] for a in range(g)],
                                       axis=0).astype(BF16) for r, blk in grp])
        k = jnp.stack([rows_of(k_refs, r, blk * nq + k0, nk) for r, blk in grp])
        v = jnp.stack([rows_of(v_refs, r, blk * nq + k0, nk) for r, blk in grp])
        if all(0 < blk < nblk - 1 for _, blk in grp):
            bias = tab_ref[0][None]
        else:
            bias = jnp.stack([bias_of(blk) for _, blk in grp])
        return jnp.einsum('bqd,bkd->bqk', q, k, preferred_element_type=F32) + bias, v

    def finish(pi, grp, s, v):
        d, nq, _ = ATTN_GEOM[pi]
        g = PLANES // d
        m = jnp.max(s, axis=-1, keepdims=True)
        p = jnp.exp2(s - m).astype(BF16)
        v1 = jnp.concatenate([v, jnp.broadcast_to(ones, v.shape)], axis=-1)
        acc = jnp.einsum('bqk,bkd->bqd', p, v1, preferred_element_type=F32)
        den = acc[..., HEAD_DIM:]
        o = acc[..., :HEAD_DIM] / den
        lse = m + jnp.log2(den)
        for bi, (r, blk) in enumerate(grp):
            for a in range(g):
                rows = slice(blk * nq, (blk + 1) * nq)
                obuf[pi, r + d * a, rows, :] = o[bi, a * nq:(a + 1) * nq]
                lbuf[pi, r + d * a, rows, :] = lse[bi, a * nq:(a + 1) * nq]

    work = []
    for pi, (d, nq, _) in enumerate(ATTN_GEOM):
        blocks = [(r, blk) for r in range(d) for blk in range(ATTN_ROWS // nq)]
        work += [(pi, blocks[g0:g0 + ATTN_BATCH[pi]]) for g0 in range(0, len(blocks), ATTN_BATCH[pi])]
    pending = logits(*work[0])
    for idx, (pi, grp) in enumerate(work):
        s, v = pending
        if idx + 1 < len(work):
            pending = logits(*work[idx + 1])
        finish(pi, grp, s, v)

    for i in range(PLANES):
        l0, l1, l2 = lbuf[0, i], lbuf[1, i], lbuf[2, i]
        mx = jnp.maximum(jnp.maximum(l0, l1), l2)
        e0, e1, e2 = jnp.exp2(l0 - mx), jnp.exp2(l1 - mx), jnp.exp2(l2 - mx)
        o = (e0 * obuf[0, i] + e1 * obuf[1, i] + e2 * obuf[2, i]) / (e0 + e1 + e2)
        ga = ga_ref[i]
        o_ref[i] = (o * (ga * jax.nn.sigmoid(ga))).astype(o_ref.dtype)


def _attention(p1, bias_tabs, b, s):
    nrow = s // PLANES
    p1 = p1.reshape(b, PLANES, nrow, P1_WIDTH)
    per = ATTN_ROWS // HALO_ROWS
    n_halo = nrow // HALO_ROWS

    def tile(col):
        return pl.BlockSpec((None, PLANES, ATTN_ROWS, HEAD_DIM), lambda bb, h, t: (bb, 0, t, col // HEAD_DIM + h))

    def prev(col):
        return pl.BlockSpec((None, PLANES, HALO_ROWS, HEAD_DIM),
                            lambda bb, h, t: (bb, 0, jnp.maximum(t * per - 1, 0), col // HEAD_DIM + h))

    def nxt(col):
        return pl.BlockSpec((None, PLANES, HALO_ROWS, HEAD_DIM),
                            lambda bb, h, t: (bb, 0, jnp.minimum((t + 1) * per, n_halo - 1), col // HEAD_DIM + h))

    def table(tab):
        return pl.BlockSpec((None,) + tab.shape[1:], lambda bb, h, t: (h, 0, 0, 0))

    part_buf = pltpu.VMEM((len(ATTN_GEOM), PLANES, ATTN_ROWS, HEAD_DIM), F32)
    return pl.pallas_call(
        _attn_kernel,
        grid=(b, N_HEADS, nrow // ATTN_ROWS),
        in_specs=[tile(COL_Q), prev(COL_K), tile(COL_K), nxt(COL_K), prev(COL_V), tile(COL_V), nxt(COL_V),
                  tile(COL_GA)] + [table(tab) for tab in bias_tabs],
        out_specs=pl.BlockSpec((None, PLANES, ATTN_ROWS, HEAD_DIM), lambda bb, h, t: (bb, 0, t, h)),
        out_shape=jax.ShapeDtypeStruct((b, PLANES, nrow, ATTN_WIDTH), BF16),
        scratch_shapes=[part_buf, part_buf],
        compiler_params=_params("parallel", "parallel", "parallel"),
        name="attention",
    )(p1, p1, p1, p1, p1, p1, p1, p1, *bias_tabs)


def _ssm_prep_kernel(arow_ref, acol_ref, c_ref, bw_ref, wz_ref, t0_ref, cc_ref, *, gc):
    nb, hc, p = SSM_BLOCK, SSM_GROUP, SSM_STATE
    blk_of_lane = lax.broadcasted_iota(jnp.int32, (hc, SSM_ROWS), 1) // hc
    blk_of_lane_p = lax.broadcasted_iota(jnp.int32, (p, SSM_ROWS), 1) // hc

    def body(gi, carry):
        lag_kernels = []
        for dirn in range(2):
            ar = arow_ref[gi, 2 * dirn:2 * dirn + 1, :]
            ai = arow_ref[gi, 2 * dirn + 1:2 * dirn + 2, :]
            cr = c_ref[gi, 2 * dirn]
            ci = c_ref[gi, 2 * dirn + 1]
            pr = jnp.ones((1, p), F32)
            pi = jnp.zeros((1, p), F32)
            ca_re, ca_im = [], []
            for _ in range(nb + 1):
                ca_re.append(cr * pr - ci * pi)
                ca_im.append(cr * pi + ci * pr)
                pr, pi = pr * ar - pi * ai, pr * ai + pi * ar
            bre = bw_ref[gi, 2 * dirn]
            bim = bw_ref[gi, 2 * dirn + 1]
            lag_kernels.append(
                jnp.dot(jnp.concatenate(ca_re[:nb], axis=0).astype(BF16), bre.astype(BF16),
                        preferred_element_type=F32)
                - jnp.dot(jnp.concatenate(ca_im[:nb], axis=0).astype(BF16), bim.astype(BF16),
                          preferred_element_type=F32))
            order = range(1, nb + 1) if dirn == 0 else range(nb, 0, -1)
            c0 = 2 * p * dirn
            cc_ref[gi, :, c0:c0 + p] = jnp.concatenate([ca_re[k] for k in order], axis=0).astype(BF16)
            cc_ref[gi, :, c0 + p:c0 + 2 * p] = (-jnp.concatenate([ca_im[k] for k in order], axis=0)).astype(BF16)
            acr = acol_ref[gi, c0:c0 + p, :]
            aci = acol_ref[gi, c0 + p:c0 + 2 * p, :]
            qr = jnp.ones((p, 1), F32)
            qi = jnp.zeros((p, 1), F32)
            powers = []
            for _ in range(nb):
                powers.append((qr, qi))
                qr, qi = qr * acr - qi * aci, qr * aci + qi * acr
            per = jnp.zeros((p, SSM_ROWS), F32)
            pei = jnp.zeros((p, SSM_ROWS), F32)
            for i in range(nb):
                e = nb - 1 - i if dirn == 0 else i
                per = jnp.where(blk_of_lane_p == i, powers[e][0], per)
                pei = jnp.where(blk_of_lane_p == i, powers[e][1], pei)
            wz_ref[gi, c0:c0 + p, :] = (per * bre - pei * bim).astype(BF16)
            wz_ref[gi, c0 + p:c0 + 2 * p, :] = (per * bim + pei * bre).astype(BF16)
        kf, kb = lag_kernels
        diag = kf[0:hc, :] + kb[0:hc, :]
        for j in range(nb):
            acc = jnp.zeros((hc, SSM_ROWS), F32)
            for i in range(nb):
                if i < j:
                    src = kf[hc * (j - i):hc * (j - i + 1), :]
                elif i > j:
                    src = kb[hc * (i - j):hc * (i - j + 1), :]
                else:
                    src = diag
                acc = jnp.where(blk_of_lane == i, src, acc)
            t0_ref[gi, hc * j:hc * (j + 1), :] = acc.astype(BF16)
        return carry

    lax.fori_loop(0, gc, body, 0)


def _ssm_weights(lam_re, lam_im, log_dt, b_re, b_im, c_re, c_im, gc=8):
    g, p, hc = SSM_GROUPS, SSM_STATE, SSM_GROUP
    dt = jnp.exp(log_dt.astype(F32))[..., None]
    lr = lam_re.astype(F32)
    li = lam_im.astype(F32)
    mag = jnp.exp(lr * dt)
    ab_re = mag * jnp.cos(li * dt)
    ab_im = mag * jnp.sin(li * dt)
    den = lr * lr + li * li
    f_re = ((ab_re - 1.0) * lr + ab_im * li) / den
    f_im = (ab_im * lr - (ab_re - 1.0) * li) / den
    br = b_re.astype(F32)
    bi = b_im.astype(F32)
    bb_re = f_re[..., None] * br - f_im[..., None] * bi
    bb_im = f_re[..., None] * bi + f_im[..., None] * br
    mag16 = jnp.exp(SSM_BLOCK * (lr * dt))
    a16_re = mag16 * jnp.cos(SSM_BLOCK * (li * dt))
    a16_im = mag16 * jnp.sin(SSM_BLOCK * (li * dt))

    def pack(re, im):
        return jnp.stack([re[0], im[0], re[1], im[1]], axis=1)

    arow = pack(ab_re, ab_im)
    acol = arow.reshape(g, 4 * p, 1)
    a16 = pack(a16_re, a16_im).reshape(g, 4 * p, 1)
    cmat = pack(c_re.astype(F32), c_im.astype(F32))
    bw = jnp.tile(pack(bb_re, bb_im), (1, 1, 1, SSM_BLOCK))
    wspec = pl.BlockSpec((gc, SSM_ROWS, SSM_ROWS), lambda i: (i, 0, 0))
    wshape = jax.ShapeDtypeStruct((g, SSM_ROWS, SSM_ROWS), BF16)
    wz, t0, cc = pl.pallas_call(
        functools.partial(_ssm_prep_kernel, gc=gc),
        grid=(g // gc,),
        in_specs=[pl.BlockSpec((gc, 4, p), lambda i: (i, 0, 0)),
                  pl.BlockSpec((gc, 4 * p, 1), lambda i: (i, 0, 0)),
                  pl.BlockSpec((gc, 4, hc, p), lambda i: (i, 0, 0, 0)),
                  pl.BlockSpec((gc, 4, p, SSM_ROWS), lambda i: (i, 0, 0, 0))],
        out_specs=[wspec, wspec, wspec],
        out_shape=[wshape, wshape, wshape],
        compiler_params=_params("parallel"),
        name="ssm_prep",
    )(arow, acol, cmat, bw)
    return wz, t0, cc, a16


def _gelu_tanh(x):
    return 0.5 * x * (1.0 + jnp.tanh(math.sqrt(2.0 / math.pi) * (x + 0.044715 * (x * x * x))))


def _cmul(ar, ai, br, bi):
    return ar * br - ai * bi, ar * bi + ai * br


def _block_states(zs, coefs, nc):
    p = SSM_STATE
    ng = len(zs)
    rows = ng * p
    lane = lax.broadcasted_iota(jnp.int32, (rows, LANES), 1)
    zero = jnp.zeros((rows, LANES), F32)

    def part(arrs, r0, c=None):
        cols = slice(None) if c is None else slice(LANES * c, LANES * (c + 1))
        return jnp.concatenate([x[r0:r0 + p, cols] for x in arrs], axis=0)

    halves = []
    for dirn in range(2):
        base = 2 * p * dirn
        zr = [part(zs, base, c) for c in range(nc)]
        zi = [part(zs, base + p, c) for c in range(nc)]
        ar = jnp.broadcast_to(part(coefs, base), (rows, LANES))
        ai = jnp.broadcast_to(part(coefs, base + p), (rows, LANES))
        order = list(range(nc)) if dirn == 0 else list(range(nc - 1, -1, -1))
        ir, ii = {order[0]: zr[order[0]]}, {order[0]: zi[order[0]]}
        for prev, c in zip(order[:-1], order[1:]):
            tr, ti = _cmul(ar, ai, ir[prev], ii[prev])
            ir[c], ii[c] = zr[c] + tr, zi[c] + ti
        pows = [(ar, ai)]
        for _ in range(nc - 1):
            pows.append(_cmul(pows[-1][0], pows[-1][1], ar, ai))
        gr, gi = ir[order[-1]], ii[order[-1]]
        br, bi = pows[nc - 1]
        s = 1
        while s < LANES:
            keep = (lane >= s) if dirn == 0 else (lane < LANES - s)
            amt = s if dirn == 0 else LANES - s
            rr = jnp.where(keep, pltpu.roll(gr, amt, 1), zero)
            ri = jnp.where(keep, pltpu.roll(gi, amt, 1), zero)
            tr, ti = _cmul(br, bi, rr, ri)
            gr, gi = gr + tr, gi + ti
            br, bi = _cmul(br, bi, br, bi)
            s *= 2
        keep = (lane >= 1) if dirn == 0 else (lane < LANES - 1)
        amt = 1 if dirn == 0 else LANES - 1
        er = jnp.where(keep, pltpu.roll(gr, amt, 1), zero)
        ei = jnp.where(keep, pltpu.roll(gi, amt, 1), zero)
        hr, hi = [None] * nc, [None] * nc
        hr[order[0]], hi[order[0]] = er, ei
        for k, (prev, c) in enumerate(zip(order[:-1], order[1:])):
            tr, ti = _cmul(pows[k][0], pows[k][1], er, ei)
            hr[c], hi[c] = ir[prev] + tr, ii[prev] + ti
        halves += [jnp.concatenate(hr, axis=1), jnp.concatenate(hi, axis=1)]
    return [jnp.concatenate([h[g * p:(g + 1) * p] for h in halves], axis=0) for g in range(ng)]


def _ssm_kernel(u_ref, wz_ref, t0_ref, cc_ref, a_ref, d_ref, perm_ref, unperm_ref, y_ref, *, gc, ntau):
    nc = ntau // LANES
    rows = [slice(SSM_GROUP * gi, SSM_GROUP * (gi + 1)) for gi in range(gc)]
    zs = []
    for gi in range(gc):
        x = u_ref[:, rows[gi], :].reshape(SSM_ROWS, ntau).astype(BF16)
        xp = jnp.dot(x, perm_ref[...], preferred_element_type=F32).astype(BF16)
        zs.append(jnp.dot(wz_ref[gi], xp, preferred_element_type=F32))
    states = _block_states(zs, [a_ref[gi] for gi in range(gc)], nc)
    for gi in range(gc):
        xf = u_ref[:, rows[gi], :].reshape(SSM_ROWS, ntau)
        h = jnp.dot(states[gi].astype(BF16), unperm_ref[...], preferred_element_type=F32).astype(BF16)
        y = (jnp.dot(t0_ref[gi], xf.astype(BF16), preferred_element_type=F32)
             + jnp.dot(cc_ref[gi], h, preferred_element_type=F32))
        y = _gelu_tanh(y + d_ref[gi] * xf)
        y_ref[:, rows[gi], :] = y.astype(y_ref.dtype).reshape(SSM_BLOCK, SSM_GROUP, ntau)


def _ssm(u_t, wz, t0, cc, a16, d_rows, gc=4):
    b, _, _, ntau = u_t.shape
    nc = ntau // LANES
    j = jnp.arange(ntau)
    src = (j % LANES) * nc + j // LANES
    perm = (jnp.arange(ntau)[:, None] == src[None, :]).astype(BF16)
    wspec = pl.BlockSpec((gc, SSM_ROWS, SSM_ROWS), lambda bb, g: (g, 0, 0))
    vspec = pl.BlockSpec((gc, SSM_ROWS, 1), lambda bb, g: (g, 0, 0))
    uspec = pl.BlockSpec((None, SSM_BLOCK, SSM_GROUP * gc, ntau), lambda bb, g: (bb, 0, g, 0))
    pspec = pl.BlockSpec((ntau, ntau), lambda bb, g: (0, 0))
    return pl.pallas_call(
        functools.partial(_ssm_kernel, gc=gc, ntau=ntau),
        grid=(b, SSM_GROUPS // gc),
        in_specs=[uspec, wspec, wspec, wspec, vspec, vspec, pspec, pspec],
        out_specs=uspec,
        out_shape=jax.ShapeDtypeStruct(u_t.shape, BF16),
        compiler_params=_params("parallel", "parallel"),
        name="ssm",
    )(u_t, wz, t0, cc, a16, d_rows, perm, perm.T)


def _column_pipeline(width, chunk, issue, finish):
    chunks = [slice(c, c + chunk) for c in range(0, width, chunk)]
    pending = issue(chunks[0])
    for idx, cols in enumerate(chunks):
        current = pending
        if idx + 1 < len(chunks):
            pending = issue(chunks[idx + 1])
        finish(cols, current)


def _glu_kernel(y_ref, w1_ref, w2_ref, gb_ref, o_ref):
    pp, tmt, _ = o_ref.shape
    yt = jnp.concatenate([y_ref[p].T for p in range(pp)], axis=0)

    def issue(cols):
        return (jnp.dot(yt, w1_ref[:, cols], preferred_element_type=F32),
                jnp.dot(yt, w2_ref[:, cols], preferred_element_type=F32))

    def finish(cols, gl):
        gb = gb_ref[:, :, cols].reshape(pp * tmt, -1)
        out = (gl[0] * jax.nn.sigmoid(gl[1]) * (gb * jax.nn.sigmoid(gb))).astype(o_ref.dtype)
        o_ref[:, :, cols] = out.reshape(pp, tmt, -1)

    _column_pipeline(o_ref.shape[2], EPILOGUE_CHUNK, issue, finish)


def _glu(y_t, w_glu, p2, tokens=512, tn=2048):
    b, _, _, ntau = y_t.shape
    tmt = min(tokens, ntau)
    pp = tokens // tmt
    nj = SSM_WIDTH // tn
    p2v = p2.reshape(b, SSM_BLOCK, ntau, P2_WIDTH)
    out = pl.pallas_call(
        _glu_kernel,
        grid=(nj, b, SSM_BLOCK // pp, ntau // tmt),
        in_specs=[
            pl.BlockSpec((None, pp, SSM_WIDTH, tmt), lambda j, bb, i, tt: (bb, i, 0, tt)),
            pl.BlockSpec((SSM_WIDTH, tn), lambda j, bb, i, tt: (0, j)),
            pl.BlockSpec((SSM_WIDTH, tn), lambda j, bb, i, tt: (0, nj + j)),
            pl.BlockSpec((None, pp, tmt, tn), lambda j, bb, i, tt: (bb, i, tt, j)),
        ],
        out_specs=pl.BlockSpec((None, pp, tmt, tn), lambda j, bb, i, tt: (bb, i, tt, j)),
        out_shape=jax.ShapeDtypeStruct((b, SSM_BLOCK, ntau, SSM_WIDTH), BF16),
        compiler_params=_params("parallel", "parallel", "parallel", "parallel"),
        name="glu",
    )(y_t, w_glu, w_glu, p2v)
    return out.reshape(b * SSM_BLOCK * ntau, SSM_WIDTH)


def _merge_kernel(a_ref, y_ref, wa_ref, wb_ref, ma_ref, mb_ref, unperm_ref, o_ref):
    rows = o_ref.shape[0]
    a = a_ref[...].reshape(rows, ATTN_WIDTH)
    y = y_ref[...].reshape(rows, SSM_WIDTH)
    sub = unperm_ref.shape[0]
    n = sub // SSM_BLOCK

    def issue(cols):
        return (jnp.dot(a, wa_ref[:, cols], preferred_element_type=F32),
                jnp.dot(y, wb_ref[:, cols], preferred_element_type=F32))

    def finish(cols, z):
        ma = ma_ref[:, :, cols].reshape(rows, -1)
        mb = mb_ref[:, :, cols].reshape(rows, -1)
        merged = (jax.nn.sigmoid(ma) * z[0] + jax.nn.sigmoid(mb) * z[1]).astype(BF16)
        merged = merged.reshape(SSM_BLOCK, rows // SSM_BLOCK, -1)
        for k in range(rows // sub):
            part = merged[:, k * n:(k + 1) * n, :].reshape(sub, -1)
            o_ref[k * sub:(k + 1) * sub, cols] = jnp.dot(unperm_ref[...], part,
                                                         preferred_element_type=F32).astype(o_ref.dtype)

    _column_pipeline(o_ref.shape[1], EPILOGUE_CHUNK, issue, finish)


def _step_permutation(tm):
    n = tm // SSM_BLOCK
    r = jnp.arange(tm)
    src = SSM_BLOCK * (r % n) + r // n
    return (src[:, None] == jnp.arange(tm)[None, :]).astype(BF16)


def _merge(a, y2, w_attn_proj, w_ssm_proj, p2, b, s, tm=512, tn=1024):
    ntau = s // SSM_BLOCK
    n = tm // SSM_BLOCK
    ma0 = (COL_MA - COL_GB) // tn
    mb0 = (COL_MB - COL_GB) // tn

    def steps(arr, width):
        return arr.reshape(b, SSM_BLOCK, ntau, width)

    return pl.pallas_call(
        _merge_kernel,
        grid=(D_MODEL // tn, b, s // tm),
        in_specs=[
            pl.BlockSpec((None, SSM_BLOCK, n, ATTN_WIDTH), lambda j, bb, t: (bb, 0, t, 0)),
            pl.BlockSpec((None, SSM_BLOCK, n, SSM_WIDTH), lambda j, bb, t: (bb, 0, t, 0)),
            pl.BlockSpec((ATTN_WIDTH, tn), lambda j, bb, t: (0, j)),
            pl.BlockSpec((SSM_WIDTH, tn), lambda j, bb, t: (0, j)),
            pl.BlockSpec((None, SSM_BLOCK, n, tn), lambda j, bb, t: (bb, 0, t, ma0 + j)),
            pl.BlockSpec((None, SSM_BLOCK, n, tn), lambda j, bb, t: (bb, 0, t, mb0 + j)),
            pl.BlockSpec((PERM_TILE, PERM_TILE), lambda j, bb, t: (0, 0)),
        ],
        out_specs=pl.BlockSpec((tm, tn), lambda j, bb, t: (bb * (s // tm) + t, j)),
        out_shape=jax.ShapeDtypeStruct((b * s, D_MODEL), BF16),
        compiler_params=_params("parallel", "parallel", "parallel"),
        name="merge",
    )(steps(a, ATTN_WIDTH), steps(y2, SSM_WIDTH), w_attn_proj, w_ssm_proj,
      steps(p2, P2_WIDTH), steps(p2, P2_WIDTH), _step_permutation(PERM_TILE).T)


def _out_kernel(m_ref, w_ref, x_ref, o_ref):
    o_ref[...] = x_ref[...] + jnp.dot(m_ref[...], w_ref[...], preferred_element_type=F32)


def _out_proj(merged, w_out, x2, tm=512, tn=1024):
    t = merged.shape[0]
    return pl.pallas_call(
        _out_kernel,
        grid=(D_MODEL // tn, t // tm),
        in_specs=[
            pl.BlockSpec((tm, D_MODEL), lambda j, i: (i, 0)),
            pl.BlockSpec((D_MODEL, tn), lambda j, i: (0, j)),
            pl.BlockSpec((tm, tn), lambda j, i: (i, j)),
        ],
        out_specs=pl.BlockSpec((tm, tn), lambda j, i: (i, j)),
        out_shape=jax.ShapeDtypeStruct((t, D_MODEL), F32),
        compiler_params=_params("parallel", "parallel"),
        name="out_proj",
    )(merged, w_out, x2)


def _layer(x, w):
    b, s, d = x.shape
    h_steps = _prenorm(x, w["norm_gain"])
    h2 = h_steps.reshape(b * s, d)
    p1 = _proj(h2, w["w_in"], COL_Q, P1_WIDTH, w["qk_gain_row"], 2 * ATTN_WIDTH, "proj_attn")
    p2 = _proj(h2, w["w_in"], COL_GB, P2_WIDTH, None, 0, "proj_gates")
    u_t = _proj_u(h_steps, w["wu_t"])
    a = _attention(p1, w["bias_tabs"], b, s).reshape(b * s, ATTN_WIDTH)
    y_t = _ssm(u_t, w["wz"], w["t0"], w["cc"], w["a16"], w["d_rows"])
    y2 = _glu(y_t, w["w_glu"], p2)
    merged = _merge(a, y2, w["w_attn_proj"], w["w_ssm_proj"], p2, b, s)
    return _out_proj(merged, w["w_out"], x.reshape(b * s, d)).reshape(b, s, d)


def _prepare(rel_bias, norm_gain, w_in, q_gain, k_gain, lam_re, lam_im, log_dt, b_re, b_im, c_re, c_im,
             d_skip, w_glu, w_attn_proj, w_ssm_proj, w_out):
    wz, t0, cc, a16 = _ssm_weights(lam_re, lam_im, log_dt, b_re, b_im, c_re, c_im)
    gains = jnp.concatenate([jnp.tile(q_gain.astype(F32) * (HEAD_DIM ** -0.5 * LOG2E), N_HEADS),
                             jnp.tile(k_gain.astype(F32), N_HEADS),
                             jnp.zeros((P1_WIDTH - 2 * ATTN_WIDTH,), F32)])
    d_rows = jnp.tile(d_skip.astype(F32).reshape(SSM_GROUPS, 1, SSM_GROUP), (1, SSM_BLOCK, 1))
    return {
        "norm_gain": norm_gain.astype(F32),
        "w_in": w_in.astype(F32),
        "wu_t": w_in[:, COL_U:COL_U + SSM_WIDTH].T.astype(BF16),
        "qk_gain_row": gains.reshape(1, P1_WIDTH),
        "bias_tabs": _bias_tables(rel_bias),
        "wz": wz, "t0": t0, "cc": cc, "a16": a16,
        "d_rows": d_rows.reshape(SSM_GROUPS, SSM_ROWS, 1),
        "w_glu": w_glu.astype(BF16),
        "w_attn_proj": w_attn_proj.astype(BF16),
        "w_ssm_proj": w_ssm_proj.astype(BF16),
        "w_out": w_out.astype(BF16),
    }


def kernel(x_prompt, x_sample, rel_bias, norm_gain, w_in, q_gain, k_gain, lam_re, lam_im, log_dt, b_re, b_im,
           c_re, c_im, d_skip, w_glu, w_attn_proj, w_ssm_proj, w_out):
    y_prompt, y_sample = x_prompt, x_sample
    for l in range(norm_gain.shape[0]):
        w = _prepare(rel_bias, norm_gain[l], w_in[l], q_gain[l], k_gain[l], lam_re[l], lam_im[l], log_dt[l],
                     b_re[l], b_im[l], c_re[l], c_im[l], d_skip[l], w_glu[l], w_attn_proj[l], w_ssm_proj[l],
                     w_out[l])
        y_prompt = _layer(y_prompt, w)
        y_sample = _layer(y_sample, w)
    return (y_prompt, y_sample)
```

```python
import functools
import math

import jax
import jax.numpy as jnp
from jax import lax
from jax.experimental import pallas as pl
from jax.experimental.pallas import tpu as pltpu

F32 = jnp.float32
BF16 = jnp.bfloat16

D_MODEL = 4096
N_HEADS = 16
HEAD_DIM = 128
ATTN_WIDTH = N_HEADS * HEAD_DIM
SIDE_KEYS = 64
N_BUCKETS = 32
REL_MAX_DIST = 1024
NEG = -1e30
LOG2E = math.log2(math.e)
EPS = 1e-6
SSM_WIDTH = 2048
SSM_GROUP = 16
SSM_GROUPS = SSM_WIDTH // SSM_GROUP
SSM_STATE = 64
SSM_BLOCK = 16
SSM_ROWS = SSM_BLOCK * SSM_GROUP

COL_Q = 0
COL_K = ATTN_WIDTH
COL_V = 2 * ATTN_WIDTH
COL_GA = 3 * ATTN_WIDTH
COL_U = 4 * ATTN_WIDTH
COL_GB = COL_U + SSM_WIDTH
COL_MA = COL_GB + SSM_WIDTH
COL_MB = COL_MA + D_MODEL
P1_WIDTH = COL_U
P2_WIDTH = SSM_WIDTH + 2 * D_MODEL

LANES = 128
PLANES = SSM_BLOCK
PERM_TILE = 256
EPILOGUE_CHUNK = 512
ATTN_GEOM = ((1, 8, 24), (4, 16, 48), (16, 64, 192))
ATTN_BATCH = (4, 8, 8)
ATTN_ROWS = 128
HALO_ROWS = SIDE_KEYS

VMEM_BYTES_V7X = 64 * 1024 * 1024
VMEM_LIMIT = VMEM_BYTES_V7X - 4 * 1024 * 1024


def _params(*sem):
    return pltpu.CompilerParams(dimension_semantics=sem, vmem_limit_bytes=VMEM_LIMIT)


def _prenorm_kernel(x_ref, g_ref, perm_ref, o_ref):
    x = x_ref[...]
    ms = jnp.mean(x * x, axis=-1, keepdims=True)
    h = (x * lax.rsqrt(ms + EPS) * g_ref[...]).astype(BF16)
    hp = jnp.dot(perm_ref[...], h, preferred_element_type=F32).astype(BF16)
    o_ref[...] = hp.reshape(o_ref.shape)


def _prenorm(x, gain, tm=PERM_TILE):
    b, s, d = x.shape
    return pl.pallas_call(
        _prenorm_kernel,
        grid=(b, s // tm),
        in_specs=[pl.BlockSpec((None, tm, d), lambda bb, t: (bb, t, 0)),
                  pl.BlockSpec((1, d), lambda bb, t: (0, 0)),
                  pl.BlockSpec((tm, tm), lambda bb, t: (0, 0))],
        out_specs=pl.BlockSpec((None, SSM_BLOCK, tm // SSM_BLOCK, d), lambda bb, t: (bb, 0, t, 0)),
        out_shape=jax.ShapeDtypeStruct((b, SSM_BLOCK, s // SSM_BLOCK, d), BF16),
        compiler_params=_params("parallel", "parallel"),
        name="prenorm",
    )(x, gain.reshape(1, d), _step_permutation(tm))


def _proj_kernel(h_ref, w_ref, g_ref, o_ref, *, n_norm_tiles):
    j = pl.program_id(0)

    @pl.when(j < n_norm_tiles)
    def _():
        def issue(cols):
            return jnp.dot(h_ref[...], w_ref[:, cols], preferred_element_type=F32)

        def finish(cols, acc):
            for hh in range(acc.shape[1] // HEAD_DIM):
                blk = acc[:, hh * HEAD_DIM:(hh + 1) * HEAD_DIM]
                sl = slice(cols.start + hh * HEAD_DIM, cols.start + (hh + 1) * HEAD_DIM)
                ms = jnp.mean(blk * blk, axis=-1, keepdims=True)
                o_ref[:, sl] = blk * lax.rsqrt(ms + EPS) * g_ref[:, sl]

        _column_pipeline(o_ref.shape[1], EPILOGUE_CHUNK, issue, finish)

    @pl.when(j >= n_norm_tiles)
    def _():
        o_ref[...] = jnp.dot(h_ref[...], w_ref[...], preferred_element_type=F32)


def _proj(h, w, col0, width, gain_row, n_norm_cols, name, tm=512, tn=2048):
    t, k = h.shape
    if gain_row is None:
        gain_row = jnp.zeros((1, width), F32)
    return pl.pallas_call(
        functools.partial(_proj_kernel, n_norm_tiles=n_norm_cols // tn),
        grid=(width // tn, t // tm),
        in_specs=[
            pl.BlockSpec((tm, k), lambda j, i: (i, 0)),
            pl.BlockSpec((k, tn), lambda j, i: (0, col0 // tn + j)),
            pl.BlockSpec((1, tn), lambda j, i: (0, j)),
        ],
        out_specs=pl.BlockSpec((tm, tn), lambda j, i: (i, j)),
        out_shape=jax.ShapeDtypeStruct((t, width), F32),
        compiler_params=_params("parallel", "parallel"),
        name=name,
    )(h, w, gain_row)


def _proj_u_kernel(w_ref, h_ref, o_ref):
    o_ref[...] = lax.dot_general(w_ref[...], h_ref[...], (((1,), (1,)), ((), ())),
                                 preferred_element_type=F32)


def _proj_u(h_steps, wu_t, tmt=256):
    b, _, ntau, _ = h_steps.shape
    return pl.pallas_call(
        _proj_u_kernel,
        grid=(b, SSM_BLOCK, ntau // tmt),
        in_specs=[
            pl.BlockSpec((SSM_WIDTH, D_MODEL), lambda bb, i, tt: (0, 0)),
            pl.BlockSpec((None, None, tmt, D_MODEL), lambda bb, i, tt: (bb, i, tt, 0)),
        ],
        out_specs=pl.BlockSpec((None, None, SSM_WIDTH, tmt), lambda bb, i, tt: (bb, i, 0, tt)),
        out_shape=jax.ShapeDtypeStruct((b, SSM_BLOCK, SSM_WIDTH, ntau), F32),
        compiler_params=_params("parallel", "parallel", "parallel"),
        name="proj_u",
    )(wu_t, h_steps)


def _rel_bucket(rel):
    half = N_BUCKETS // 2
    exact = half // 2
    base = jnp.where(rel > 0, half, 0)
    n = jnp.abs(rel)
    nf = jnp.maximum(n, 1).astype(F32)
    large = exact + (jnp.log(nf / exact) / math.log(REL_MAX_DIST / exact) * (half - exact)).astype(jnp.int32)
    large = jnp.minimum(large, half - 1)
    return base + jnp.where(n < exact, n, large)


def _bias_tables(rel_bias):
    buckets = jnp.arange(N_BUCKETS)
    tabs = []
    for d, nq, nk in ATTN_GEOM:
        g = PLANES // d
        koff = (nk - nq) // 2
        aq = jnp.arange(g)[:, None, None, None]
        tq = jnp.arange(nq)[None, :, None, None]
        ak = jnp.arange(g)[None, None, :, None]
        tk = jnp.arange(nk)[None, None, None, :]
        rel = (g * ((tk - koff) - tq) + (ak - aq)).reshape(g * nq, g * nk)
        onehot = (_rel_bucket(rel * d)[..., None] == buckets).astype(F32)
        bias = jnp.einsum('qkb,bh->hqk', onehot, rel_bias.astype(F32), precision=lax.Precision.HIGHEST)
        bias = jnp.where((jnp.abs(rel) <= SIDE_KEYS)[None], bias, NEG)
        trow = jnp.arange(g * nk) % nk - koff
        before = (trow < 0)[None, None, :]
        after = (trow >= nq)[None, None, :]
        bias = bias * LOG2E
        tabs.append(jnp.stack([bias, jnp.where(before, NEG, bias), jnp.where(after, NEG, bias)], axis=1))
    return tabs


def _attn_kernel(q_ref, kp_ref, kc_ref, kn_ref, vp_ref, vc_ref, vn_ref, ga_ref, t1_ref, t4_ref, t16_ref, o_ref,
                 obuf, lbuf):
    t = pl.program_id(2)
    first = t == 0
    last = t == pl.num_programs(2) - 1
    k_refs = (kp_ref, kc_ref, kn_ref)
    v_refs = (vp_ref, vc_ref, vn_ref)
    ones = jnp.ones((HEAD_DIM,), BF16)

    def halo_rows(refs, plane, start, n):
        pieces = []
        r0 = -HALO_ROWS
        for ref in refs:
            r1 = r0 + ref.shape[1]
            lo, hi = max(start, r0), min(start + n, r1)
            if lo < hi:
                pieces.append(ref[plane, lo - r0:hi - r0, :])
            r0 = r1
        return pieces

    def logits(pi, grp):
        d, nq, nk = ATTN_GEOM[pi]
        tab_ref = (t1_ref, t4_ref, t16_ref)[pi]
        g = PLANES // d
        nblk = ATTN_ROWS // nq
        k0 = -((nk - nq) // 2)

        def rows_of(refs, r, start, n):
            pieces = [piece for a in range(g) for piece in halo_rows(refs, r + d * a, start, n)]
            return jnp.concatenate(pieces, axis=0).astype(BF16)

        def bias_of(blk):
            if blk == 0:
                return jnp.where(first, tab_ref[1], tab_ref[0])
            if blk == nblk - 1:
                return jnp.where(last, tab_ref[2], tab_ref[0])
            return tab_ref[0]

        q = jnp.stack([jnp.concatenate([q_ref[r + d * a, blk * nq:(blk + 1) * nq, :] for a in range(g)],
                                       axis=0).astype(BF16) for r, blk in grp])
        k = jnp.stack([rows_of(k_refs, r, blk * nq + k0, nk) for r, blk in grp])
        v = jnp.stack([rows_of(v_refs, r, blk * nq + k0, nk) for r, blk in grp])
        if all(0 < blk < nblk - 1 for _, blk in grp):
            bias = tab_ref[0][None]
        else:
            bias = jnp.stack([bias_of(blk) for _, blk in grp])
        return jnp.einsum('bqd,bkd->bqk', q, k, preferred_element_type=F32) + bias, v

    def finish(pi, grp, s, v):
        d, nq, _ = ATTN_GEOM[pi]
        g = PLANES // d
        m = jnp.max(s, axis=-1, keepdims=True)
        p = jnp.exp2(s - m).astype(BF16)
        v1 = jnp.concatenate([v, jnp.broadcast_to(ones, v.shape)], axis=-1)
        acc = jnp.einsum('bqk,bkd->bqd', p, v1, preferred_element_type=F32)
        den = acc[..., HEAD_DIM:]
        o = acc[..., :HEAD_DIM] / den
        lse = m + jnp.log2(den)
        for bi, (r, blk) in enumerate(grp):
            for a in range(g):
                rows = slice(blk * nq, (blk + 1) * nq)
                obuf[pi, r + d * a, rows, :] = o[bi, a * nq:(a + 1) * nq]
                lbuf[pi, r + d * a, rows, :] = lse[bi, a * nq:(a + 1) * nq]

    work = []
    for pi, (d, nq, _) in enumerate(ATTN_GEOM):
        blocks = [(r, blk) for r in range(d) for blk in range(ATTN_ROWS // nq)]
        work += [(pi, blocks[g0:g0 + ATTN_BATCH[pi]]) for g0 in range(0, len(blocks), ATTN_BATCH[pi])]
    pending = logits(*work[0])
    for idx, (pi, grp) in enumerate(work):
        s, v = pending
        if idx + 1 < len(work):
            pending = logits(*work[idx + 1])
        finish(pi, grp, s, v)

    for i in range(PLANES):
        l0, l1, l2 = lbuf[0, i], lbuf[1, i], lbuf[2, i]
        mx = jnp.maximum(jnp.maximum(l0, l1), l2)
        e0, e1, e2 = jnp.exp2(l0 - mx), jnp.exp2(l1 - mx), jnp.exp2(l2 - mx)
        o = (e0 * obuf[0, i] + e1 * obuf[1, i] + e2 * obuf[2, i]) / (e0 + e1 + e2)
        ga = ga_ref[i]
        o_ref[i] = (o * (ga * jax.nn.sigmoid(ga))).astype(o_ref.dtype)


def _attention(p1, bias_tabs, b, s):
    nrow = s // PLANES
    p1 = p1.reshape(b, PLANES, nrow, P1_WIDTH)
    per = ATTN_ROWS // HALO_ROWS
    n_halo = nrow // HALO_ROWS

    def tile(col):
        return pl.BlockSpec((None, PLANES, ATTN_ROWS, HEAD_DIM), lambda bb, h, t: (bb, 0, t, col // HEAD_DIM + h))

    def prev(col):
        return pl.BlockSpec((None, PLANES, HALO_ROWS, HEAD_DIM),
                            lambda bb, h, t: (bb, 0, jnp.maximum(t * per - 1, 0), col // HEAD_DIM + h))

    def nxt(col):
        return pl.BlockSpec((None, PLANES, HALO_ROWS, HEAD_DIM),
                            lambda bb, h, t: (bb, 0, jnp.minimum((t + 1) * per, n_halo - 1), col // HEAD_DIM + h))

    def table(tab):
        return pl.BlockSpec((None,) + tab.shape[1:], lambda bb, h, t: (h, 0, 0, 0))

    part_buf = pltpu.VMEM((len(ATTN_GEOM), PLANES, ATTN_ROWS, HEAD_DIM), F32)
    return pl.pallas_call(
        _attn_kernel,
        grid=(b, N_HEADS, nrow // ATTN_ROWS),
        in_specs=[tile(COL_Q), prev(COL_K), tile(COL_K), nxt(COL_K), prev(COL_V), tile(COL_V), nxt(COL_V),
                  tile(COL_GA)] + [table(tab) for tab in bias_tabs],
        out_specs=pl.BlockSpec((None, PLANES, ATTN_ROWS, HEAD_DIM), lambda bb, h, t: (bb, 0, t, h)),
        out_shape=jax.ShapeDtypeStruct((b, PLANES, nrow, ATTN_WIDTH), BF16),
        scratch_shapes=[part_buf, part_buf],
        compiler_params=_params("parallel", "parallel", "parallel"),
        name="attention",
    )(p1, p1, p1, p1, p1, p1, p1, p1, *bias_tabs)


def _ssm_prep_kernel(arow_ref, acol_ref, c_ref, bw_ref, wz_ref, t0_ref, cc_ref, *, gc):
    nb, hc, p = SSM_BLOCK, SSM_GROUP, SSM_STATE
    blk_of_lane = lax.broadcasted_iota(jnp.int32, (hc, SSM_ROWS), 1) // hc
    blk_of_lane_p = lax.broadcasted_iota(jnp.int32, (p, SSM_ROWS), 1) // hc

    def body(gi, carry):
        lag_kernels = []
        for dirn in range(2):
            ar = arow_ref[gi, 2 * dirn:2 * dirn + 1, :]
            ai = arow_ref[gi, 2 * dirn + 1:2 * dirn + 2, :]
            cr = c_ref[gi, 2 * dirn]
            ci = c_ref[gi, 2 * dirn + 1]
            pr = jnp.ones((1, p), F32)
            pi = jnp.zeros((1, p), F32)
            ca_re, ca_im = [], []
            for _ in range(nb + 1):
                ca_re.append(cr * pr - ci * pi)
                ca_im.append(cr * pi + ci * pr)
                pr, pi = pr * ar - pi * ai, pr * ai + pi * ar
            bre = bw_ref[gi, 2 * dirn]
            bim = bw_ref[gi, 2 * dirn + 1]
            lag_kernels.append(
                jnp.dot(jnp.concatenate(ca_re[:nb], axis=0).astype(BF16), bre.astype(BF16),
                        preferred_element_type=F32)
                - jnp.dot(jnp.concatenate(ca_im[:nb], axis=0).astype(BF16), bim.astype(BF16),
                          preferred_element_type=F32))
            order = range(1, nb + 1) if dirn == 0 else range(nb, 0, -1)
            c0 = 2 * p * dirn
            cc_ref[gi, :, c0:c0 + p] = jnp.concatenate([ca_re[k] for k in order], axis=0).astype(BF16)
            cc_ref[gi, :, c0 + p:c0 + 2 * p] = (-jnp.concatenate([ca_im[k] for k in order], axis=0)).astype(BF16)
            acr = acol_ref[gi, c0:c0 + p, :]
            aci = acol_ref[gi, c0 + p:c0 + 2 * p, :]
            qr = jnp.ones((p, 1), F32)
            qi = jnp.zeros((p, 1), F32)
            powers = []
            for _ in range(nb):
                powers.append((qr, qi))
                qr, qi = qr * acr - qi * aci, qr * aci + qi * acr
            per = jnp.zeros((p, SSM_ROWS), F32)
            pei = jnp.zeros((p, SSM_ROWS), F32)
            for i in range(nb):
                e = nb - 1 - i if dirn == 0 else i
                per = jnp.where(blk_of_lane_p == i, powers[e][0], per)
                pei = jnp.where(blk_of_lane_p == i, powers[e][1], pei)
            wz_ref[gi, c0:c0 + p, :] = (per * bre - pei * bim).astype(BF16)
            wz_ref[gi, c0 + p:c0 + 2 * p, :] = (per * bim + pei * bre).astype(BF16)
        kf, kb = lag_kernels
        diag = kf[0:hc, :] + kb[0:hc, :]
        for j in range(nb):
            acc = jnp.zeros((hc, SSM_ROWS), F32)
            for i in range(nb):
                if i < j:
                    src = kf[hc * (j - i):hc * (j - i + 1), :]
                elif i > j:
                    src = kb[hc * (i - j):hc * (i - j + 1), :]
                else:
                    src = diag
                acc = jnp.where(blk_of_lane == i, src, acc)
            t0_ref[gi, hc * j:hc * (j + 1), :] = acc.astype(BF16)
        return carry

    lax.fori_loop(0, gc, body, 0)


def _ssm_weights(lam_re, lam_im, log_dt, b_re, b_im, c_re, c_im, gc=8):
    g, p, hc = SSM_GROUPS, SSM_STATE, SSM_GROUP
    dt = jnp.exp(log_dt.astype(F32))[..., None]
    lr = lam_re.astype(F32)
    li = lam_im.astype(F32)
    mag = jnp.exp(lr * dt)
    ab_re = mag * jnp.cos(li * dt)
    ab_im = mag * jnp.sin(li * dt)
    den = lr * lr + li * li
    f_re = ((ab_re - 1.0) * lr + ab_im * li) / den
    f_im = (ab_im * lr - (ab_re - 1.0) * li) / den
    br = b_re.astype(F32)
    bi = b_im.astype(F32)
    bb_re = f_re[..., None] * br - f_im[..., None] * bi
    bb_im = f_re[..., None] * bi + f_im[..., None] * br
    mag16 = jnp.exp(SSM_BLOCK * (lr * dt))
    a16_re = mag16 * jnp.cos(SSM_BLOCK * (li * dt))
    a16_im = mag16 * jnp.sin(SSM_BLOCK * (li * dt))

    def pack(re, im):
        return jnp.stack([re[0], im[0], re[1], im[1]], axis=1)

    arow = pack(ab_re, ab_im)
    acol = arow.reshape(g, 4 * p, 1)
    a16 = pack(a16_re, a16_im).reshape(g, 4 * p, 1)
    cmat = pack(c_re.astype(F32), c_im.astype(F32))
    bw = jnp.tile(pack(bb_re, bb_im), (1, 1, 1, SSM_BLOCK))
    wspec = pl.BlockSpec((gc, SSM_ROWS, SSM_ROWS), lambda i: (i, 0, 0))
    wshape = jax.ShapeDtypeStruct((g, SSM_ROWS, SSM_ROWS), BF16)
    wz, t0, cc = pl.pallas_call(
        functools.partial(_ssm_prep_kernel, gc=gc),
        grid=(g // gc,),
        in_specs=[pl.BlockSpec((gc, 4, p), lambda i: (i, 0, 0)),
                  pl.BlockSpec((gc, 4 * p, 1), lambda i: (i, 0, 0)),
                  pl.BlockSpec((gc, 4, hc, p), lambda i: (i, 0, 0, 0)),
                  pl.BlockSpec((gc, 4, p, SSM_ROWS), lambda i: (i, 0, 0, 0))],
        out_specs=[wspec, wspec, wspec],
        out_shape=[wshape, wshape, wshape],
        compiler_params=_params("parallel"),
        name="ssm_prep",
    )(arow, acol, cmat, bw)
    return wz, t0, cc, a16


def _gelu_tanh(x):
    return 0.5 * x * (1.0 + jnp.tanh(math.sqrt(2.0 / math.pi) * (x + 0.044715 * (x * x * x))))


def _cmul(ar, ai, br, bi):
    return ar * br - ai * bi, ar * bi + ai * br


def _block_states(zs, coefs, nc):
    p = SSM_STATE
    ng = len(zs)
    rows = ng * p
    lane = lax.broadcasted_iota(jnp.int32, (rows, LANES), 1)
    zero = jnp.zeros((rows, LANES), F32)

    def part(arrs, r0, c=None):
        cols = slice(None) if c is None else slice(LANES * c, LANES * (c + 1))
        return jnp.concatenate([x[r0:r0 + p, cols] for x in arrs], axis=0)

    halves = []
    for dirn in range(2):
        base = 2 * p * dirn
        zr = [part(zs, base, c) for c in range(nc)]
        zi = [part(zs, base + p, c) for c in range(nc)]
        ar = jnp.broadcast_to(part(coefs, base), (rows, LANES))
        ai = jnp.broadcast_to(part(coefs, base + p), (rows, LANES))
        order = list(range(nc)) if dirn == 0 else list(range(nc - 1, -1, -1))
        ir, ii = {order[0]: zr[order[0]]}, {order[0]: zi[order[0]]}
        for prev, c in zip(order[:-1], order[1:]):
            tr, ti = _cmul(ar, ai, ir[prev], ii[prev])
            ir[c], ii[c] = zr[c] + tr, zi[c] + ti
        pows = [(ar, ai)]
        for _ in range(nc - 1):
            pows.append(_cmul(pows[-1][0], pows[-1][1], ar, ai))
        gr, gi = ir[order[-1]], ii[order[-1]]
        br, bi = pows[nc - 1]
        s = 1
        while s < LANES:
            keep = (lane >= s) if dirn == 0 else (lane < LANES - s)
            amt = s if dirn == 0 else LANES - s
            rr = jnp.where(keep, pltpu.roll(gr, amt, 1), zero)
            ri = jnp.where(keep, pltpu.roll(gi, amt, 1), zero)
            tr, ti = _cmul(br, bi, rr, ri)
            gr, gi = gr + tr, gi + ti
            br, bi = _cmul(br, bi, br, bi)
            s *= 2
        keep = (lane >= 1) if dirn == 0 else (lane < LANES - 1)
        amt = 1 if dirn == 0 else LANES - 1
        er = jnp.where(keep, pltpu.roll(gr, amt, 1), zero)
        ei = jnp.where(keep, pltpu.roll(gi, amt, 1), zero)
        hr, hi = [None] * nc, [None] * nc
        hr[order[0]], hi[order[0]] = er, ei
        for k, (prev, c) in enumerate(zip(order[:-1], order[1:])):
            tr, ti = _cmul(pows[k][0], pows[k][1], er, ei)
            hr[c], hi[c] = ir[prev] + tr, ii[prev] + ti
        halves += [jnp.concatenate(hr, axis=1), jnp.concatenate(hi, axis=1)]
    return [jnp.concatenate([h[g * p:(g + 1) * p] for h in halves], axis=0) for g in range(ng)]


def _ssm_kernel(u_ref, wz_ref, t0_ref, cc_ref, a_ref, d_ref, perm_ref, unperm_ref, y_ref, *, gc, ntau):
    nc = ntau // LANES
    rows = [slice(SSM_GROUP * gi, SSM_GROUP * (gi + 1)) for gi in range(gc)]
    zs = []
    for gi in range(gc):
        x = u_ref[:, rows[gi], :].reshape(SSM_ROWS, ntau).astype(BF16)
        xp = jnp.dot(x, perm_ref[...], preferred_element_type=F32).astype(BF16)
        zs.append(jnp.dot(wz_ref[gi], xp, preferred_element_type=F32))
    states = _block_states(zs, [a_ref[gi] for gi in range(gc)], nc)
    for gi in range(gc):
        xf = u_ref[:, rows[gi], :].reshape(SSM_ROWS, ntau)
        h = jnp.dot(states[gi].astype(BF16), unperm_ref[...], preferred_element_type=F32).astype(BF16)
        y = (jnp.dot(t0_ref[gi], xf.astype(BF16), preferred_element_type=F32)
             + jnp.dot(cc_ref[gi], h, preferred_element_type=F32))
        y = _gelu_tanh(y + d_ref[gi] * xf)
        y_ref[:, rows[gi], :] = y.astype(y_ref.dtype).reshape(SSM_BLOCK, SSM_GROUP, ntau)


def _ssm(u_t, wz, t0, cc, a16, d_rows, gc=8):
    b, _, _, ntau = u_t.shape
    nc = ntau // LANES
    j = jnp.arange(ntau)
    src = (j % LANES) * nc + j // LANES
    perm = (jnp.arange(ntau)[:, None] == src[None, :]).astype(BF16)
    wspec = pl.BlockSpec((gc, SSM_ROWS, SSM_ROWS), lambda bb, g: (g, 0, 0))
    vspec = pl.BlockSpec((gc, SSM_ROWS, 1), lambda bb, g: (g, 0, 0))
    uspec = pl.BlockSpec((None, SSM_BLOCK, SSM_GROUP * gc, ntau), lambda bb, g: (bb, 0, g, 0))
    pspec = pl.BlockSpec((ntau, ntau), lambda bb, g: (0, 0))
    return pl.pallas_call(
        functools.partial(_ssm_kernel, gc=gc, ntau=ntau),
        grid=(b, SSM_GROUPS // gc),
        in_specs=[uspec, wspec, wspec, wspec, vspec, vspec, pspec, pspec],
        out_specs=uspec,
        out_shape=jax.ShapeDtypeStruct(u_t.shape, BF16),
        compiler_params=_params("parallel", "parallel"),
        name="ssm",
    )(u_t, wz, t0, cc, a16, d_rows, perm, perm.T)


def _column_pipeline(width, chunk, issue, finish):
    chunks = [slice(c, c + chunk) for c in range(0, width, chunk)]
    pending = issue(chunks[0])
    for idx, cols in enumerate(chunks):
        current = pending
        if idx + 1 < len(chunks):
            pending = issue(chunks[idx + 1])
        finish(cols, current)


def _glu_kernel(y_ref, w1_ref, w2_ref, gb_ref, o_ref):
    pp, tmt, _ = o_ref.shape
    yt = jnp.concatenate([y_ref[p].T for p in range(pp)], axis=0)

    def issue(cols):
        return (jnp.dot(yt, w1_ref[:, cols], preferred_element_type=F32),
                jnp.dot(yt, w2_ref[:, cols], preferred_element_type=F32))

    def finish(cols, gl):
        gb = gb_ref[:, :, cols].reshape(pp * tmt, -1)
        out = (gl[0] * jax.nn.sigmoid(gl[1]) * (gb * jax.nn.sigmoid(gb))).astype(o_ref.dtype)
        o_ref[:, :, cols] = out.reshape(pp, tmt, -1)

    _column_pipeline(o_ref.shape[2], EPILOGUE_CHUNK, issue, finish)


def _glu(y_t, w_glu, p2, tokens=512, tn=2048):
    b, _, _, ntau = y_t.shape
    tmt = min(tokens, ntau)
    pp = tokens // tmt
    nj = SSM_WIDTH // tn
    p2v = p2.reshape(b, SSM_BLOCK, ntau, P2_WIDTH)
    out = pl.pallas_call(
        _glu_kernel,
        grid=(nj, b, SSM_BLOCK // pp, ntau // tmt),
        in_specs=[
            pl.BlockSpec((None, pp, SSM_WIDTH, tmt), lambda j, bb, i, tt: (bb, i, 0, tt)),
            pl.BlockSpec((SSM_WIDTH, tn), lambda j, bb, i, tt: (0, j)),
            pl.BlockSpec((SSM_WIDTH, tn), lambda j, bb, i, tt: (0, nj + j)),
            pl.BlockSpec((None, pp, tmt, tn), lambda j, bb, i, tt: (bb, i, tt, j)),
        ],
        out_specs=pl.BlockSpec((None, pp, tmt, tn), lambda j, bb, i, tt: (bb, i, tt, j)),
        out_shape=jax.ShapeDtypeStruct((b, SSM_BLOCK, ntau, SSM_WIDTH), BF16),
        compiler_params=_params("parallel", "parallel", "parallel", "parallel"),
        name="glu",
    )(y_t, w_glu, w_glu, p2v)
    return out.reshape(b * SSM_BLOCK * ntau, SSM_WIDTH)


def _merge_kernel(a_ref, y_ref, wa_ref, wb_ref, ma_ref, mb_ref, unperm_ref, o_ref):
    rows = o_ref.shape[0]
    a = a_ref[...].reshape(rows, ATTN_WIDTH)
    y = y_ref[...].reshape(rows, SSM_WIDTH)
    sub = unperm_ref.shape[0]
    n = sub // SSM_BLOCK

    def issue(cols):
        return (jnp.dot(a, wa_ref[:, cols], preferred_element_type=F32),
                jnp.dot(y, wb_ref[:, cols], preferred_element_type=F32))

    def finish(cols, z):
        ma = ma_ref[:, :, cols].reshape(rows, -1)
        mb = mb_ref[:, :, cols].reshape(rows, -1)
        merged = (jax.nn.sigmoid(ma) * z[0] + jax.nn.sigmoid(mb) * z[1]).astype(BF16)
        merged = merged.reshape(SSM_BLOCK, rows // SSM_BLOCK, -1)
        for k in range(rows // sub):
            part = merged[:, k * n:(k + 1) * n, :].reshape(sub, -1)
            o_ref[k * sub:(k + 1) * sub, cols] = jnp.dot(unperm_ref[...], part,
                                                         preferred_element_type=F32).astype(o_ref.dtype)

    _column_pipeline(o_ref.shape[1], EPILOGUE_CHUNK, issue, finish)


def _step_permutation(tm):
    n = tm // SSM_BLOCK
    r = jnp.arange(tm)
    src = SSM_BLOCK * (r % n) + r // n
    return (src[:, None] == jnp.arange(tm)[None, :]).astype(BF16)


def _merge(a, y2, w_attn_proj, w_ssm_proj, p2, b, s, tm=512, tn=1024):
    ntau = s // SSM_BLOCK
    n = tm // SSM_BLOCK
    ma0 = (COL_MA - COL_GB) // tn
    mb0 = (COL_MB - COL_GB) // tn

    def steps(arr, width):
        return arr.reshape(b, SSM_BLOCK, ntau, width)

    return pl.pallas_call(
        _merge_kernel,
        grid=(D_MODEL // tn, b, s // tm),
        in_specs=[
            pl.BlockSpec((None, SSM_BLOCK, n, ATTN_WIDTH), lambda j, bb, t: (bb, 0, t, 0)),
            pl.BlockSpec((None, SSM_BLOCK, n, SSM_WIDTH), lambda j, bb, t: (bb, 0, t, 0)),
            pl.BlockSpec((ATTN_WIDTH, tn), lambda j, bb, t: (0, j)),
            pl.BlockSpec((SSM_WIDTH, tn), lambda j, bb, t: (0, j)),
            pl.BlockSpec((None, SSM_BLOCK, n, tn), lambda j, bb, t: (bb, 0, t, ma0 + j)),
            pl.BlockSpec((None, SSM_BLOCK, n, tn), lambda j, bb, t: (bb, 0, t, mb0 + j)),
            pl.BlockSpec((PERM_TILE, PERM_TILE), lambda j, bb, t: (0, 0)),
        ],
        out_specs=pl.BlockSpec((tm, tn), lambda j, bb, t: (bb * (s // tm) + t, j)),
        out_shape=jax.ShapeDtypeStruct((b * s, D_MODEL), BF16),
        compiler_params=_params("parallel", "parallel", "parallel"),
        name="merge",
    )(steps(a, ATTN_WIDTH), steps(y2, SSM_WIDTH), w_attn_proj, w_ssm_proj,
      steps(p2, P2_WIDTH), steps(p2, P2_WIDTH), _step_permutation(PERM_TILE).T)


def _out_kernel(m_ref, w_ref, x_ref, o_ref):
    o_ref[...] = x_ref[...] + jnp.dot(m_ref[...], w_ref[...], preferred_element_type=F32)


def _out_proj(merged, w_out, x2, tm=1024, tn=1024):
    t = merged.shape[0]
    return pl.pallas_call(
        _out_kernel,
        grid=(D_MODEL // tn, t // tm),
        in_specs=[
            pl.BlockSpec((tm, D_MODEL), lambda j, i: (i, 0)),
            pl.BlockSpec((D_MODEL, tn), lambda j, i: (0, j)),
            pl.BlockSpec((tm, tn), lambda j, i: (i, j)),
        ],
        out_specs=pl.BlockSpec((tm, tn), lambda j, i: (i, j)),
        out_shape=jax.ShapeDtypeStruct((t, D_MODEL), F32),
        compiler_params=_params("parallel", "parallel"),
        name="out_proj",
    )(merged, w_out, x2)


def _layer(x, w):
    b, s, d = x.shape
    h_steps = _prenorm(x, w["norm_gain"])
    h2 = h_steps.reshape(b * s, d)
    p1 = _proj(h2, w["w_in"], COL_Q, P1_WIDTH, w["qk_gain_row"], 2 * ATTN_WIDTH, "proj_attn")
    p2 = _proj(h2, w["w_in"], COL_GB, P2_WIDTH, None, 0, "proj_gates")
    u_t = _proj_u(h_steps, w["wu_t"])
    a = _attention(p1, w["bias_tabs"], b, s).reshape(b * s, ATTN_WIDTH)
    y_t = _ssm(u_t, w["wz"], w["t0"], w["cc"], w["a16"], w["d_rows"])
    y2 = _glu(y_t, w["w_glu"], p2)
    merged = _merge(a, y2, w["w_attn_proj"], w["w_ssm_proj"], p2, b, s)
    return _out_proj(merged, w["w_out"], x.reshape(b * s, d)).reshape(b, s, d)


def _prepare(rel_bias, norm_gain, w_in, q_gain, k_gain, lam_re, lam_im, log_dt, b_re, b_im, c_re, c_im,
             d_skip, w_glu, w_attn_proj, w_ssm_proj, w_out):
    wz, t0, cc, a16 = _ssm_weights(lam_re, lam_im, log_dt, b_re, b_im, c_re, c_im)
    gains = jnp.concatenate([jnp.tile(q_gain.astype(F32) * (HEAD_DIM ** -0.5 * LOG2E), N_HEADS),
                             jnp.tile(k_gain.astype(F32), N_HEADS),
                             jnp.zeros((P1_WIDTH - 2 * ATTN_WIDTH,), F32)])
    d_rows = jnp.tile(d_skip.astype(F32).reshape(SSM_GROUPS, 1, SSM_GROUP), (1, SSM_BLOCK, 1))
    return {
        "norm_gain": norm_gain.astype(F32),
        "w_in": w_in.astype(BF16),
        "wu_t": w_in[:, COL_U:COL_U + SSM_WIDTH].T.astype(BF16),
        "qk_gain_row": gains.reshape(1, P1_WIDTH),
        "bias_tabs": _bias_tables(rel_bias),
        "wz": wz, "t0": t0, "cc": cc, "a16": a16,
        "d_rows": d_rows.reshape(SSM_GROUPS, SSM_ROWS, 1),
        "w_glu": w_glu.astype(BF16),
        "w_attn_proj": w_attn_proj.astype(BF16),
        "w_ssm_proj": w_ssm_proj.astype(BF16),
        "w_out": w_out.astype(BF16),
    }


def kernel(x_prompt, x_sample, rel_bias, norm_gain, w_in, q_gain, k_gain, lam_re, lam_im, log_dt, b_re, b_im,
           c_re, c_im, d_skip, w_glu, w_attn_proj, w_ssm_proj, w_out):
    y_prompt, y_sample = x_prompt, x_sample
    for l in range(norm_gain.shape[0]):
        w = _prepare(rel_bias, norm_gain[l], w_in[l], q_gain[l], k_gain[l], lam_re[l], lam_im[l], log_dt[l],
                     b_re[l], b_im[l], c_re[l], c_im[l], d_skip[l], w_glu[l], w_attn_proj[l], w_ssm_proj[l],
                     w_out[l])
        y_prompt = _layer(y_prompt, w)
        y_sample = _layer(y_sample, w)
    return (y_prompt, y_sample)
```

```python
import functools
import math

import jax
import jax.numpy as jnp
from jax import lax
from jax.experimental import pallas as pl
from jax.experimental.pallas import tpu as pltpu

F32 = jnp.float32
BF16 = jnp.bfloat16

D_MODEL = 4096
N_HEADS = 16
HEAD_DIM = 128
ATTN_WIDTH = N_HEADS * HEAD_DIM
SIDE_KEYS = 64
N_BUCKETS = 32
REL_MAX_DIST = 1024
NEG = -1e30
LOG2E = math.log2(math.e)
EPS = 1e-6
SSM_WIDTH = 2048
SSM_GROUP = 16
SSM_GROUPS = SSM_WIDTH // SSM_GROUP
SSM_STATE = 64
SSM_BLOCK = 16
SSM_ROWS = SSM_BLOCK * SSM_GROUP

COL_Q = 0
COL_K = ATTN_WIDTH
COL_V = 2 * ATTN_WIDTH
COL_GA = 3 * ATTN_WIDTH
COL_U = 4 * ATTN_WIDTH
COL_GB = COL_U + SSM_WIDTH
COL_MA = COL_GB + SSM_WIDTH
COL_MB = COL_MA + D_MODEL
P1_WIDTH = COL_U
P2_WIDTH = SSM_WIDTH + 2 * D_MODEL

LANES = 128
PLANES = SSM_BLOCK
PERM_TILE = 256
EPILOGUE_CHUNK = 512
GLU_CHUNK = 256
ATTN_GEOM = ((1, 8, 24), (4, 32, 64), (16, 128, 256))
ATTN_BATCH = (4, 4, 4)
ATTN_ROWS = 128
HALO_ROWS = SIDE_KEYS

VMEM_BYTES_V7X = 64 * 1024 * 1024
VMEM_LIMIT = VMEM_BYTES_V7X - 4 * 1024 * 1024


def _params(*sem):
    return pltpu.CompilerParams(dimension_semantics=sem, vmem_limit_bytes=VMEM_LIMIT)


def _prenorm_kernel(x_ref, g_ref, perm_ref, o_ref):
    x = x_ref[...]
    ms = jnp.mean(x * x, axis=-1, keepdims=True)
    h = (x * lax.rsqrt(ms + EPS) * g_ref[...]).astype(BF16)
    hp = jnp.dot(perm_ref[...], h, preferred_element_type=F32).astype(BF16)
    o_ref[...] = hp.reshape(o_ref.shape)


def _prenorm(x, gain, tm=PERM_TILE):
    b, s, d = x.shape
    return pl.pallas_call(
        _prenorm_kernel,
        grid=(b, s // tm),
        in_specs=[pl.BlockSpec((None, tm, d), lambda bb, t: (bb, t, 0)),
                  pl.BlockSpec((1, d), lambda bb, t: (0, 0)),
                  pl.BlockSpec((tm, tm), lambda bb, t: (0, 0))],
        out_specs=pl.BlockSpec((None, SSM_BLOCK, tm // SSM_BLOCK, d), lambda bb, t: (bb, 0, t, 0)),
        out_shape=jax.ShapeDtypeStruct((b, SSM_BLOCK, s // SSM_BLOCK, d), BF16),
        compiler_params=_params("parallel", "parallel"),
        name="prenorm",
    )(x, gain.reshape(1, d), _step_permutation(tm))


def _proj_kernel(h_ref, w_ref, g_ref, o_ref, *, n_norm_tiles):
    j = pl.program_id(0)

    @pl.when(j < n_norm_tiles)
    def _():
        def issue(cols):
            return jnp.dot(h_ref[...], w_ref[:, cols], preferred_element_type=F32)

        def finish(cols, acc):
            for hh in range(acc.shape[1] // HEAD_DIM):
                blk = acc[:, hh * HEAD_DIM:(hh + 1) * HEAD_DIM]
                sl = slice(cols.start + hh * HEAD_DIM, cols.start + (hh + 1) * HEAD_DIM)
                ms = jnp.mean(blk * blk, axis=-1, keepdims=True)
                o_ref[:, sl] = blk * lax.rsqrt(ms + EPS) * g_ref[:, sl]

        _column_pipeline(o_ref.shape[1], EPILOGUE_CHUNK, issue, finish)

    @pl.when(j >= n_norm_tiles)
    def _():
        o_ref[...] = jnp.dot(h_ref[...], w_ref[...], preferred_element_type=F32)


def _proj(h, w, col0, width, gain_row, n_norm_cols, name, tm=512, tn=2048):
    t, k = h.shape
    if gain_row is None:
        gain_row = jnp.zeros((1, width), F32)
    return pl.pallas_call(
        functools.partial(_proj_kernel, n_norm_tiles=n_norm_cols // tn),
        grid=(width // tn, t // tm),
        in_specs=[
            pl.BlockSpec((tm, k), lambda j, i: (i, 0)),
            pl.BlockSpec((k, tn), lambda j, i: (0, col0 // tn + j)),
            pl.BlockSpec((1, tn), lambda j, i: (0, j)),
        ],
        out_specs=pl.BlockSpec((tm, tn), lambda j, i: (i, j)),
        out_shape=jax.ShapeDtypeStruct((t, width), F32),
        compiler_params=_params("parallel", "parallel"),
        name=name,
    )(h, w, gain_row)


def _proj_u_kernel(w_ref, h_ref, o_ref):
    o_ref[...] = lax.dot_general(w_ref[...], h_ref[...], (((1,), (1,)), ((), ())),
                                 preferred_element_type=F32)


def _proj_u(h_steps, wu_t, tokens=512):
    b, _, ntau, _ = h_steps.shape
    tmt = min(tokens, ntau)
    return pl.pallas_call(
        _proj_u_kernel,
        grid=(b, SSM_BLOCK, ntau // tmt),
        in_specs=[
            pl.BlockSpec((SSM_WIDTH, D_MODEL), lambda bb, i, tt: (0, 0)),
            pl.BlockSpec((None, None, tmt, D_MODEL), lambda bb, i, tt: (bb, i, tt, 0)),
        ],
        out_specs=pl.BlockSpec((None, None, SSM_WIDTH, tmt), lambda bb, i, tt: (bb, i, 0, tt)),
        out_shape=jax.ShapeDtypeStruct((b, SSM_BLOCK, SSM_WIDTH, ntau), F32),
        compiler_params=_params("parallel", "parallel", "parallel"),
        name="proj_u",
    )(wu_t, h_steps)


def _rel_bucket(rel):
    half = N_BUCKETS // 2
    exact = half // 2
    base = jnp.where(rel > 0, half, 0)
    n = jnp.abs(rel)
    nf = jnp.maximum(n, 1).astype(F32)
    large = exact + (jnp.log(nf / exact) / math.log(REL_MAX_DIST / exact) * (half - exact)).astype(jnp.int32)
    large = jnp.minimum(large, half - 1)
    return base + jnp.where(n < exact, n, large)


def _bias_tables(rel_bias):
    buckets = jnp.arange(N_BUCKETS)
    tabs = []
    for d, nq, nk in ATTN_GEOM:
        g = PLANES // d
        koff = (nk - nq) // 2
        aq = jnp.arange(g)[:, None, None, None]
        tq = jnp.arange(nq)[None, :, None, None]
        ak = jnp.arange(g)[None, None, :, None]
        tk = jnp.arange(nk)[None, None, None, :]
        rel = (g * ((tk - koff) - tq) + (ak - aq)).reshape(g * nq, g * nk)
        onehot = (_rel_bucket(rel * d)[..., None] == buckets).astype(F32)
        bias = jnp.einsum('qkb,bh->hqk', onehot, rel_bias.astype(F32), precision=lax.Precision.HIGHEST)
        bias = jnp.where((jnp.abs(rel) <= SIDE_KEYS)[None], bias, NEG)
        trow = jnp.arange(g * nk) % nk - koff
        before = (trow < 0)[None, None, :]
        after = (trow >= nq)[None, None, :]
        bias = bias * LOG2E
        tabs.append(jnp.stack([bias, jnp.where(before, NEG, bias), jnp.where(after, NEG, bias)], axis=1))
    return tabs


def _attn_kernel(q_ref, kp_ref, kc_ref, kn_ref, vp_ref, vc_ref, vn_ref, ga_ref, t1_ref, t4_ref, t16_ref, o_ref,
                 obuf, lbuf):
    t = pl.program_id(2)
    first = t == 0
    last = t == pl.num_programs(2) - 1
    k_refs = (kp_ref, kc_ref, kn_ref)
    v_refs = (vp_ref, vc_ref, vn_ref)
    ones = jnp.ones((HEAD_DIM,), BF16)

    def halo_rows(refs, plane, start, n):
        pieces = []
        r0 = -HALO_ROWS
        for ref in refs:
            r1 = r0 + ref.shape[1]
            lo, hi = max(start, r0), min(start + n, r1)
            if lo < hi:
                pieces.append(ref[plane, lo - r0:hi - r0, :])
            r0 = r1
        return pieces

    edge_bias = {}

    def logits(pi, grp):
        d, nq, nk = ATTN_GEOM[pi]
        tab_ref = (t1_ref, t4_ref, t16_ref)[pi]
        g = PLANES // d
        nblk = ATTN_ROWS // nq
        k0 = -((nk - nq) // 2)

        def rows_of(refs, r, start, n):
            pieces = [piece for a in range(g) for piece in halo_rows(refs, r + d * a, start, n)]
            return jnp.concatenate(pieces, axis=0).astype(BF16)

        def bias_of(blk):
            if (pi, blk) not in edge_bias:
                bias = tab_ref[0]
                if blk == 0:
                    bias = jnp.where(first, tab_ref[1], bias)
                if blk == nblk - 1:
                    bias = jnp.minimum(bias, jnp.where(last, tab_ref[2], tab_ref[0]))
                edge_bias[(pi, blk)] = bias
            return edge_bias[(pi, blk)]

        q = jnp.stack([jnp.concatenate([q_ref[r + d * a, blk * nq:(blk + 1) * nq, :] for a in range(g)],
                                       axis=0).astype(BF16) for r, blk in grp])
        k = jnp.stack([rows_of(k_refs, r, blk * nq + k0, nk) for r, blk in grp])
        v = jnp.stack([rows_of(v_refs, r, blk * nq + k0, nk) for r, blk in grp])
        edges = {blk if blk in (0, nblk - 1) else None for _, blk in grp}
        if len(edges) == 1:
            blk = edges.pop()
            bias = (tab_ref[0] if blk is None else bias_of(blk))[None]
        else:
            bias = jnp.stack([bias_of(blk) if blk in (0, nblk - 1) else tab_ref[0] for _, blk in grp])
        return jnp.einsum('bqd,bkd->bqk', q, k, preferred_element_type=F32) + bias, v

    def finish(pi, grp, s, v):
        d, nq, _ = ATTN_GEOM[pi]
        g = PLANES // d
        m = jnp.max(s, axis=-1, keepdims=True)
        p = jnp.exp2(s - m).astype(BF16)
        v1 = jnp.concatenate([v, jnp.broadcast_to(ones, v.shape)], axis=-1)
        acc = jnp.einsum('bqk,bkd->bqd', p, v1, preferred_element_type=F32)
        den = acc[..., HEAD_DIM:]
        o = acc[..., :HEAD_DIM] / den
        lse = m + jnp.log2(den)
        for bi, (r, blk) in enumerate(grp):
            for a in range(g):
                rows = slice(blk * nq, (blk + 1) * nq)
                obuf[pi, r + d * a, rows, :] = o[bi, a * nq:(a + 1) * nq]
                lbuf[pi, r + d * a, rows, :] = lse[bi, a * nq:(a + 1) * nq]

    work = []
    for pi, (d, nq, _) in enumerate(ATTN_GEOM):
        blocks = [(r, blk) for r in range(d) for blk in range(ATTN_ROWS // nq)]
        work += [(pi, blocks[g0:g0 + ATTN_BATCH[pi]]) for g0 in range(0, len(blocks), ATTN_BATCH[pi])]
    pending = logits(*work[0])
    for idx, (pi, grp) in enumerate(work):
        s, v = pending
        if idx + 1 < len(work):
            pending = logits(*work[idx + 1])
        finish(pi, grp, s, v)

    for i in range(PLANES):
        l0, l1, l2 = lbuf[0, i], lbuf[1, i], lbuf[2, i]
        mx = jnp.maximum(jnp.maximum(l0, l1), l2)
        e0, e1, e2 = jnp.exp2(l0 - mx), jnp.exp2(l1 - mx), jnp.exp2(l2 - mx)
        o = (e0 * obuf[0, i] + e1 * obuf[1, i] + e2 * obuf[2, i]) / (e0 + e1 + e2)
        ga = ga_ref[i]
        o_ref[i] = (o * (ga * jax.nn.sigmoid(ga))).astype(o_ref.dtype)


def _attention(p1, bias_tabs, b, s):
    nrow = s // PLANES
    p1 = p1.reshape(b, PLANES, nrow, P1_WIDTH)
    per = ATTN_ROWS // HALO_ROWS
    n_halo = nrow // HALO_ROWS

    def tile(col):
        return pl.BlockSpec((None, PLANES, ATTN_ROWS, HEAD_DIM), lambda bb, h, t: (bb, 0, t, col // HEAD_DIM + h))

    def prev(col):
        return pl.BlockSpec((None, PLANES, HALO_ROWS, HEAD_DIM),
                            lambda bb, h, t: (bb, 0, jnp.maximum(t * per - 1, 0), col // HEAD_DIM + h))

    def nxt(col):
        return pl.BlockSpec((None, PLANES, HALO_ROWS, HEAD_DIM),
                            lambda bb, h, t: (bb, 0, jnp.minimum((t + 1) * per, n_halo - 1), col // HEAD_DIM + h))

    def table(tab):
        return pl.BlockSpec((None,) + tab.shape[1:], lambda bb, h, t: (h, 0, 0, 0))

    part_buf = pltpu.VMEM((len(ATTN_GEOM), PLANES, ATTN_ROWS, HEAD_DIM), F32)
    return pl.pallas_call(
        _attn_kernel,
        grid=(b, N_HEADS, nrow // ATTN_ROWS),
        in_specs=[tile(COL_Q), prev(COL_K), tile(COL_K), nxt(COL_K), prev(COL_V), tile(COL_V), nxt(COL_V),
                  tile(COL_GA)] + [table(tab) for tab in bias_tabs],
        out_specs=pl.BlockSpec((None, PLANES, ATTN_ROWS, HEAD_DIM), lambda bb, h, t: (bb, 0, t, h)),
        out_shape=jax.ShapeDtypeStruct((b, PLANES, nrow, ATTN_WIDTH), BF16),
        scratch_shapes=[part_buf, part_buf],
        compiler_params=_params("parallel", "parallel", "parallel"),
        name="attention",
    )(p1, p1, p1, p1, p1, p1, p1, p1, *bias_tabs)


def _ssm_prep_kernel(arow_ref, acol_ref, c_ref, bw_ref, wz_ref, t0_ref, cc_ref, *, gc):
    nb, hc, p = SSM_BLOCK, SSM_GROUP, SSM_STATE
    blk_of_lane = lax.broadcasted_iota(jnp.int32, (hc, SSM_ROWS), 1) // hc
    blk_of_lane_p = lax.broadcasted_iota(jnp.int32, (p, SSM_ROWS), 1) // hc

    def body(gi, carry):
        lag_kernels = []
        for dirn in range(2):
            ar = arow_ref[gi, 2 * dirn:2 * dirn + 1, :]
            ai = arow_ref[gi, 2 * dirn + 1:2 * dirn + 2, :]
            cr = c_ref[gi, 2 * dirn]
            ci = c_ref[gi, 2 * dirn + 1]
            pr = jnp.ones((1, p), F32)
            pi = jnp.zeros((1, p), F32)
            ca_re, ca_im = [], []
            for _ in range(nb + 1):
                ca_re.append(cr * pr - ci * pi)
                ca_im.append(cr * pi + ci * pr)
                pr, pi = pr * ar - pi * ai, pr * ai + pi * ar
            bre = bw_ref[gi, 2 * dirn]
            bim = bw_ref[gi, 2 * dirn + 1]
            lag_kernels.append(
                jnp.dot(jnp.concatenate(ca_re[:nb], axis=0).astype(BF16), bre.astype(BF16),
                        preferred_element_type=F32)
                - jnp.dot(jnp.concatenate(ca_im[:nb], axis=0).astype(BF16), bim.astype(BF16),
                          preferred_element_type=F32))
            order = range(1, nb + 1) if dirn == 0 else range(nb, 0, -1)
            c0 = 2 * p * dirn
            cc_ref[gi, :, c0:c0 + p] = jnp.concatenate([ca_re[k] for k in order], axis=0).astype(BF16)
            cc_ref[gi, :, c0 + p:c0 + 2 * p] = (-jnp.concatenate([ca_im[k] for k in order], axis=0)).astype(BF16)
            acr = acol_ref[gi, c0:c0 + p, :]
            aci = acol_ref[gi, c0 + p:c0 + 2 * p, :]
            qr = jnp.ones((p, 1), F32)
            qi = jnp.zeros((p, 1), F32)
            powers = []
            for _ in range(nb):
                powers.append((qr, qi))
                qr, qi = qr * acr - qi * aci, qr * aci + qi * acr
            per = jnp.zeros((p, SSM_ROWS), F32)
            pei = jnp.zeros((p, SSM_ROWS), F32)
            for i in range(nb):
                e = nb - 1 - i if dirn == 0 else i
                per = jnp.where(blk_of_lane_p == i, powers[e][0], per)
                pei = jnp.where(blk_of_lane_p == i, powers[e][1], pei)
            wz_ref[gi, c0:c0 + p, :] = (per * bre - pei * bim).astype(BF16)
            wz_ref[gi, c0 + p:c0 + 2 * p, :] = (per * bim + pei * bre).astype(BF16)
        kf, kb = lag_kernels
        diag = kf[0:hc, :] + kb[0:hc, :]
        for j in range(nb):
            acc = jnp.zeros((hc, SSM_ROWS), F32)
            for i in range(nb):
                if i < j:
                    src = kf[hc * (j - i):hc * (j - i + 1), :]
                elif i > j:
                    src = kb[hc * (i - j):hc * (i - j + 1), :]
                else:
                    src = diag
                acc = jnp.where(blk_of_lane == i, src, acc)
            t0_ref[gi, hc * j:hc * (j + 1), :] = acc.astype(BF16)
        return carry

    lax.fori_loop(0, gc, body, 0)


def _ssm_weights(lam_re, lam_im, log_dt, b_re, b_im, c_re, c_im, gc=8):
    g, p, hc = SSM_GROUPS, SSM_STATE, SSM_GROUP
    dt = jnp.exp(log_dt.astype(F32))[..., None]
    lr = lam_re.astype(F32)
    li = lam_im.astype(F32)
    mag = jnp.exp(lr * dt)
    ab_re = mag * jnp.cos(li * dt)
    ab_im = mag * jnp.sin(li * dt)
    den = lr * lr + li * li
    f_re = ((ab_re - 1.0) * lr + ab_im * li) / den
    f_im = (ab_im * lr - (ab_re - 1.0) * li) / den
    br = b_re.astype(F32)
    bi = b_im.astype(F32)
    bb_re = f_re[..., None] * br - f_im[..., None] * bi
    bb_im = f_re[..., None] * bi + f_im[..., None] * br
    mag16 = jnp.exp(SSM_BLOCK * (lr * dt))
    a16_re = mag16 * jnp.cos(SSM_BLOCK * (li * dt))
    a16_im = mag16 * jnp.sin(SSM_BLOCK * (li * dt))

    def pack(re, im):
        return jnp.stack([re[0], im[0], re[1], im[1]], axis=1)

    arow = pack(ab_re, ab_im)
    acol = arow.reshape(g, 4 * p, 1)
    a16 = pack(a16_re, a16_im).reshape(g, 4 * p, 1)
    cmat = pack(c_re.astype(F32), c_im.astype(F32))
    bw = jnp.tile(pack(bb_re, bb_im), (1, 1, 1, SSM_BLOCK))
    wspec = pl.BlockSpec((gc, SSM_ROWS, SSM_ROWS), lambda i: (i, 0, 0))
    wshape = jax.ShapeDtypeStruct((g, SSM_ROWS, SSM_ROWS), BF16)
    wz, t0, cc = pl.pallas_call(
        functools.partial(_ssm_prep_kernel, gc=gc),
        grid=(g // gc,),
        in_specs=[pl.BlockSpec((gc, 4, p), lambda i: (i, 0, 0)),
                  pl.BlockSpec((gc, 4 * p, 1), lambda i: (i, 0, 0)),
                  pl.BlockSpec((gc, 4, hc, p), lambda i: (i, 0, 0, 0)),
                  pl.BlockSpec((gc, 4, p, SSM_ROWS), lambda i: (i, 0, 0, 0))],
        out_specs=[wspec, wspec, wspec],
        out_shape=[wshape, wshape, wshape],
        compiler_params=_params("parallel"),
        name="ssm_prep",
    )(arow, acol, cmat, bw)
    return wz, t0, cc, a16


def _gelu_tanh(x):
    return 0.5 * x * (1.0 + jnp.tanh(math.sqrt(2.0 / math.pi) * (x + 0.044715 * (x * x * x))))


def _cmul(ar, ai, br, bi):
    return ar * br - ai * bi, ar * bi + ai * br


def _block_states(zs, coefs, nc):
    p = SSM_STATE
    ng = len(zs)
    rows = ng * p
    lane = lax.broadcasted_iota(jnp.int32, (rows, LANES), 1)
    zero = jnp.zeros((rows, LANES), F32)

    def part(arrs, r0, c=None):
        cols = slice(None) if c is None else slice(LANES * c, LANES * (c + 1))
        return jnp.concatenate([x[r0:r0 + p, cols] for x in arrs], axis=0)

    halves = []
    for dirn in range(2):
        base = 2 * p * dirn
        zr = [part(zs, base, c) for c in range(nc)]
        zi = [part(zs, base + p, c) for c in range(nc)]
        ar = jnp.broadcast_to(part(coefs, base), (rows, LANES))
        ai = jnp.broadcast_to(part(coefs, base + p), (rows, LANES))
        order = list(range(nc)) if dirn == 0 else list(range(nc - 1, -1, -1))
        ir, ii = {order[0]: zr[order[0]]}, {order[0]: zi[order[0]]}
        for prev, c in zip(order[:-1], order[1:]):
            tr, ti = _cmul(ar, ai, ir[prev], ii[prev])
            ir[c], ii[c] = zr[c] + tr, zi[c] + ti
        pows = [(ar, ai)]
        for _ in range(nc - 1):
            pows.append(_cmul(pows[-1][0], pows[-1][1], ar, ai))
        gr, gi = ir[order[-1]], ii[order[-1]]
        br, bi = pows[nc - 1]
        s = 1
        while s < LANES:
            keep = (lane >= s) if dirn == 0 else (lane < LANES - s)
            amt = s if dirn == 0 else LANES - s
            rr = jnp.where(keep, pltpu.roll(gr, amt, 1), zero)
            ri = jnp.where(keep, pltpu.roll(gi, amt, 1), zero)
            tr, ti = _cmul(br, bi, rr, ri)
            gr, gi = gr + tr, gi + ti
            br, bi = _cmul(br, bi, br, bi)
            s *= 2
        keep = (lane >= 1) if dirn == 0 else (lane < LANES - 1)
        amt = 1 if dirn == 0 else LANES - 1
        er = jnp.where(keep, pltpu.roll(gr, amt, 1), zero)
        ei = jnp.where(keep, pltpu.roll(gi, amt, 1), zero)
        hr, hi = [None] * nc, [None] * nc
        hr[order[0]], hi[order[0]] = er, ei
        for k, (prev, c) in enumerate(zip(order[:-1], order[1:])):
            tr, ti = _cmul(pows[k][0], pows[k][1], er, ei)
            hr[c], hi[c] = ir[prev] + tr, ii[prev] + ti
        halves += [jnp.concatenate(hr, axis=1), jnp.concatenate(hi, axis=1)]
    return [jnp.concatenate([h[g * p:(g + 1) * p] for h in halves], axis=0) for g in range(ng)]


def _ssm_kernel(u_ref, wz_ref, t0_ref, cc_ref, a_ref, d_ref, perm_ref, unperm_ref, y_ref, *, gc, ntau):
    nc = ntau // LANES
    rows = [slice(SSM_GROUP * gi, SSM_GROUP * (gi + 1)) for gi in range(gc)]
    zs = []
    for gi in range(gc):
        x = u_ref[:, rows[gi], :].reshape(SSM_ROWS, ntau).astype(BF16)
        xp = jnp.dot(x, perm_ref[...], preferred_element_type=F32).astype(BF16)
        zs.append(jnp.dot(wz_ref[gi], xp, preferred_element_type=F32))
    states = _block_states(zs, [a_ref[gi] for gi in range(gc)], nc)
    for gi in range(gc):
        xf = u_ref[:, rows[gi], :].reshape(SSM_ROWS, ntau)
        h = jnp.dot(states[gi].astype(BF16), unperm_ref[...], preferred_element_type=F32).astype(BF16)
        y = (jnp.dot(t0_ref[gi], xf.astype(BF16), preferred_element_type=F32)
             + jnp.dot(cc_ref[gi], h, preferred_element_type=F32))
        y = _gelu_tanh(y + d_ref[gi] * xf)
        y_ref[:, rows[gi], :] = y.astype(y_ref.dtype).reshape(SSM_BLOCK, SSM_GROUP, ntau)


def _ssm(u_t, wz, t0, cc, a16, d_rows, gc=8):
    b, _, _, ntau = u_t.shape
    nc = ntau // LANES
    j = jnp.arange(ntau)
    src = (j % LANES) * nc + j // LANES
    perm = (jnp.arange(ntau)[:, None] == src[None, :]).astype(BF16)
    wspec = pl.BlockSpec((gc, SSM_ROWS, SSM_ROWS), lambda bb, g: (g, 0, 0))
    vspec = pl.BlockSpec((gc, SSM_ROWS, 1), lambda bb, g: (g, 0, 0))
    uspec = pl.BlockSpec((None, SSM_BLOCK, SSM_GROUP * gc, ntau), lambda bb, g: (bb, 0, g, 0))
    pspec = pl.BlockSpec((ntau, ntau), lambda bb, g: (0, 0))
    return pl.pallas_call(
        functools.partial(_ssm_kernel, gc=gc, ntau=ntau),
        grid=(b, SSM_GROUPS // gc),
        in_specs=[uspec, wspec, wspec, wspec, vspec, vspec, pspec, pspec],
        out_specs=uspec,
        out_shape=jax.ShapeDtypeStruct(u_t.shape, BF16),
        compiler_params=_params("parallel", "parallel"),
        name="ssm",
    )(u_t, wz, t0, cc, a16, d_rows, perm, perm.T)


def _column_pipeline(width, chunk, issue, finish):
    chunks = [slice(c, c + chunk) for c in range(0, width, chunk)]
    pending = issue(chunks[0])
    for idx, cols in enumerate(chunks):
        current = pending
        if idx + 1 < len(chunks):
            pending = issue(chunks[idx + 1])
        finish(cols, current)


def _glu_kernel(y_ref, w1_ref, w2_ref, gb_ref, o_ref):
    pp, tmt, _ = o_ref.shape
    yt = jnp.concatenate([y_ref[p].T for p in range(pp)], axis=0)

    def issue(cols):
        return (jnp.dot(yt, w1_ref[:, cols], preferred_element_type=F32),
                jnp.dot(yt, w2_ref[:, cols], preferred_element_type=F32))

    def finish(cols, gl):
        gb = gb_ref[:, :, cols].reshape(pp * tmt, -1)
        out = (gl[0] * jax.nn.sigmoid(gl[1]) * (gb * jax.nn.sigmoid(gb))).astype(o_ref.dtype)
        o_ref[:, :, cols] = out.reshape(pp, tmt, -1)

    _column_pipeline(o_ref.shape[2], GLU_CHUNK, issue, finish)


def _glu(y_t, w_glu, p2, tokens=512, tn=2048):
    b, _, _, ntau = y_t.shape
    tmt = min(tokens, ntau)
    pp = tokens // tmt
    nj = SSM_WIDTH // tn
    p2v = p2.reshape(b, SSM_BLOCK, ntau, P2_WIDTH)
    out = pl.pallas_call(
        _glu_kernel,
        grid=(nj, b, SSM_BLOCK // pp, ntau // tmt),
        in_specs=[
            pl.BlockSpec((None, pp, SSM_WIDTH, tmt), lambda j, bb, i, tt: (bb, i, 0, tt)),
            pl.BlockSpec((SSM_WIDTH, tn), lambda j, bb, i, tt: (0, j)),
            pl.BlockSpec((SSM_WIDTH, tn), lambda j, bb, i, tt: (0, nj + j)),
            pl.BlockSpec((None, pp, tmt, tn), lambda j, bb, i, tt: (bb, i, tt, j)),
        ],
        out_specs=pl.BlockSpec((None, pp, tmt, tn), lambda j, bb, i, tt: (bb, i, tt, j)),
        out_shape=jax.ShapeDtypeStruct((b, SSM_BLOCK, ntau, SSM_WIDTH), BF16),
        compiler_params=_params("parallel", "parallel", "parallel", "parallel"),
        name="glu",
    )(y_t, w_glu, w_glu, p2v)
    return out.reshape(b * SSM_BLOCK * ntau, SSM_WIDTH)


def _merge_kernel(a_ref, y_ref, wa_ref, wb_ref, ma_ref, mb_ref, unperm_ref, o_ref):
    rows = o_ref.shape[0]
    a = a_ref[...].reshape(rows, ATTN_WIDTH)
    y = y_ref[...].reshape(rows, SSM_WIDTH)
    sub = unperm_ref.shape[0]
    n = sub // SSM_BLOCK

    def issue(cols):
        return (jnp.dot(a, wa_ref[:, cols], preferred_element_type=F32),
                jnp.dot(y, wb_ref[:, cols], preferred_element_type=F32))

    def finish(cols, z):
        ma = ma_ref[:, :, cols].reshape(rows, -1)
        mb = mb_ref[:, :, cols].reshape(rows, -1)
        merged = (jax.nn.sigmoid(ma) * z[0] + jax.nn.sigmoid(mb) * z[1]).astype(BF16)
        merged = merged.reshape(SSM_BLOCK, rows // SSM_BLOCK, -1)
        for k in range(rows // sub):
            part = merged[:, k * n:(k + 1) * n, :].reshape(sub, -1)
            o_ref[k * sub:(k + 1) * sub, cols] = jnp.dot(unperm_ref[...], part,
                                                         preferred_element_type=F32).astype(o_ref.dtype)

    _column_pipeline(o_ref.shape[1], EPILOGUE_CHUNK, issue, finish)


def _step_permutation(tm):
    n = tm // SSM_BLOCK
    r = jnp.arange(tm)
    src = SSM_BLOCK * (r % n) + r // n
    return (src[:, None] == jnp.arange(tm)[None, :]).astype(BF16)


def _merge(a, y2, w_attn_proj, w_ssm_proj, p2, b, s, tm=512, tn=1024):
    ntau = s // SSM_BLOCK
    n = tm // SSM_BLOCK
    ma0 = (COL_MA - COL_GB) // tn
    mb0 = (COL_MB - COL_GB) // tn

    def steps(arr, width):
        return arr.reshape(b, SSM_BLOCK, ntau, width)

    return pl.pallas_call(
        _merge_kernel,
        grid=(D_MODEL // tn, b, s // tm),
        in_specs=[
            pl.BlockSpec((None, SSM_BLOCK, n, ATTN_WIDTH), lambda j, bb, t: (bb, 0, t, 0)),
            pl.BlockSpec((None, SSM_BLOCK, n, SSM_WIDTH), lambda j, bb, t: (bb, 0, t, 0)),
            pl.BlockSpec((ATTN_WIDTH, tn), lambda j, bb, t: (0, j)),
            pl.BlockSpec((SSM_WIDTH, tn), lambda j, bb, t: (0, j)),
            pl.BlockSpec((None, SSM_BLOCK, n, tn), lambda j, bb, t: (bb, 0, t, ma0 + j)),
            pl.BlockSpec((None, SSM_BLOCK, n, tn), lambda j, bb, t: (bb, 0, t, mb0 + j)),
            pl.BlockSpec((PERM_TILE, PERM_TILE), lambda j, bb, t: (0, 0)),
        ],
        out_specs=pl.BlockSpec((tm, tn), lambda j, bb, t: (bb * (s // tm) + t, j)),
        out_shape=jax.ShapeDtypeStruct((b * s, D_MODEL), BF16),
        compiler_params=_params("parallel", "parallel", "parallel"),
        name="merge",
    )(steps(a, ATTN_WIDTH), steps(y2, SSM_WIDTH), w_attn_proj, w_ssm_proj,
      steps(p2, P2_WIDTH), steps(p2, P2_WIDTH), _step_permutation(PERM_TILE).T)


def _out_kernel(m_ref, w_ref, x_ref, o_ref):
    o_ref[...] = x_ref[...] + jnp.dot(m_ref[...], w_ref[...], preferred_element_type=F32)


def _out_proj(merged, w_out, x2, tm=1024, tn=1024):
    t = merged.shape[0]
    return pl.pallas_call(
        _out_kernel,
        grid=(D_MODEL // tn, t // tm),
        in_specs=[
            pl.BlockSpec((tm, D_MODEL), lambda j, i: (i, 0)),
            pl.BlockSpec((D_MODEL, tn), lambda j, i: (0, j)),
            pl.BlockSpec((tm, tn), lambda j, i: (i, j)),
        ],
        out_specs=pl.BlockSpec((tm, tn), lambda j, i: (i, j)),
        out_shape=jax.ShapeDtypeStruct((t, D_MODEL), F32),
        compiler_params=_params("parallel", "parallel"),
        name="out_proj",
    )(merged, w_out, x2)


def _layer(x, w):
    b, s, d = x.shape
    h_steps = _prenorm(x, w["norm_gain"])
    h2 = h_steps.reshape(b * s, d)
    p1 = _proj(h2, w["w_in"], COL_Q, P1_WIDTH, w["qk_gain_row"], 2 * ATTN_WIDTH, "proj_attn")
    p2 = _proj(h2, w["w_in"], COL_GB, P2_WIDTH, None, 0, "proj_gates")
    u_t = _proj_u(h_steps, w["wu_t"])
    a = _attention(p1, w["bias_tabs"], b, s).reshape(b * s, ATTN_WIDTH)
    y_t = _ssm(u_t, w["wz"], w["t0"], w["cc"], w["a16"], w["d_rows"])
    y2 = _glu(y_t, w["w_glu"], p2)
    merged = _merge(a, y2, w["w_attn_proj"], w["w_ssm_proj"], p2, b, s)
    return _out_proj(merged, w["w_out"], x.reshape(b * s, d)).reshape(b, s, d)


def _prepare(rel_bias, norm_gain, w_in, q_gain, k_gain, lam_re, lam_im, log_dt, b_re, b_im, c_re, c_im,
             d_skip, w_glu, w_attn_proj, w_ssm_proj, w_out):
    wz, t0, cc, a16 = _ssm_weights(lam_re, lam_im, log_dt, b_re, b_im, c_re, c_im)
    gains = jnp.concatenate([jnp.tile(q_gain.astype(F32) * (HEAD_DIM ** -0.5 * LOG2E), N_HEADS),
                             jnp.tile(k_gain.astype(F32), N_HEADS),
                             jnp.zeros((P1_WIDTH - 2 * ATTN_WIDTH,), F32)])
    d_rows = jnp.tile(d_skip.astype(F32).reshape(SSM_GROUPS, 1, SSM_GROUP), (1, SSM_BLOCK, 1))
    return {
        "norm_gain": norm_gain.astype(F32),
        "w_in": w_in.astype(BF16),
        "wu_t": w_in[:, COL_U:COL_U + SSM_WIDTH].T.astype(BF16),
        "qk_gain_row": gains.reshape(1, P1_WIDTH),
        "bias_tabs": _bias_tables(rel_bias),
        "wz": wz, "t0": t0, "cc": cc, "a16": a16,
        "d_rows": d_rows.reshape(SSM_GROUPS, SSM_ROWS, 1),
        "w_glu": w_glu.astype(BF16),
        "w_attn_proj": w_attn_proj.astype(BF16),
        "w_ssm_proj": w_ssm_proj.astype(BF16),
        "w_out": w_out.astype(BF16),
    }


def kernel(x_prompt, x_sample, rel_bias, norm_gain, w_in, q_gain, k_gain, lam_re, lam_im, log_dt, b_re, b_im,
           c_re, c_im, d_skip, w_glu, w_attn_proj, w_ssm_proj, w_out):
    y_prompt, y_sample = x_prompt, x_sample
    for l in range(norm_gain.shape[0]):
        w = _prepare(rel_bias, norm_gain[l], w_in[l], q_gain[l], k_gain[l], lam_re[l], lam_im[l], log_dt[l],
                     b_re[l], b_im[l], c_re[l], c_im[l], d_skip[l], w_glu[l], w_attn_proj[l], w_ssm_proj[l],
                     w_out[l])
        y_prompt = _layer(y_prompt, w)
        y_sample = _layer(y_sample, w)
    return (y_prompt, y_sample)
```

```python
import functools
import math

import jax
import jax.numpy as jnp
from jax import lax
from jax.experimental import pallas as pl
from jax.experimental.pallas import tpu as pltpu

F32 = jnp.float32
BF16 = jnp.bfloat16

D_MODEL = 4096
N_HEADS = 16
HEAD_DIM = 128
ATTN_WIDTH = N_HEADS * HEAD_DIM
SIDE_KEYS = 64
N_BUCKETS = 32
REL_MAX_DIST = 1024
NEG = -1e30
LOG2E = math.log2(math.e)
EPS = 1e-6
SSM_WIDTH = 2048
SSM_GROUP = 16
SSM_GROUPS = SSM_WIDTH // SSM_GROUP
SSM_STATE = 64
SSM_BLOCK = 16
SSM_ROWS = SSM_BLOCK * SSM_GROUP

COL_Q = 0
COL_K = ATTN_WIDTH
COL_V = 2 * ATTN_WIDTH
COL_GA = 3 * ATTN_WIDTH
COL_U = 4 * ATTN_WIDTH
COL_GB = COL_U + SSM_WIDTH
COL_MA = COL_GB + SSM_WIDTH
COL_MB = COL_MA + D_MODEL
P1_WIDTH = COL_U
P2_WIDTH = SSM_WIDTH + 2 * D_MODEL

LANES = 128
PLANES = SSM_BLOCK
PERM_TILE = 256
EPILOGUE_CHUNK = 512
GLU_CHUNK = 256
ATTN_GEOM = ((1, 8, 24), (4, 32, 64), (16, 128, 256))
ATTN_BATCH = (4, 4, 4)
ATTN_ROWS = 256
HALO_ROWS = SIDE_KEYS

VMEM_BYTES_V7X = 64 * 1024 * 1024
VMEM_LIMIT = VMEM_BYTES_V7X - 4 * 1024 * 1024


def _params(*sem):
    return pltpu.CompilerParams(dimension_semantics=sem, vmem_limit_bytes=VMEM_LIMIT)


def _prenorm_kernel(x_ref, g_ref, perm_ref, o_ref):
    x = x_ref[...]
    ms = jnp.mean(x * x, axis=-1, keepdims=True)
    h = (x * lax.rsqrt(ms + EPS) * g_ref[...]).astype(BF16)
    hp = jnp.dot(perm_ref[...], h, preferred_element_type=F32).astype(BF16)
    o_ref[...] = hp.reshape(o_ref.shape)


def _prenorm(x, gain, tm=PERM_TILE):
    b, s, d = x.shape
    return pl.pallas_call(
        _prenorm_kernel,
        grid=(b, s // tm),
        in_specs=[pl.BlockSpec((None, tm, d), lambda bb, t: (bb, t, 0)),
                  pl.BlockSpec((1, d), lambda bb, t: (0, 0)),
                  pl.BlockSpec((tm, tm), lambda bb, t: (0, 0))],
        out_specs=pl.BlockSpec((None, SSM_BLOCK, tm // SSM_BLOCK, d), lambda bb, t: (bb, 0, t, 0)),
        out_shape=jax.ShapeDtypeStruct((b, SSM_BLOCK, s // SSM_BLOCK, d), BF16),
        compiler_params=_params("parallel", "parallel"),
        name="prenorm",
    )(x, gain.reshape(1, d), _step_permutation(tm))


def _proj_kernel(h_ref, w_ref, g_ref, *refs, n_norm_tiles, n_side):
    side_in, o_ref, side_out = refs[:n_side], refs[n_side], refs[n_side + 1:]
    j = pl.program_id(0)

    for src, dst in zip(side_in, side_out):
        dst[...] = src[...].astype(BF16)

    @pl.when(j < n_norm_tiles)
    def _():
        def issue(cols):
            return jnp.dot(h_ref[...], w_ref[:, cols], preferred_element_type=F32)

        def finish(cols, acc):
            for hh in range(acc.shape[1] // HEAD_DIM):
                blk = acc[:, hh * HEAD_DIM:(hh + 1) * HEAD_DIM]
                sl = slice(cols.start + hh * HEAD_DIM, cols.start + (hh + 1) * HEAD_DIM)
                ms = jnp.mean(blk * blk, axis=-1, keepdims=True)
                o_ref[:, sl] = blk * lax.rsqrt(ms + EPS) * g_ref[:, sl]

        _column_pipeline(o_ref.shape[1], EPILOGUE_CHUNK, issue, finish)

    @pl.when(j >= n_norm_tiles)
    def _():
        o_ref[...] = jnp.dot(h_ref[...], w_ref[...], preferred_element_type=F32)


def _proj(h, w, col0, width, gain_row, n_norm_cols, name, side=(), tm=512, tn=2048):
    t, k = h.shape
    if gain_row is None:
        gain_row = jnp.zeros((1, width), F32)
    nm = t // tm
    nsteps = (width // tn) * nm

    def side_spec(mat):
        return pl.BlockSpec((mat.shape[0] // nsteps, mat.shape[1]), lambda j, i: (j * nm + i, 0))

    out = pl.pallas_call(
        functools.partial(_proj_kernel, n_norm_tiles=n_norm_cols // tn, n_side=len(side)),
        grid=(width // tn, nm),
        in_specs=[
            pl.BlockSpec((tm, k), lambda j, i: (i, 0)),
            pl.BlockSpec((k, tn), lambda j, i: (0, col0 // tn + j)),
            pl.BlockSpec((1, tn), lambda j, i: (0, j)),
        ] + [side_spec(mat) for mat in side],
        out_specs=[pl.BlockSpec((tm, tn), lambda j, i: (i, j))] + [side_spec(mat) for mat in side],
        out_shape=[jax.ShapeDtypeStruct((t, width), F32)]
        + [jax.ShapeDtypeStruct(mat.shape, BF16) for mat in side],
        compiler_params=_params("parallel", "parallel"),
        name=name,
    )(h, w, gain_row, *side)
    return out[0], tuple(out[1:])


def _proj_u_kernel(w_ref, h_ref, o_ref):
    o_ref[...] = lax.dot_general(w_ref[...], h_ref[...], (((1,), (1,)), ((), ())),
                                 preferred_element_type=F32)


def _proj_u(h_steps, wu_t, tokens=512):
    b, _, ntau, _ = h_steps.shape
    tmt = min(tokens, ntau)
    return pl.pallas_call(
        _proj_u_kernel,
        grid=(b, SSM_BLOCK, ntau // tmt),
        in_specs=[
            pl.BlockSpec((SSM_WIDTH, D_MODEL), lambda bb, i, tt: (0, 0)),
            pl.BlockSpec((None, None, tmt, D_MODEL), lambda bb, i, tt: (bb, i, tt, 0)),
        ],
        out_specs=pl.BlockSpec((None, None, SSM_WIDTH, tmt), lambda bb, i, tt: (bb, i, 0, tt)),
        out_shape=jax.ShapeDtypeStruct((b, SSM_BLOCK, SSM_WIDTH, ntau), F32),
        compiler_params=_params("parallel", "parallel", "parallel"),
        name="proj_u",
    )(wu_t, h_steps)


def _rel_bucket(rel):
    half = N_BUCKETS // 2
    exact = half // 2
    base = jnp.where(rel > 0, half, 0)
    n = jnp.abs(rel)
    nf = jnp.maximum(n, 1).astype(F32)
    large = exact + (jnp.log(nf / exact) / math.log(REL_MAX_DIST / exact) * (half - exact)).astype(jnp.int32)
    large = jnp.minimum(large, half - 1)
    return base + jnp.where(n < exact, n, large)


def _bias_tables(rel_bias):
    buckets = jnp.arange(N_BUCKETS)
    tabs = []
    for d, nq, nk in ATTN_GEOM:
        g = PLANES // d
        koff = (nk - nq) // 2
        aq = jnp.arange(g)[:, None, None, None]
        tq = jnp.arange(nq)[None, :, None, None]
        ak = jnp.arange(g)[None, None, :, None]
        tk = jnp.arange(nk)[None, None, None, :]
        rel = (g * ((tk - koff) - tq) + (ak - aq)).reshape(g * nq, g * nk)
        onehot = (_rel_bucket(rel * d)[..., None] == buckets).astype(F32)
        bias = jnp.einsum('qkb,bh->hqk', onehot, rel_bias.astype(F32), precision=lax.Precision.HIGHEST)
        bias = jnp.where((jnp.abs(rel) <= SIDE_KEYS)[None], bias, NEG)
        trow = jnp.arange(g * nk) % nk - koff
        before = (trow < 0)[None, None, :]
        after = (trow >= nq)[None, None, :]
        bias = bias * LOG2E
        tabs.append(jnp.stack([bias, jnp.where(before, NEG, bias), jnp.where(after, NEG, bias)], axis=1))
    return tabs


def _attn_kernel(q_ref, kp_ref, kc_ref, kn_ref, vp_ref, vc_ref, vn_ref, ga_ref, t1_ref, t4_ref, t16_ref, o_ref,
                 obuf, lbuf):
    t = pl.program_id(2)
    first = t == 0
    last = t == pl.num_programs(2) - 1
    k_refs = (kp_ref, kc_ref, kn_ref)
    v_refs = (vp_ref, vc_ref, vn_ref)
    ones = jnp.ones((HEAD_DIM,), BF16)

    def halo_rows(refs, plane, start, n):
        pieces = []
        r0 = -HALO_ROWS
        for ref in refs:
            r1 = r0 + ref.shape[1]
            lo, hi = max(start, r0), min(start + n, r1)
            if lo < hi:
                pieces.append(ref[plane, lo - r0:hi - r0, :])
            r0 = r1
        return pieces

    edge_bias = {}

    def logits(pi, grp):
        d, nq, nk = ATTN_GEOM[pi]
        tab_ref = (t1_ref, t4_ref, t16_ref)[pi]
        g = PLANES // d
        nblk = ATTN_ROWS // nq
        k0 = -((nk - nq) // 2)

        def rows_of(refs, r, start, n):
            pieces = [piece for a in range(g) for piece in halo_rows(refs, r + d * a, start, n)]
            return jnp.concatenate(pieces, axis=0).astype(BF16)

        def bias_of(blk):
            if (pi, blk) not in edge_bias:
                bias = tab_ref[0]
                if blk == 0:
                    bias = jnp.where(first, tab_ref[1], bias)
                if blk == nblk - 1:
                    bias = jnp.minimum(bias, jnp.where(last, tab_ref[2], tab_ref[0]))
                edge_bias[(pi, blk)] = bias
            return edge_bias[(pi, blk)]

        q = jnp.stack([jnp.concatenate([q_ref[r + d * a, blk * nq:(blk + 1) * nq, :] for a in range(g)],
                                       axis=0).astype(BF16) for r, blk in grp])
        k = jnp.stack([rows_of(k_refs, r, blk * nq + k0, nk) for r, blk in grp])
        v = jnp.stack([rows_of(v_refs, r, blk * nq + k0, nk) for r, blk in grp])
        edges = {blk if blk in (0, nblk - 1) else None for _, blk in grp}
        if len(edges) == 1:
            blk = edges.pop()
            bias = (tab_ref[0] if blk is None else bias_of(blk))[None]
        else:
            bias = jnp.stack([bias_of(blk) if blk in (0, nblk - 1) else tab_ref[0] for _, blk in grp])
        return jnp.einsum('bqd,bkd->bqk', q, k, preferred_element_type=F32) + bias, v

    def finish(pi, grp, s, v):
        d, nq, _ = ATTN_GEOM[pi]
        g = PLANES // d
        m = jnp.max(s, axis=-1, keepdims=True)
        p = jnp.exp2(s - m).astype(BF16)
        v1 = jnp.concatenate([v, jnp.broadcast_to(ones, v.shape)], axis=-1)
        acc = jnp.einsum('bqk,bkd->bqd', p, v1, preferred_element_type=F32)
        den = acc[..., HEAD_DIM:]
        o = acc[..., :HEAD_DIM] / den
        lse = m + jnp.log2(den)
        for bi, (r, blk) in enumerate(grp):
            for a in range(g):
                rows = slice(blk * nq, (blk + 1) * nq)
                obuf[pi, r + d * a, rows, :] = o[bi, a * nq:(a + 1) * nq]
                lbuf[pi, r + d * a, rows, :] = lse[bi, a * nq:(a + 1) * nq]

    work = []
    for pi, (d, nq, _) in enumerate(ATTN_GEOM):
        blocks = [(r, blk) for r in range(d) for blk in range(ATTN_ROWS // nq)]
        work += [(pi, blocks[g0:g0 + ATTN_BATCH[pi]]) for g0 in range(0, len(blocks), ATTN_BATCH[pi])]
    pending = logits(*work[0])
    for idx, (pi, grp) in enumerate(work):
        s, v = pending
        if idx + 1 < len(work):
            pending = logits(*work[idx + 1])
        finish(pi, grp, s, v)

    for i in range(PLANES):
        l0, l1, l2 = lbuf[0, i], lbuf[1, i], lbuf[2, i]
        mx = jnp.maximum(jnp.maximum(l0, l1), l2)
        e0, e1, e2 = jnp.exp2(l0 - mx), jnp.exp2(l1 - mx), jnp.exp2(l2 - mx)
        o = (e0 * obuf[0, i] + e1 * obuf[1, i] + e2 * obuf[2, i]) / (e0 + e1 + e2)
        ga = ga_ref[i]
        o_ref[i] = (o * (ga * jax.nn.sigmoid(ga))).astype(o_ref.dtype)


def _attention(p1, bias_tabs, b, s):
    nrow = s // PLANES
    p1 = p1.reshape(b, PLANES, nrow, P1_WIDTH)
    per = ATTN_ROWS // HALO_ROWS
    n_halo = nrow // HALO_ROWS

    def tile(col):
        return pl.BlockSpec((None, PLANES, ATTN_ROWS, HEAD_DIM), lambda bb, h, t: (bb, 0, t, col // HEAD_DIM + h))

    def prev(col):
        return pl.BlockSpec((None, PLANES, HALO_ROWS, HEAD_DIM),
                            lambda bb, h, t: (bb, 0, jnp.maximum(t * per - 1, 0), col // HEAD_DIM + h))

    def nxt(col):
        return pl.BlockSpec((None, PLANES, HALO_ROWS, HEAD_DIM),
                            lambda bb, h, t: (bb, 0, jnp.minimum((t + 1) * per, n_halo - 1), col // HEAD_DIM + h))

    def table(tab):
        return pl.BlockSpec((None,) + tab.shape[1:], lambda bb, h, t: (h, 0, 0, 0))

    part_buf = pltpu.VMEM((len(ATTN_GEOM), PLANES, ATTN_ROWS, HEAD_DIM), F32)
    return pl.pallas_call(
        _attn_kernel,
        grid=(b, N_HEADS, nrow // ATTN_ROWS),
        in_specs=[tile(COL_Q), prev(COL_K), tile(COL_K), nxt(COL_K), prev(COL_V), tile(COL_V), nxt(COL_V),
                  tile(COL_GA)] + [table(tab) for tab in bias_tabs],
        out_specs=pl.BlockSpec((None, PLANES, ATTN_ROWS, HEAD_DIM), lambda bb, h, t: (bb, 0, t, h)),
        out_shape=jax.ShapeDtypeStruct((b, PLANES, nrow, ATTN_WIDTH), BF16),
        scratch_shapes=[part_buf, part_buf],
        compiler_params=_params("parallel", "parallel", "parallel"),
        name="attention",
    )(p1, p1, p1, p1, p1, p1, p1, p1, *bias_tabs)


def _ssm_prep_kernel(arow_ref, acol_ref, c_ref, bw_ref, wz_ref, t0_ref, cc_ref, *, gc):
    nb, hc, p = SSM_BLOCK, SSM_GROUP, SSM_STATE
    blk_of_lane = lax.broadcasted_iota(jnp.int32, (hc, SSM_ROWS), 1) // hc
    blk_of_lane_p = lax.broadcasted_iota(jnp.int32, (p, SSM_ROWS), 1) // hc

    def body(gi, carry):
        lag_kernels = []
        for dirn in range(2):
            ar = arow_ref[gi, 2 * dirn:2 * dirn + 1, :]
            ai = arow_ref[gi, 2 * dirn + 1:2 * dirn + 2, :]
            cr = c_ref[gi, 2 * dirn]
            ci = c_ref[gi, 2 * dirn + 1]
            pr = jnp.ones((1, p), F32)
            pi = jnp.zeros((1, p), F32)
            ca_re, ca_im = [], []
            for _ in range(nb + 1):
                ca_re.append(cr * pr - ci * pi)
                ca_im.append(cr * pi + ci * pr)
                pr, pi = pr * ar - pi * ai, pr * ai + pi * ar
            bre = bw_ref[gi, 2 * dirn]
            bim = bw_ref[gi, 2 * dirn + 1]
            lag_kernels.append(
                jnp.dot(jnp.concatenate(ca_re[:nb], axis=0).astype(BF16), bre.astype(BF16),
                        preferred_element_type=F32)
                - jnp.dot(jnp.concatenate(ca_im[:nb], axis=0).astype(BF16), bim.astype(BF16),
                          preferred_element_type=F32))
            order = range(1, nb + 1) if dirn == 0 else range(nb, 0, -1)
            c0 = 2 * p * dirn
            cc_ref[gi, :, c0:c0 + p] = jnp.concatenate([ca_re[k] for k in order], axis=0).astype(BF16)
            cc_ref[gi, :, c0 + p:c0 + 2 * p] = (-jnp.concatenate([ca_im[k] for k in order], axis=0)).astype(BF16)
            acr = acol_ref[gi, c0:c0 + p, :]
            aci = acol_ref[gi, c0 + p:c0 + 2 * p, :]
            qr = jnp.ones((p, 1), F32)
            qi = jnp.zeros((p, 1), F32)
            powers = []
            for _ in range(nb):
                powers.append((qr, qi))
                qr, qi = qr * acr - qi * aci, qr * aci + qi * acr
            per = jnp.zeros((p, SSM_ROWS), F32)
            pei = jnp.zeros((p, SSM_ROWS), F32)
            for i in range(nb):
                e = nb - 1 - i if dirn == 0 else i
                per = jnp.where(blk_of_lane_p == i, powers[e][0], per)
                pei = jnp.where(blk_of_lane_p == i, powers[e][1], pei)
            wz_ref[gi, c0:c0 + p, :] = (per * bre - pei * bim).astype(BF16)
            wz_ref[gi, c0 + p:c0 + 2 * p, :] = (per * bim + pei * bre).astype(BF16)
        kf, kb = lag_kernels
        diag = kf[0:hc, :] + kb[0:hc, :]
        for j in range(nb):
            acc = jnp.zeros((hc, SSM_ROWS), F32)
            for i in range(nb):
                if i < j:
                    src = kf[hc * (j - i):hc * (j - i + 1), :]
                elif i > j:
                    src = kb[hc * (i - j):hc * (i - j + 1), :]
                else:
                    src = diag
                acc = jnp.where(blk_of_lane == i, src, acc)
            t0_ref[gi, hc * j:hc * (j + 1), :] = acc.astype(BF16)
        return carry

    lax.fori_loop(0, gc, body, 0)


def _ssm_weights(lam_re, lam_im, log_dt, b_re, b_im, c_re, c_im, gc=8):
    g, p, hc = SSM_GROUPS, SSM_STATE, SSM_GROUP
    dt = jnp.exp(log_dt.astype(F32))[..., None]
    lr = lam_re.astype(F32)
    li = lam_im.astype(F32)
    mag = jnp.exp(lr * dt)
    ab_re = mag * jnp.cos(li * dt)
    ab_im = mag * jnp.sin(li * dt)
    den = lr * lr + li * li
    f_re = ((ab_re - 1.0) * lr + ab_im * li) / den
    f_im = (ab_im * lr - (ab_re - 1.0) * li) / den
    br = b_re.astype(F32)
    bi = b_im.astype(F32)
    bb_re = f_re[..., None] * br - f_im[..., None] * bi
    bb_im = f_re[..., None] * bi + f_im[..., None] * br
    mag16 = jnp.exp(SSM_BLOCK * (lr * dt))
    a16_re = mag16 * jnp.cos(SSM_BLOCK * (li * dt))
    a16_im = mag16 * jnp.sin(SSM_BLOCK * (li * dt))

    def pack(re, im):
        return jnp.stack([re[0], im[0], re[1], im[1]], axis=1)

    arow = pack(ab_re, ab_im)
    acol = arow.reshape(g, 4 * p, 1)
    a16 = pack(a16_re, a16_im).reshape(g, 4 * p, 1)
    cmat = pack(c_re.astype(F32), c_im.astype(F32))
    bw = jnp.tile(pack(bb_re, bb_im), (1, 1, 1, SSM_BLOCK))
    wspec = pl.BlockSpec((gc, SSM_ROWS, SSM_ROWS), lambda i: (i, 0, 0))
    wshape = jax.ShapeDtypeStruct((g, SSM_ROWS, SSM_ROWS), BF16)
    wz, t0, cc = pl.pallas_call(
        functools.partial(_ssm_prep_kernel, gc=gc),
        grid=(g // gc,),
        in_specs=[pl.BlockSpec((gc, 4, p), lambda i: (i, 0, 0)),
                  pl.BlockSpec((gc, 4 * p, 1), lambda i: (i, 0, 0)),
                  pl.BlockSpec((gc, 4, hc, p), lambda i: (i, 0, 0, 0)),
                  pl.BlockSpec((gc, 4, p, SSM_ROWS), lambda i: (i, 0, 0, 0))],
        out_specs=[wspec, wspec, wspec],
        out_shape=[wshape, wshape, wshape],
        compiler_params=_params("parallel"),
        name="ssm_prep",
    )(arow, acol, cmat, bw)
    return wz, t0, cc, a16


def _gelu_tanh(x):
    return 0.5 * x * (1.0 + jnp.tanh(math.sqrt(2.0 / math.pi) * (x + 0.044715 * (x * x * x))))


def _cmul(ar, ai, br, bi):
    return ar * br - ai * bi, ar * bi + ai * br


def _block_states(zs, coefs, nc):
    p = SSM_STATE
    ng = len(zs)
    rows = ng * p
    lane = lax.broadcasted_iota(jnp.int32, (rows, LANES), 1)
    zero = jnp.zeros((rows, LANES), F32)

    def part(arrs, r0, c=None):
        cols = slice(None) if c is None else slice(LANES * c, LANES * (c + 1))
        return jnp.concatenate([x[r0:r0 + p, cols] for x in arrs], axis=0)

    halves = []
    for dirn in range(2):
        base = 2 * p * dirn
        zr = [part(zs, base, c) for c in range(nc)]
        zi = [part(zs, base + p, c) for c in range(nc)]
        ar = jnp.broadcast_to(part(coefs, base), (rows, LANES))
        ai = jnp.broadcast_to(part(coefs, base + p), (rows, LANES))
        order = list(range(nc)) if dirn == 0 else list(range(nc - 1, -1, -1))
        ir, ii = {order[0]: zr[order[0]]}, {order[0]: zi[order[0]]}
        for prev, c in zip(order[:-1], order[1:]):
            tr, ti = _cmul(ar, ai, ir[prev], ii[prev])
            ir[c], ii[c] = zr[c] + tr, zi[c] + ti
        pows = [(ar, ai)]
        for _ in range(nc - 1):
            pows.append(_cmul(pows[-1][0], pows[-1][1], ar, ai))
        gr, gi = ir[order[-1]], ii[order[-1]]
        br, bi = pows[nc - 1]
        s = 1
        while s < LANES:
            keep = (lane >= s) if dirn == 0 else (lane < LANES - s)
            amt = s if dirn == 0 else LANES - s
            rr = jnp.where(keep, pltpu.roll(gr, amt, 1), zero)
            ri = jnp.where(keep, pltpu.roll(gi, amt, 1), zero)
            tr, ti = _cmul(br, bi, rr, ri)
            gr, gi = gr + tr, gi + ti
            br, bi = _cmul(br, bi, br, bi)
            s *= 2
        keep = (lane >= 1) if dirn == 0 else (lane < LANES - 1)
        amt = 1 if dirn == 0 else LANES - 1
        er = jnp.where(keep, pltpu.roll(gr, amt, 1), zero)
        ei = jnp.where(keep, pltpu.roll(gi, amt, 1), zero)
        hr, hi = [None] * nc, [None] * nc
        hr[order[0]], hi[order[0]] = er, ei
        for k, (prev, c) in enumerate(zip(order[:-1], order[1:])):
            tr, ti = _cmul(pows[k][0], pows[k][1], er, ei)
            hr[c], hi[c] = ir[prev] + tr, ii[prev] + ti
        halves += [jnp.concatenate(hr, axis=1), jnp.concatenate(hi, axis=1)]
    return [jnp.concatenate([h[g * p:(g + 1) * p] for h in halves], axis=0) for g in range(ng)]


def _ssm_kernel(u_ref, wz_ref, t0_ref, cc_ref, a_ref, d_ref, perm_ref, unperm_ref, y_ref, *, gc, ntau):
    nc = ntau // LANES
    rows = [slice(SSM_GROUP * gi, SSM_GROUP * (gi + 1)) for gi in range(gc)]
    zs = []
    for gi in range(gc):
        x = u_ref[:, rows[gi], :].reshape(SSM_ROWS, ntau).astype(BF16)
        xp = jnp.dot(x, perm_ref[...], preferred_element_type=F32).astype(BF16)
        zs.append(jnp.dot(wz_ref[gi], xp, preferred_element_type=F32))
    states = _block_states(zs, [a_ref[gi] for gi in range(gc)], nc)
    for gi in range(gc):
        xf = u_ref[:, rows[gi], :].reshape(SSM_ROWS, ntau)
        h = jnp.dot(states[gi].astype(BF16), unperm_ref[...], preferred_element_type=F32).astype(BF16)
        y = (jnp.dot(t0_ref[gi], xf.astype(BF16), preferred_element_type=F32)
             + jnp.dot(cc_ref[gi], h, preferred_element_type=F32))
        y = _gelu_tanh(y + d_ref[gi] * xf)
        y_ref[:, rows[gi], :] = y.astype(y_ref.dtype).reshape(SSM_BLOCK, SSM_GROUP, ntau)


def _ssm(u_t, wz, t0, cc, a16, d_rows, gc=8):
    b, _, _, ntau = u_t.shape
    nc = ntau // LANES
    j = jnp.arange(ntau)
    src = (j % LANES) * nc + j // LANES
    perm = (jnp.arange(ntau)[:, None] == src[None, :]).astype(BF16)
    wspec = pl.BlockSpec((gc, SSM_ROWS, SSM_ROWS), lambda bb, g: (g, 0, 0))
    vspec = pl.BlockSpec((gc, SSM_ROWS, 1), lambda bb, g: (g, 0, 0))
    uspec = pl.BlockSpec((None, SSM_BLOCK, SSM_GROUP * gc, ntau), lambda bb, g: (bb, 0, g, 0))
    pspec = pl.BlockSpec((ntau, ntau), lambda bb, g: (0, 0))
    return pl.pallas_call(
        functools.partial(_ssm_kernel, gc=gc, ntau=ntau),
        grid=(b, SSM_GROUPS // gc),
        in_specs=[uspec, wspec, wspec, wspec, vspec, vspec, pspec, pspec],
        out_specs=uspec,
        out_shape=jax.ShapeDtypeStruct(u_t.shape, BF16),
        compiler_params=_params("parallel", "parallel"),
        name="ssm",
    )(u_t, wz, t0, cc, a16, d_rows, perm, perm.T)


def _column_pipeline(width, chunk, issue, finish):
    chunks = [slice(c, c + chunk) for c in range(0, width, chunk)]
    pending = issue(chunks[0])
    for idx, cols in enumerate(chunks):
        current = pending
        if idx + 1 < len(chunks):
            pending = issue(chunks[idx + 1])
        finish(cols, current)


def _glu_kernel(y_ref, w1_ref, w2_ref, gb_ref, o_ref):
    pp, tmt, _ = o_ref.shape
    yt = jnp.concatenate([y_ref[p].T for p in range(pp)], axis=0)

    def issue(cols):
        return (jnp.dot(yt, w1_ref[:, cols], preferred_element_type=F32),
                jnp.dot(yt, w2_ref[:, cols], preferred_element_type=F32))

    def finish(cols, gl):
        gb = gb_ref[:, :, cols].reshape(pp * tmt, -1)
        out = (gl[0] * jax.nn.sigmoid(gl[1]) * (gb * jax.nn.sigmoid(gb))).astype(o_ref.dtype)
        o_ref[:, :, cols] = out.reshape(pp, tmt, -1)

    _column_pipeline(o_ref.shape[2], GLU_CHUNK, issue, finish)


def _glu(y_t, w_glu, p2, tokens=512, tn=2048):
    b, _, _, ntau = y_t.shape
    tmt = min(tokens, ntau)
    pp = tokens // tmt
    nj = SSM_WIDTH // tn
    p2v = p2.reshape(b, SSM_BLOCK, ntau, P2_WIDTH)
    out = pl.pallas_call(
        _glu_kernel,
        grid=(nj, b, SSM_BLOCK // pp, ntau // tmt),
        in_specs=[
            pl.BlockSpec((None, pp, SSM_WIDTH, tmt), lambda j, bb, i, tt: (bb, i, 0, tt)),
            pl.BlockSpec((SSM_WIDTH, tn), lambda j, bb, i, tt: (0, j)),
            pl.BlockSpec((SSM_WIDTH, tn), lambda j, bb, i, tt: (0, nj + j)),
            pl.BlockSpec((None, pp, tmt, tn), lambda j, bb, i, tt: (bb, i, tt, j)),
        ],
        out_specs=pl.BlockSpec((None, pp, tmt, tn), lambda j, bb, i, tt: (bb, i, tt, j)),
        out_shape=jax.ShapeDtypeStruct((b, SSM_BLOCK, ntau, SSM_WIDTH), BF16),
        compiler_params=_params("parallel", "parallel", "parallel", "parallel"),
        name="glu",
    )(y_t, w_glu, w_glu, p2v)
    return out.reshape(b * SSM_BLOCK * ntau, SSM_WIDTH)


def _merge_kernel(a_ref, y_ref, wa_ref, wb_ref, ma_ref, mb_ref, unperm_ref, o_ref):
    rows = o_ref.shape[0]
    a = a_ref[...].reshape(rows, ATTN_WIDTH)
    y = y_ref[...].reshape(rows, SSM_WIDTH)
    sub = unperm_ref.shape[0]
    n = sub // SSM_BLOCK

    def issue(cols):
        return (jnp.dot(a, wa_ref[:, cols], preferred_element_type=F32),
                jnp.dot(y, wb_ref[:, cols], preferred_element_type=F32))

    def finish(cols, z):
        ma = ma_ref[:, :, cols].reshape(rows, -1)
        mb = mb_ref[:, :, cols].reshape(rows, -1)
        merged = (jax.nn.sigmoid(ma) * z[0] + jax.nn.sigmoid(mb) * z[1]).astype(BF16)
        merged = merged.reshape(SSM_BLOCK, rows // SSM_BLOCK, -1)
        for k in range(rows // sub):
            part = merged[:, k * n:(k + 1) * n, :].reshape(sub, -1)
            o_ref[k * sub:(k + 1) * sub, cols] = jnp.dot(unperm_ref[...], part,
                                                         preferred_element_type=F32).astype(o_ref.dtype)

    _column_pipeline(o_ref.shape[1], EPILOGUE_CHUNK, issue, finish)


def _step_permutation(tm):
    n = tm // SSM_BLOCK
    r = jnp.arange(tm)
    src = SSM_BLOCK * (r % n) + r // n
    return (src[:, None] == jnp.arange(tm)[None, :]).astype(BF16)


def _merge(a, y2, w_attn_proj, w_ssm_proj, p2, b, s, tm=512, tn=1024):
    ntau = s // SSM_BLOCK
    n = tm // SSM_BLOCK
    ma0 = (COL_MA - COL_GB) // tn
    mb0 = (COL_MB - COL_GB) // tn

    def steps(arr, width):
        return arr.reshape(b, SSM_BLOCK, ntau, width)

    return pl.pallas_call(
        _merge_kernel,
        grid=(D_MODEL // tn, b, s // tm),
        in_specs=[
            pl.BlockSpec((None, SSM_BLOCK, n, ATTN_WIDTH), lambda j, bb, t: (bb, 0, t, 0)),
            pl.BlockSpec((None, SSM_BLOCK, n, SSM_WIDTH), lambda j, bb, t: (bb, 0, t, 0)),
            pl.BlockSpec((ATTN_WIDTH, tn), lambda j, bb, t: (0, j)),
            pl.BlockSpec((SSM_WIDTH, tn), lambda j, bb, t: (0, j)),
            pl.BlockSpec((None, SSM_BLOCK, n, tn), lambda j, bb, t: (bb, 0, t, ma0 + j)),
            pl.BlockSpec((None, SSM_BLOCK, n, tn), lambda j, bb, t: (bb, 0, t, mb0 + j)),
            pl.BlockSpec((PERM_TILE, PERM_TILE), lambda j, bb, t: (0, 0)),
        ],
        out_specs=pl.BlockSpec((tm, tn), lambda j, bb, t: (bb * (s // tm) + t, j)),
        out_shape=jax.ShapeDtypeStruct((b * s, D_MODEL), BF16),
        compiler_params=_params("parallel", "parallel", "parallel"),
        name="merge",
    )(steps(a, ATTN_WIDTH), steps(y2, SSM_WIDTH), w_attn_proj, w_ssm_proj,
      steps(p2, P2_WIDTH), steps(p2, P2_WIDTH), _step_permutation(PERM_TILE).T)


def _out_kernel(m_ref, w_ref, x_ref, o_ref):
    o_ref[...] = x_ref[...] + jnp.dot(m_ref[...], w_ref[...], preferred_element_type=F32)


def _out_proj(merged, w_out, x2, tm=1024, tn=1024):
    t = merged.shape[0]
    return pl.pallas_call(
        _out_kernel,
        grid=(D_MODEL // tn, t // tm),
        in_specs=[
            pl.BlockSpec((tm, D_MODEL), lambda j, i: (i, 0)),
            pl.BlockSpec((D_MODEL, tn), lambda j, i: (0, j)),
            pl.BlockSpec((tm, tn), lambda j, i: (i, j)),
        ],
        out_specs=pl.BlockSpec((tm, tn), lambda j, i: (i, j)),
        out_shape=jax.ShapeDtypeStruct((t, D_MODEL), F32),
        compiler_params=_params("parallel", "parallel"),
        name="out_proj",
    )(merged, w_out, x2)


def _layer(x, w, late):
    b, s, d = x.shape
    h_steps = _prenorm(x, w["norm_gain"])
    h2 = h_steps.reshape(b * s, d)
    side = late if late[0].dtype != BF16 else ()
    p1, rounded = _proj(h2, w["w_in"], COL_Q, P1_WIDTH, w["qk_gain_row"], 2 * ATTN_WIDTH, "proj_attn", side)
    w_glu, w_attn_proj, w_ssm_proj, w_out = rounded if side else late
    p2, _ = _proj(h2, w["w_in"], COL_GB, P2_WIDTH, None, 0, "proj_gates")
    u_t = _proj_u(h_steps, w["wu_t"])
    a = _attention(p1, w["bias_tabs"], b, s).reshape(b * s, ATTN_WIDTH)
    y_t = _ssm(u_t, w["wz"], w["t0"], w["cc"], w["a16"], w["d_rows"])
    y2 = _glu(y_t, w_glu, p2)
    merged = _merge(a, y2, w_attn_proj, w_ssm_proj, p2, b, s)
    out = _out_proj(merged, w_out, x.reshape(b * s, d)).reshape(b, s, d)
    return out, (w_glu, w_attn_proj, w_ssm_proj, w_out)


def _prepare(rel_bias, norm_gain, w_in, q_gain, k_gain, lam_re, lam_im, log_dt, b_re, b_im, c_re, c_im, d_skip):
    wz, t0, cc, a16 = _ssm_weights(lam_re, lam_im, log_dt, b_re, b_im, c_re, c_im)
    gains = jnp.concatenate([jnp.tile(q_gain.astype(F32) * (HEAD_DIM ** -0.5 * LOG2E), N_HEADS),
                             jnp.tile(k_gain.astype(F32), N_HEADS),
                             jnp.zeros((P1_WIDTH - 2 * ATTN_WIDTH,), F32)])
    d_rows = jnp.tile(d_skip.astype(F32).reshape(SSM_GROUPS, 1, SSM_GROUP), (1, SSM_BLOCK, 1))
    return {
        "norm_gain": norm_gain.astype(F32),
        "w_in": w_in.astype(BF16),
        "wu_t": w_in[:, COL_U:COL_U + SSM_WIDTH].T.astype(BF16),
        "qk_gain_row": gains.reshape(1, P1_WIDTH),
        "bias_tabs": _bias_tables(rel_bias),
        "wz": wz, "t0": t0, "cc": cc, "a16": a16,
        "d_rows": d_rows.reshape(SSM_GROUPS, SSM_ROWS, 1),
    }


def kernel(x_prompt, x_sample, rel_bias, norm_gain, w_in, q_gain, k_gain, lam_re, lam_im, log_dt, b_re, b_im,
           c_re, c_im, d_skip, w_glu, w_attn_proj, w_ssm_proj, w_out):
    y_prompt, y_sample = x_prompt, x_sample
    for l in range(norm_gain.shape[0]):
        w = _prepare(rel_bias, norm_gain[l], w_in[l], q_gain[l], k_gain[l], lam_re[l], lam_im[l], log_dt[l],
                     b_re[l], b_im[l], c_re[l], c_im[l], d_skip[l])
        late = tuple(m[l].astype(F32) for m in (w_glu, w_attn_proj, w_ssm_proj, w_out))
        y_prompt, late = _layer(y_prompt, w, late)
        y_sample, _ = _layer(y_sample, w, late)
    return (y_prompt, y_sample)
```

```python
import functools
import math

import jax
import jax.numpy as jnp
from jax import lax
from jax.experimental import pallas as pl
from jax.experimental.pallas import tpu as pltpu

F32 = jnp.float32
BF16 = jnp.bfloat16

D_MODEL = 4096
N_HEADS = 16
HEAD_DIM = 128
ATTN_WIDTH = N_HEADS * HEAD_DIM
SIDE_KEYS = 64
N_BUCKETS = 32
REL_MAX_DIST = 1024
NEG = -1e30
LOG2E = math.log2(math.e)
EPS = 1e-6
SSM_WIDTH = 2048
SSM_GROUP = 16
SSM_GROUPS = SSM_WIDTH // SSM_GROUP
SSM_STATE = 64
SSM_BLOCK = 16
SSM_ROWS = SSM_BLOCK * SSM_GROUP

COL_Q = 0
COL_K = ATTN_WIDTH
COL_V = 2 * ATTN_WIDTH
COL_GA = 3 * ATTN_WIDTH
COL_U = 4 * ATTN_WIDTH
COL_GB = COL_U + SSM_WIDTH
COL_MA = COL_GB + SSM_WIDTH
COL_MB = COL_MA + D_MODEL
P1_WIDTH = COL_U
P2_WIDTH = SSM_WIDTH + 2 * D_MODEL

LANES = 128
PLANES = SSM_BLOCK
PERM_TILE = 256
EPILOGUE_CHUNK = 512
GLU_CHUNK = 256
ATTN_GEOM = ((1, 8, 24), (4, 32, 64), (16, 128, 256))
ATTN_BATCH = (4, 4, 4)
ATTN_ROWS = 256
HALO_ROWS = SIDE_KEYS

VMEM_BYTES_V7X = 64 * 1024 * 1024
VMEM_LIMIT = VMEM_BYTES_V7X - 4 * 1024 * 1024


def _params(*sem):
    return pltpu.CompilerParams(dimension_semantics=sem, vmem_limit_bytes=VMEM_LIMIT)


def _prenorm_kernel(x_ref, g_ref, perm_ref, o_ref):
    x = x_ref[...]
    ms = jnp.mean(x * x, axis=-1, keepdims=True)
    h = (x * lax.rsqrt(ms + EPS) * g_ref[...]).astype(BF16)
    hp = jnp.dot(perm_ref[...], h, preferred_element_type=F32).astype(BF16)
    o_ref[...] = hp.reshape(o_ref.shape)


def _prenorm(x, gain, tm=PERM_TILE):
    b, s, d = x.shape
    return pl.pallas_call(
        _prenorm_kernel,
        grid=(b, s // tm),
        in_specs=[pl.BlockSpec((None, tm, d), lambda bb, t: (bb, t, 0)),
                  pl.BlockSpec((1, d), lambda bb, t: (0, 0)),
                  pl.BlockSpec((tm, tm), lambda bb, t: (0, 0))],
        out_specs=pl.BlockSpec((None, SSM_BLOCK, tm // SSM_BLOCK, d), lambda bb, t: (bb, 0, t, 0)),
        out_shape=jax.ShapeDtypeStruct((b, SSM_BLOCK, s // SSM_BLOCK, d), BF16),
        compiler_params=_params("parallel", "parallel"),
        name="prenorm",
    )(x, gain.reshape(1, d), _step_permutation(tm))


def _proj_kernel(h_ref, w_ref, g_ref, *refs, n_norm_tiles, n_side):
    side_in, o_ref, side_out = refs[:n_side], refs[n_side], refs[n_side + 1:]
    j = pl.program_id(0)

    for src, dst in zip(side_in, side_out):
        dst[...] = src[...].astype(BF16)

    @pl.when(j < n_norm_tiles)
    def _():
        def issue(cols):
            return jnp.dot(h_ref[...], w_ref[:, cols], preferred_element_type=F32)

        def finish(cols, acc):
            for hh in range(acc.shape[1] // HEAD_DIM):
                blk = acc[:, hh * HEAD_DIM:(hh + 1) * HEAD_DIM]
                sl = slice(cols.start + hh * HEAD_DIM, cols.start + (hh + 1) * HEAD_DIM)
                ms = jnp.mean(blk * blk, axis=-1, keepdims=True)
                o_ref[:, sl] = blk * lax.rsqrt(ms + EPS) * g_ref[:, sl]

        _column_pipeline(o_ref.shape[1], EPILOGUE_CHUNK, issue, finish)

    @pl.when(j >= n_norm_tiles)
    def _():
        o_ref[...] = jnp.dot(h_ref[...], w_ref[...], preferred_element_type=F32)


def _proj(h, w, col0, width, gain_row, n_norm_cols, name, side=(), tm=512, tn=2048):
    t, k = h.shape
    if gain_row is None:
        gain_row = jnp.zeros((1, width), F32)
    nm = t // tm
    nsteps = (width // tn) * nm

    def side_spec(mat):
        return pl.BlockSpec((mat.shape[0] // nsteps, mat.shape[1]), lambda j, i: (j * nm + i, 0))

    out = pl.pallas_call(
        functools.partial(_proj_kernel, n_norm_tiles=n_norm_cols // tn, n_side=len(side)),
        grid=(width // tn, nm),
        in_specs=[
            pl.BlockSpec((tm, k), lambda j, i: (i, 0)),
            pl.BlockSpec((k, tn), lambda j, i: (0, col0 // tn + j)),
            pl.BlockSpec((1, tn), lambda j, i: (0, j)),
        ] + [side_spec(mat) for mat in side],
        out_specs=[pl.BlockSpec((tm, tn), lambda j, i: (i, j))] + [side_spec(mat) for mat in side],
        out_shape=[jax.ShapeDtypeStruct((t, width), F32)]
        + [jax.ShapeDtypeStruct(mat.shape, BF16) for mat in side],
        compiler_params=_params("parallel", "parallel"),
        name=name,
    )(h, w, gain_row, *side)
    return out[0], tuple(out[1:])


def _proj_u_kernel(w_ref, h_ref, o_ref):
    o_ref[...] = lax.dot_general(w_ref[...], h_ref[...], (((1,), (1,)), ((), ())),
                                 preferred_element_type=F32)


def _proj_u(h_steps, wu_t, tokens=512):
    b, _, ntau, _ = h_steps.shape
    tmt = min(tokens, ntau)
    return pl.pallas_call(
        _proj_u_kernel,
        grid=(b, SSM_BLOCK, ntau // tmt),
        in_specs=[
            pl.BlockSpec((SSM_WIDTH, D_MODEL), lambda bb, i, tt: (0, 0)),
            pl.BlockSpec((None, None, tmt, D_MODEL), lambda bb, i, tt: (bb, i, tt, 0)),
        ],
        out_specs=pl.BlockSpec((None, None, SSM_WIDTH, tmt), lambda bb, i, tt: (bb, i, 0, tt)),
        out_shape=jax.ShapeDtypeStruct((b, SSM_BLOCK, SSM_WIDTH, ntau), F32),
        compiler_params=_params("parallel", "parallel", "parallel"),
        name="proj_u",
    )(wu_t, h_steps)


def _rel_bucket(rel):
    half = N_BUCKETS // 2
    exact = half // 2
    base = jnp.where(rel > 0, half, 0)
    n = jnp.abs(rel)
    nf = jnp.maximum(n, 1).astype(F32)
    large = exact + (jnp.log(nf / exact) / math.log(REL_MAX_DIST / exact) * (half - exact)).astype(jnp.int32)
    large = jnp.minimum(large, half - 1)
    return base + jnp.where(n < exact, n, large)


def _bias_tables(rel_bias):
    buckets = jnp.arange(N_BUCKETS)
    tabs = []
    for d, nq, nk in ATTN_GEOM:
        g = PLANES // d
        koff = (nk - nq) // 2
        aq = jnp.arange(g)[:, None, None, None]
        tq = jnp.arange(nq)[None, :, None, None]
        ak = jnp.arange(g)[None, None, :, None]
        tk = jnp.arange(nk)[None, None, None, :]
        rel = (g * ((tk - koff) - tq) + (ak - aq)).reshape(g * nq, g * nk)
        onehot = (_rel_bucket(rel * d)[..., None] == buckets).astype(F32)
        bias = jnp.einsum('qkb,bh->hqk', onehot, rel_bias.astype(F32), precision=lax.Precision.HIGHEST)
        bias = jnp.where((jnp.abs(rel) <= SIDE_KEYS)[None], bias, NEG)
        trow = jnp.arange(g * nk) % nk - koff
        before = (trow < 0)[None, None, :]
        after = (trow >= nq)[None, None, :]
        bias = bias * LOG2E
        tabs.append(jnp.stack([bias, jnp.where(before, NEG, bias), jnp.where(after, NEG, bias)], axis=1))
    return tabs


def _attn_kernel(q_ref, kp_ref, kc_ref, kn_ref, vp_ref, vc_ref, vn_ref, ga_ref, t1_ref, t4_ref, t16_ref, o_ref,
                 obuf, lbuf):
    t = pl.program_id(2)
    first = t == 0
    last = t == pl.num_programs(2) - 1
    k_refs = (kp_ref, kc_ref, kn_ref)
    v_refs = (vp_ref, vc_ref, vn_ref)
    ones = jnp.ones((HEAD_DIM,), BF16)

    def halo_rows(refs, plane, start, n):
        pieces = []
        r0 = -HALO_ROWS
        for ref in refs:
            r1 = r0 + ref.shape[1]
            lo, hi = max(start, r0), min(start + n, r1)
            if lo < hi:
                pieces.append(ref[plane, lo - r0:hi - r0, :])
            r0 = r1
        return pieces

    edge_bias = {}

    def logits(pi, grp):
        d, nq, nk = ATTN_GEOM[pi]
        tab_ref = (t1_ref, t4_ref, t16_ref)[pi]
        g = PLANES // d
        nblk = ATTN_ROWS // nq
        k0 = -((nk - nq) // 2)

        def rows_of(refs, r, start, n):
            pieces = [piece for a in range(g) for piece in halo_rows(refs, r + d * a, start, n)]
            return jnp.concatenate(pieces, axis=0).astype(BF16)

        def bias_of(blk):
            if (pi, blk) not in edge_bias:
                bias = tab_ref[0]
                if blk == 0:
                    bias = jnp.where(first, tab_ref[1], bias)
                if blk == nblk - 1:
                    bias = jnp.minimum(bias, jnp.where(last, tab_ref[2], tab_ref[0]))
                edge_bias[(pi, blk)] = bias
            return edge_bias[(pi, blk)]

        q = jnp.stack([jnp.concatenate([q_ref[r + d * a, blk * nq:(blk + 1) * nq, :] for a in range(g)],
                                       axis=0).astype(BF16) for r, blk in grp])
        k = jnp.stack([rows_of(k_refs, r, blk * nq + k0, nk) for r, blk in grp])
        v = jnp.stack([rows_of(v_refs, r, blk * nq + k0, nk) for r, blk in grp])
        edges = {blk if blk in (0, nblk - 1) else None for _, blk in grp}
        if len(edges) == 1:
            blk = edges.pop()
            bias = (tab_ref[0] if blk is None else bias_of(blk))[None]
        else:
            bias = jnp.stack([bias_of(blk) if blk in (0, nblk - 1) else tab_ref[0] for _, blk in grp])
        return jnp.einsum('bqd,bkd->bqk', q, k, preferred_element_type=F32) + bias, v

    def finish(pi, grp, s, v):
        d, nq, _ = ATTN_GEOM[pi]
        g = PLANES // d
        m = jnp.max(s, axis=-1, keepdims=True)
        p = jnp.exp2(s - m).astype(BF16)
        v1 = jnp.concatenate([v, jnp.broadcast_to(ones, v.shape)], axis=-1)
        acc = jnp.einsum('bqk,bkd->bqd', p, v1, preferred_element_type=F32)
        den = acc[..., HEAD_DIM:]
        o = acc[..., :HEAD_DIM] / den
        lse = m + jnp.log2(den)
        for bi, (r, blk) in enumerate(grp):
            for a in range(g):
                rows = slice(blk * nq, (blk + 1) * nq)
                obuf[pi, r + d * a, rows, :] = o[bi, a * nq:(a + 1) * nq]
                lbuf[pi, r + d * a, rows, :] = lse[bi, a * nq:(a + 1) * nq]

    work = []
    for pi, (d, nq, _) in enumerate(ATTN_GEOM):
        blocks = [(r, blk) for r in range(d) for blk in range(ATTN_ROWS // nq)]
        work += [(pi, blocks[g0:g0 + ATTN_BATCH[pi]]) for g0 in range(0, len(blocks), ATTN_BATCH[pi])]
    pending = logits(*work[0])
    for idx, (pi, grp) in enumerate(work):
        s, v = pending
        if idx + 1 < len(work):
            pending = logits(*work[idx + 1])
        finish(pi, grp, s, v)

    for i in range(PLANES):
        l0, l1, l2 = lbuf[0, i], lbuf[1, i], lbuf[2, i]
        mx = jnp.maximum(jnp.maximum(l0, l1), l2)
        e0, e1, e2 = jnp.exp2(l0 - mx), jnp.exp2(l1 - mx), jnp.exp2(l2 - mx)
        o = (e0 * obuf[0, i] + e1 * obuf[1, i] + e2 * obuf[2, i]) / (e0 + e1 + e2)
        ga = ga_ref[i]
        o_ref[i] = (o * (ga * jax.nn.sigmoid(ga))).astype(o_ref.dtype)


def _attention(p1, bias_tabs, b, s):
    nrow = s // PLANES
    p1 = p1.reshape(b, PLANES, nrow, P1_WIDTH)
    per = ATTN_ROWS // HALO_ROWS
    n_halo = nrow // HALO_ROWS

    def tile(col):
        return pl.BlockSpec((None, PLANES, ATTN_ROWS, HEAD_DIM), lambda bb, h, t: (bb, 0, t, col // HEAD_DIM + h))

    def prev(col):
        return pl.BlockSpec((None, PLANES, HALO_ROWS, HEAD_DIM),
                            lambda bb, h, t: (bb, 0, jnp.maximum(t * per - 1, 0), col // HEAD_DIM + h))

    def nxt(col):
        return pl.BlockSpec((None, PLANES, HALO_ROWS, HEAD_DIM),
                            lambda bb, h, t: (bb, 0, jnp.minimum((t + 1) * per, n_halo - 1), col // HEAD_DIM + h))

    def table(tab):
        return pl.BlockSpec((None,) + tab.shape[1:], lambda bb, h, t: (h, 0, 0, 0))

    part_buf = pltpu.VMEM((len(ATTN_GEOM), PLANES, ATTN_ROWS, HEAD_DIM), F32)
    return pl.pallas_call(
        _attn_kernel,
        grid=(b, N_HEADS, nrow // ATTN_ROWS),
        in_specs=[tile(COL_Q), prev(COL_K), tile(COL_K), nxt(COL_K), prev(COL_V), tile(COL_V), nxt(COL_V),
                  tile(COL_GA)] + [table(tab) for tab in bias_tabs],
        out_specs=pl.BlockSpec((None, PLANES, ATTN_ROWS, HEAD_DIM), lambda bb, h, t: (bb, 0, t, h)),
        out_shape=jax.ShapeDtypeStruct((b, PLANES, nrow, ATTN_WIDTH), BF16),
        scratch_shapes=[part_buf, part_buf],
        compiler_params=_params("parallel", "parallel", "parallel"),
        name="attention",
    )(p1, p1, p1, p1, p1, p1, p1, p1, *bias_tabs)


def _ssm_prep_kernel(arow_ref, acol_ref, c_ref, bw_ref, side_ref, wz_ref, t0_ref, cc_ref, side_out_ref, *, gc):
    nb, hc, p = SSM_BLOCK, SSM_GROUP, SSM_STATE
    side_out_ref[...] = side_ref[...].astype(BF16)
    blk_of_lane = lax.broadcasted_iota(jnp.int32, (hc, SSM_ROWS), 1) // hc
    blk_of_lane_p = lax.broadcasted_iota(jnp.int32, (p, SSM_ROWS), 1) // hc

    def body(gi, carry):
        lag_kernels = []
        for dirn in range(2):
            ar = arow_ref[gi, 2 * dirn:2 * dirn + 1, :]
            ai = arow_ref[gi, 2 * dirn + 1:2 * dirn + 2, :]
            cr = c_ref[gi, 2 * dirn]
            ci = c_ref[gi, 2 * dirn + 1]
            pr = jnp.ones((1, p), F32)
            pi = jnp.zeros((1, p), F32)
            ca_re, ca_im = [], []
            for _ in range(nb + 1):
                ca_re.append(cr * pr - ci * pi)
                ca_im.append(cr * pi + ci * pr)
                pr, pi = pr * ar - pi * ai, pr * ai + pi * ar
            bre = bw_ref[gi, 2 * dirn]
            bim = bw_ref[gi, 2 * dirn + 1]
            lag_kernels.append(
                jnp.dot(jnp.concatenate(ca_re[:nb], axis=0).astype(BF16), bre.astype(BF16),
                        preferred_element_type=F32)
                - jnp.dot(jnp.concatenate(ca_im[:nb], axis=0).astype(BF16), bim.astype(BF16),
                          preferred_element_type=F32))
            order = range(1, nb + 1) if dirn == 0 else range(nb, 0, -1)
            c0 = 2 * p * dirn
            cc_ref[gi, :, c0:c0 + p] = jnp.concatenate([ca_re[k] for k in order], axis=0).astype(BF16)
            cc_ref[gi, :, c0 + p:c0 + 2 * p] = (-jnp.concatenate([ca_im[k] for k in order], axis=0)).astype(BF16)
            acr = acol_ref[gi, c0:c0 + p, :]
            aci = acol_ref[gi, c0 + p:c0 + 2 * p, :]
            qr = jnp.ones((p, 1), F32)
            qi = jnp.zeros((p, 1), F32)
            powers = []
            for _ in range(nb):
                powers.append((qr, qi))
                qr, qi = qr * acr - qi * aci, qr * aci + qi * acr
            per = jnp.zeros((p, SSM_ROWS), F32)
            pei = jnp.zeros((p, SSM_ROWS), F32)
            for i in range(nb):
                e = nb - 1 - i if dirn == 0 else i
                per = jnp.where(blk_of_lane_p == i, powers[e][0], per)
                pei = jnp.where(blk_of_lane_p == i, powers[e][1], pei)
            wz_ref[gi, c0:c0 + p, :] = (per * bre - pei * bim).astype(BF16)
            wz_ref[gi, c0 + p:c0 + 2 * p, :] = (per * bim + pei * bre).astype(BF16)
        kf, kb = lag_kernels
        diag = kf[0:hc, :] + kb[0:hc, :]
        for j in range(nb):
            acc = jnp.zeros((hc, SSM_ROWS), F32)
            for i in range(nb):
                if i < j:
                    src = kf[hc * (j - i):hc * (j - i + 1), :]
                elif i > j:
                    src = kb[hc * (i - j):hc * (i - j + 1), :]
                else:
                    src = diag
                acc = jnp.where(blk_of_lane == i, src, acc)
            t0_ref[gi, hc * j:hc * (j + 1), :] = acc.astype(BF16)
        return carry

    lax.fori_loop(0, gc, body, 0)


def _ssm_weights(lam_re, lam_im, log_dt, b_re, b_im, c_re, c_im, side, gc=2):
    g, p, hc = SSM_GROUPS, SSM_STATE, SSM_GROUP
    dt = jnp.exp(log_dt.astype(F32))[..., None]
    lr = lam_re.astype(F32)
    li = lam_im.astype(F32)
    mag = jnp.exp(lr * dt)
    ab_re = mag * jnp.cos(li * dt)
    ab_im = mag * jnp.sin(li * dt)
    den = lr * lr + li * li
    f_re = ((ab_re - 1.0) * lr + ab_im * li) / den
    f_im = (ab_im * lr - (ab_re - 1.0) * li) / den
    br = b_re.astype(F32)
    bi = b_im.astype(F32)
    bb_re = f_re[..., None] * br - f_im[..., None] * bi
    bb_im = f_re[..., None] * bi + f_im[..., None] * br
    mag16 = jnp.exp(SSM_BLOCK * (lr * dt))
    a16_re = mag16 * jnp.cos(SSM_BLOCK * (li * dt))
    a16_im = mag16 * jnp.sin(SSM_BLOCK * (li * dt))

    def pack(re, im):
        return jnp.stack([re[0], im[0], re[1], im[1]], axis=1)

    arow = pack(ab_re, ab_im)
    acol = arow.reshape(g, 4 * p, 1)
    a16 = pack(a16_re, a16_im).reshape(g, 4 * p, 1)
    cmat = pack(c_re.astype(F32), c_im.astype(F32))
    bw = jnp.tile(pack(bb_re, bb_im), (1, 1, 1, SSM_BLOCK))
    wspec = pl.BlockSpec((gc, SSM_ROWS, SSM_ROWS), lambda i: (i, 0, 0))
    wshape = jax.ShapeDtypeStruct((g, SSM_ROWS, SSM_ROWS), BF16)
    nsteps = g // gc
    side_spec = pl.BlockSpec((side.shape[0] // nsteps, side.shape[1]), lambda i: (i, 0))
    wz, t0, cc, side_bf16 = pl.pallas_call(
        functools.partial(_ssm_prep_kernel, gc=gc),
        grid=(nsteps,),
        in_specs=[pl.BlockSpec((gc, 4, p), lambda i: (i, 0, 0)),
                  pl.BlockSpec((gc, 4 * p, 1), lambda i: (i, 0, 0)),
                  pl.BlockSpec((gc, 4, hc, p), lambda i: (i, 0, 0, 0)),
                  pl.BlockSpec((gc, 4, p, SSM_ROWS), lambda i: (i, 0, 0, 0)),
                  side_spec],
        out_specs=[wspec, wspec, wspec, side_spec],
        out_shape=[wshape, wshape, wshape, jax.ShapeDtypeStruct(side.shape, BF16)],
        compiler_params=_params("parallel"),
        name="ssm_prep",
    )(arow, acol, cmat, bw, side)
    return wz, t0, cc, a16, side_bf16


def _gelu_tanh(x):
    return 0.5 * x * (1.0 + jnp.tanh(math.sqrt(2.0 / math.pi) * (x + 0.044715 * (x * x * x))))


def _cmul(ar, ai, br, bi):
    return ar * br - ai * bi, ar * bi + ai * br


def _block_states(zs, coefs, nc):
    p = SSM_STATE
    ng = len(zs)
    rows = ng * p
    lane = lax.broadcasted_iota(jnp.int32, (rows, LANES), 1)
    zero = jnp.zeros((rows, LANES), F32)

    def part(arrs, r0, c=None):
        cols = slice(None) if c is None else slice(LANES * c, LANES * (c + 1))
        return jnp.concatenate([x[r0:r0 + p, cols] for x in arrs], axis=0)

    halves = []
    for dirn in range(2):
        base = 2 * p * dirn
        zr = [part(zs, base, c) for c in range(nc)]
        zi = [part(zs, base + p, c) for c in range(nc)]
        ar = jnp.broadcast_to(part(coefs, base), (rows, LANES))
        ai = jnp.broadcast_to(part(coefs, base + p), (rows, LANES))
        order = list(range(nc)) if dirn == 0 else list(range(nc - 1, -1, -1))
        ir, ii = {order[0]: zr[order[0]]}, {order[0]: zi[order[0]]}
        for prev, c in zip(order[:-1], order[1:]):
            tr, ti = _cmul(ar, ai, ir[prev], ii[prev])
            ir[c], ii[c] = zr[c] + tr, zi[c] + ti
        pows = [(ar, ai)]
        for _ in range(nc - 1):
            pows.append(_cmul(pows[-1][0], pows[-1][1], ar, ai))
        gr, gi = ir[order[-1]], ii[order[-1]]
        br, bi = pows[nc - 1]
        s = 1
        while s < LANES:
            keep = (lane >= s) if dirn == 0 else (lane < LANES - s)
            amt = s if dirn == 0 else LANES - s
            rr = jnp.where(keep, pltpu.roll(gr, amt, 1), zero)
            ri = jnp.where(keep, pltpu.roll(gi, amt, 1), zero)
            tr, ti = _cmul(br, bi, rr, ri)
            gr, gi = gr + tr, gi + ti
            br, bi = _cmul(br, bi, br, bi)
            s *= 2
        keep = (lane >= 1) if dirn == 0 else (lane < LANES - 1)
        amt = 1 if dirn == 0 else LANES - 1
        er = jnp.where(keep, pltpu.roll(gr, amt, 1), zero)
        ei = jnp.where(keep, pltpu.roll(gi, amt, 1), zero)
        hr, hi = [None] * nc, [None] * nc
        hr[order[0]], hi[order[0]] = er, ei
        for k, (prev, c) in enumerate(zip(order[:-1], order[1:])):
            tr, ti = _cmul(pows[k][0], pows[k][1], er, ei)
            hr[c], hi[c] = ir[prev] + tr, ii[prev] + ti
        halves += [jnp.concatenate(hr, axis=1), jnp.concatenate(hi, axis=1)]
    return [jnp.concatenate([h[g * p:(g + 1) * p] for h in halves], axis=0) for g in range(ng)]


def _ssm_kernel(u_ref, wz_ref, t0_ref, cc_ref, a_ref, d_ref, perm_ref, unperm_ref, y_ref, *, gc, ntau):
    nc = ntau // LANES
    rows = [slice(SSM_GROUP * gi, SSM_GROUP * (gi + 1)) for gi in range(gc)]
    zs = []
    for gi in range(gc):
        x = u_ref[:, rows[gi], :].reshape(SSM_ROWS, ntau).astype(BF16)
        xp = jnp.dot(x, perm_ref[...], preferred_element_type=F32).astype(BF16)
        zs.append(jnp.dot(wz_ref[gi], xp, preferred_element_type=F32))
    states = _block_states(zs, [a_ref[gi] for gi in range(gc)], nc)
    for gi in range(gc):
        xf = u_ref[:, rows[gi], :].reshape(SSM_ROWS, ntau)
        h = jnp.dot(states[gi].astype(BF16), unperm_ref[...], preferred_element_type=F32).astype(BF16)
        y = (jnp.dot(t0_ref[gi], xf.astype(BF16), preferred_element_type=F32)
             + jnp.dot(cc_ref[gi], h, preferred_element_type=F32))
        y = _gelu_tanh(y + d_ref[gi] * xf)
        y_ref[:, rows[gi], :] = y.astype(y_ref.dtype).reshape(SSM_BLOCK, SSM_GROUP, ntau)


def _ssm(u_t, wz, t0, cc, a16, d_rows, gc=8):
    b, _, _, ntau = u_t.shape
    nc = ntau // LANES
    j = jnp.arange(ntau)
    src = (j % LANES) * nc + j // LANES
    perm = (jnp.arange(ntau)[:, None] == src[None, :]).astype(BF16)
    wspec = pl.BlockSpec((gc, SSM_ROWS, SSM_ROWS), lambda bb, g: (g, 0, 0))
    vspec = pl.BlockSpec((gc, SSM_ROWS, 1), lambda bb, g: (g, 0, 0))
    uspec = pl.BlockSpec((None, SSM_BLOCK, SSM_GROUP * gc, ntau), lambda bb, g: (bb, 0, g, 0))
    pspec = pl.BlockSpec((ntau, ntau), lambda bb, g: (0, 0))
    return pl.pallas_call(
        functools.partial(_ssm_kernel, gc=gc, ntau=ntau),
        grid=(b, SSM_GROUPS // gc),
        in_specs=[uspec, wspec, wspec, wspec, vspec, vspec, pspec, pspec],
        out_specs=uspec,
        out_shape=jax.ShapeDtypeStruct(u_t.shape, BF16),
        compiler_params=_params("parallel", "parallel"),
        name="ssm",
    )(u_t, wz, t0, cc, a16, d_rows, perm, perm.T)


def _column_pipeline(width, chunk, issue, finish):
    chunks = [slice(c, c + chunk) for c in range(0, width, chunk)]
    pending = issue(chunks[0])
    for idx, cols in enumerate(chunks):
        current = pending
        if idx + 1 < len(chunks):
            pending = issue(chunks[idx + 1])
        finish(cols, current)


def _glu_kernel(y_ref, w1_ref, w2_ref, gb_ref, o_ref):
    pp, tmt, _ = o_ref.shape
    yt = jnp.concatenate([y_ref[p].T for p in range(pp)], axis=0)

    def issue(cols):
        return (jnp.dot(yt, w1_ref[:, cols], preferred_element_type=F32),
                jnp.dot(yt, w2_ref[:, cols], preferred_element_type=F32))

    def finish(cols, gl):
        gb = gb_ref[:, :, cols].reshape(pp * tmt, -1)
        out = (gl[0] * jax.nn.sigmoid(gl[1]) * (gb * jax.nn.sigmoid(gb))).astype(o_ref.dtype)
        o_ref[:, :, cols] = out.reshape(pp, tmt, -1)

    _column_pipeline(o_ref.shape[2], GLU_CHUNK, issue, finish)


def _glu(y_t, w_glu, p2, tokens=512, tn=2048):
    b, _, _, ntau = y_t.shape
    tmt = min(tokens, ntau)
    pp = tokens // tmt
    nj = SSM_WIDTH // tn
    p2v = p2.reshape(b, SSM_BLOCK, ntau, P2_WIDTH)
    out = pl.pallas_call(
        _glu_kernel,
        grid=(nj, b, SSM_BLOCK // pp, ntau // tmt),
        in_specs=[
            pl.BlockSpec((None, pp, SSM_WIDTH, tmt), lambda j, bb, i, tt: (bb, i, 0, tt)),
            pl.BlockSpec((SSM_WIDTH, tn), lambda j, bb, i, tt: (0, j)),
            pl.BlockSpec((SSM_WIDTH, tn), lambda j, bb, i, tt: (0, nj + j)),
            pl.BlockSpec((None, pp, tmt, tn), lambda j, bb, i, tt: (bb, i, tt, j)),
        ],
        out_specs=pl.BlockSpec((None, pp, tmt, tn), lambda j, bb, i, tt: (bb, i, tt, j)),
        out_shape=jax.ShapeDtypeStruct((b, SSM_BLOCK, ntau, SSM_WIDTH), BF16),
        compiler_params=_params("parallel", "parallel", "parallel", "parallel"),
        name="glu",
    )(y_t, w_glu, w_glu, p2v)
    return out.reshape(b * SSM_BLOCK * ntau, SSM_WIDTH)


def _merge_kernel(a_ref, y_ref, wa_ref, wb_ref, ma_ref, mb_ref, unperm_ref, o_ref):
    rows = o_ref.shape[0]
    a = a_ref[...].reshape(rows, ATTN_WIDTH)
    y = y_ref[...].reshape(rows, SSM_WIDTH)
    sub = unperm_ref.shape[0]
    n = sub // SSM_BLOCK

    def issue(cols):
        return (jnp.dot(a, wa_ref[:, cols], preferred_element_type=F32),
                jnp.dot(y, wb_ref[:, cols], preferred_element_type=F32))

    def finish(cols, z):
        ma = ma_ref[:, :, cols].reshape(rows, -1)
        mb = mb_ref[:, :, cols].reshape(rows, -1)
        merged = (jax.nn.sigmoid(ma) * z[0] + jax.nn.sigmoid(mb) * z[1]).astype(BF16)
        merged = merged.reshape(SSM_BLOCK, rows // SSM_BLOCK, -1)
        for k in range(rows // sub):
            part = merged[:, k * n:(k + 1) * n, :].reshape(sub, -1)
            o_ref[k * sub:(k + 1) * sub, cols] = jnp.dot(unperm_ref[...], part,
                                                         preferred_element_type=F32).astype(o_ref.dtype)

    _column_pipeline(o_ref.shape[1], EPILOGUE_CHUNK, issue, finish)


def _step_permutation(tm):
    n = tm // SSM_BLOCK
    r = jnp.arange(tm)
    src = SSM_BLOCK * (r % n) + r // n
    return (src[:, None] == jnp.arange(tm)[None, :]).astype(BF16)


def _merge(a, y2, w_attn_proj, w_ssm_proj, p2, b, s, tm=512, tn=1024):
    ntau = s // SSM_BLOCK
    n = tm // SSM_BLOCK
    ma0 = (COL_MA - COL_GB) // tn
    mb0 = (COL_MB - COL_GB) // tn

    def steps(arr, width):
        return arr.reshape(b, SSM_BLOCK, ntau, width)

    return pl.pallas_call(
        _merge_kernel,
        grid=(D_MODEL // tn, b, s // tm),
        in_specs=[
            pl.BlockSpec((None, SSM_BLOCK, n, ATTN_WIDTH), lambda j, bb, t: (bb, 0, t, 0)),
            pl.BlockSpec((None, SSM_BLOCK, n, SSM_WIDTH), lambda j, bb, t: (bb, 0, t, 0)),
            pl.BlockSpec((ATTN_WIDTH, tn), lambda j, bb, t: (0, j)),
            pl.BlockSpec((SSM_WIDTH, tn), lambda j, bb, t: (0, j)),
            pl.BlockSpec((None, SSM_BLOCK, n, tn), lambda j, bb, t: (bb, 0, t, ma0 + j)),
            pl.BlockSpec((None, SSM_BLOCK, n, tn), lambda j, bb, t: (bb, 0, t, mb0 + j)),
            pl.BlockSpec((PERM_TILE, PERM_TILE), lambda j, bb, t: (0, 0)),
        ],
        out_specs=pl.BlockSpec((tm, tn), lambda j, bb, t: (bb * (s // tm) + t, j)),
        out_shape=jax.ShapeDtypeStruct((b * s, D_MODEL), BF16),
        compiler_params=_params("parallel", "parallel", "parallel"),
        name="merge",
    )(steps(a, ATTN_WIDTH), steps(y2, SSM_WIDTH), w_attn_proj, w_ssm_proj,
      steps(p2, P2_WIDTH), steps(p2, P2_WIDTH), _step_permutation(PERM_TILE).T)


def _out_kernel(m_ref, w_ref, x_ref, o_ref):
    o_ref[...] = x_ref[...] + jnp.dot(m_ref[...], w_ref[...], preferred_element_type=F32)


def _out_proj(merged, w_out, x2, tm=1024, tn=1024):
    t = merged.shape[0]
    return pl.pallas_call(
        _out_kernel,
        grid=(D_MODEL // tn, t // tm),
        in_specs=[
            pl.BlockSpec((tm, D_MODEL), lambda j, i: (i, 0)),
            pl.BlockSpec((D_MODEL, tn), lambda j, i: (0, j)),
            pl.BlockSpec((tm, tn), lambda j, i: (i, j)),
        ],
        out_specs=pl.BlockSpec((tm, tn), lambda j, i: (i, j)),
        out_shape=jax.ShapeDtypeStruct((t, D_MODEL), F32),
        compiler_params=_params("parallel", "parallel"),
        name="out_proj",
    )(merged, w_out, x2)


def _layer(x, w, late):
    b, s, d = x.shape
    h_steps = _prenorm(x, w["norm_gain"])
    h2 = h_steps.reshape(b * s, d)
    side = late if late[0].dtype != BF16 else ()
    p1, rounded = _proj(h2, w["w_in"], COL_Q, P1_WIDTH, w["qk_gain_row"], 2 * ATTN_WIDTH, "proj_attn", side)
    w_glu, w_attn_proj, w_ssm_proj, w_out = rounded if side else late
    p2, _ = _proj(h2, w["w_in"], COL_GB, P2_WIDTH, None, 0, "proj_gates")
    u_t = _proj_u(h_steps, w["wu_t"])
    a = _attention(p1, w["bias_tabs"], b, s).reshape(b * s, ATTN_WIDTH)
    y_t = _ssm(u_t, w["wz"], w["t0"], w["cc"], w["a16"], w["d_rows"])
    y2 = _glu(y_t, w_glu, p2)
    merged = _merge(a, y2, w_attn_proj, w_ssm_proj, p2, b, s)
    out = _out_proj(merged, w_out, x.reshape(b * s, d)).reshape(b, s, d)
    return out, (w_glu, w_attn_proj, w_ssm_proj, w_out)


def _prepare(rel_bias, norm_gain, w_in, q_gain, k_gain, lam_re, lam_im, log_dt, b_re, b_im, c_re, c_im, d_skip):
    wz, t0, cc, a16, w_in_bf16 = _ssm_weights(lam_re, lam_im, log_dt, b_re, b_im, c_re, c_im, w_in.astype(F32))
    gains = jnp.concatenate([jnp.tile(q_gain.astype(F32) * (HEAD_DIM ** -0.5 * LOG2E), N_HEADS),
                             jnp.tile(k_gain.astype(F32), N_HEADS),
                             jnp.zeros((P1_WIDTH - 2 * ATTN_WIDTH,), F32)])
    d_rows = jnp.tile(d_skip.astype(F32).reshape(SSM_GROUPS, 1, SSM_GROUP), (1, SSM_BLOCK, 1))
    return {
        "norm_gain": norm_gain.astype(F32),
        "w_in": w_in_bf16,
        "wu_t": w_in[:, COL_U:COL_U + SSM_WIDTH].T.astype(BF16),
        "qk_gain_row": gains.reshape(1, P1_WIDTH),
        "bias_tabs": _bias_tables(rel_bias),
        "wz": wz, "t0": t0, "cc": cc, "a16": a16,
        "d_rows": d_rows.reshape(SSM_GROUPS, SSM_ROWS, 1),
    }


def kernel(x_prompt, x_sample, rel_bias, norm_gain, w_in, q_gain, k_gain, lam_re, lam_im, log_dt, b_re, b_im,
           c_re, c_im, d_skip, w_glu, w_attn_proj, w_ssm_proj, w_out):
    y_prompt, y_sample = x_prompt, x_sample
    for l in range(norm_gain.shape[0]):
        w = _prepare(rel_bias, norm_gain[l], w_in[l], q_gain[l], k_gain[l], lam_re[l], lam_im[l], log_dt[l],
                     b_re[l], b_im[l], c_re[l], c_im[l], d_skip[l])
        late = tuple(m[l].astype(F32) for m in (w_glu, w_attn_proj, w_ssm_proj, w_out))
        y_prompt, late = _layer(y_prompt, w, late)
        y_sample, _ = _layer(y_sample, w, late)
    return (y_prompt, y_sample)
```

```python
import functools
import math

import jax
import jax.numpy as jnp
from jax import lax
from jax.experimental import pallas as pl
from jax.experimental.pallas import tpu as pltpu

F32 = jnp.float32
BF16 = jnp.bfloat16

D_MODEL = 4096
N_HEADS = 16
HEAD_DIM = 128
ATTN_WIDTH = N_HEADS * HEAD_DIM
SIDE_KEYS = 64
N_BUCKETS = 32
REL_MAX_DIST = 1024
NEG = -1e30
LOG2E = math.log2(math.e)
EPS = 1e-6
SSM_WIDTH = 2048
SSM_GROUP = 16
SSM_GROUPS = SSM_WIDTH // SSM_GROUP
SSM_STATE = 64
SSM_BLOCK = 16
SSM_ROWS = SSM_BLOCK * SSM_GROUP

COL_Q = 0
COL_K = ATTN_WIDTH
COL_V = 2 * ATTN_WIDTH
COL_GA = 3 * ATTN_WIDTH
COL_U = 4 * ATTN_WIDTH
COL_GB = COL_U + SSM_WIDTH
COL_MA = COL_GB + SSM_WIDTH
COL_MB = COL_MA + D_MODEL
P1_WIDTH = COL_U
P2_WIDTH = SSM_WIDTH + 2 * D_MODEL

LANES = 128
PLANES = SSM_BLOCK
PERM_TILE = 256
EPILOGUE_CHUNK = 512
GLU_CHUNK = 256
ATTN_GEOM = ((1, 8, 24), (4, 32, 64), (16, 128, 256))
ATTN_BATCH = (4, 4, 4)
ATTN_ROWS = 256
HALO_ROWS = SIDE_KEYS

VMEM_BYTES_V7X = 64 * 1024 * 1024
VMEM_LIMIT = VMEM_BYTES_V7X - 4 * 1024 * 1024


def _params(*sem):
    return pltpu.CompilerParams(dimension_semantics=sem, vmem_limit_bytes=VMEM_LIMIT)


def _prenorm_kernel(x_ref, g_ref, perm_ref, o_ref):
    x = x_ref[...]
    ms = jnp.mean(x * x, axis=-1, keepdims=True)
    h = (x * lax.rsqrt(ms + EPS) * g_ref[...]).astype(BF16)
    hp = jnp.dot(perm_ref[...], h, preferred_element_type=F32).astype(BF16)
    o_ref[...] = hp.reshape(o_ref.shape)


def _prenorm(x, gain, tm=PERM_TILE):
    b, s, d = x.shape
    return pl.pallas_call(
        _prenorm_kernel,
        grid=(b, s // tm),
        in_specs=[pl.BlockSpec((None, tm, d), lambda bb, t: (bb, t, 0)),
                  pl.BlockSpec((1, d), lambda bb, t: (0, 0)),
                  pl.BlockSpec((tm, tm), lambda bb, t: (0, 0))],
        out_specs=pl.BlockSpec((None, SSM_BLOCK, tm // SSM_BLOCK, d), lambda bb, t: (bb, 0, t, 0)),
        out_shape=jax.ShapeDtypeStruct((b, SSM_BLOCK, s // SSM_BLOCK, d), BF16),
        compiler_params=_params("parallel", "parallel"),
        name="prenorm",
    )(x, gain.reshape(1, d), _step_permutation(tm))


def _proj_kernel(h_ref, w_ref, g_ref, *refs, n_norm_tiles, n_side):
    side_in, o_ref, side_out = refs[:n_side], refs[n_side], refs[n_side + 1:]
    j = pl.program_id(0)

    for src, dst in zip(side_in, side_out):
        dst[...] = src[...].astype(BF16)

    @pl.when(j < n_norm_tiles)
    def _():
        def issue(cols):
            return jnp.dot(h_ref[...], w_ref[:, cols], preferred_element_type=F32)

        def finish(cols, acc):
            for hh in range(acc.shape[1] // HEAD_DIM):
                blk = acc[:, hh * HEAD_DIM:(hh + 1) * HEAD_DIM]
                sl = slice(cols.start + hh * HEAD_DIM, cols.start + (hh + 1) * HEAD_DIM)
                ms = jnp.mean(blk * blk, axis=-1, keepdims=True)
                o_ref[:, sl] = blk * lax.rsqrt(ms + EPS) * g_ref[:, sl]

        _column_pipeline(o_ref.shape[1], EPILOGUE_CHUNK, issue, finish)

    @pl.when(j >= n_norm_tiles)
    def _():
        o_ref[...] = jnp.dot(h_ref[...], w_ref[...], preferred_element_type=F32)


def _proj(h, w, col0, width, gain_row, n_norm_cols, name, side=(), tm=512, tn=2048):
    t, k = h.shape
    if gain_row is None:
        gain_row = jnp.zeros((1, width), F32)
    nm = t // tm
    nsteps = (width // tn) * nm

    def side_spec(mat):
        return pl.BlockSpec((mat.shape[0] // nsteps, mat.shape[1]), lambda j, i: (j * nm + i, 0))

    out = pl.pallas_call(
        functools.partial(_proj_kernel, n_norm_tiles=n_norm_cols // tn, n_side=len(side)),
        grid=(width // tn, nm),
        in_specs=[
            pl.BlockSpec((tm, k), lambda j, i: (i, 0)),
            pl.BlockSpec((k, tn), lambda j, i: (0, col0 // tn + j)),
            pl.BlockSpec((1, tn), lambda j, i: (0, j)),
        ] + [side_spec(mat) for mat in side],
        out_specs=[pl.BlockSpec((tm, tn), lambda j, i: (i, j))] + [side_spec(mat) for mat in side],
        out_shape=[jax.ShapeDtypeStruct((t, width), F32)]
        + [jax.ShapeDtypeStruct(mat.shape, BF16) for mat in side],
        compiler_params=_params("parallel", "parallel"),
        name=name,
    )(h, w, gain_row, *side)
    return out[0], tuple(out[1:])


def _proj_u_kernel(w_ref, h_ref, o_ref):
    o_ref[...] = lax.dot_general(w_ref[...], h_ref[...], (((1,), (1,)), ((), ())),
                                 preferred_element_type=F32)


def _proj_u(h_steps, wu_t, tokens=512):
    b, _, ntau, _ = h_steps.shape
    tmt = min(tokens, ntau)
    return pl.pallas_call(
        _proj_u_kernel,
        grid=(b, SSM_BLOCK, ntau // tmt),
        in_specs=[
            pl.BlockSpec((SSM_WIDTH, D_MODEL), lambda bb, i, tt: (0, 0)),
            pl.BlockSpec((None, None, tmt, D_MODEL), lambda bb, i, tt: (bb, i, tt, 0)),
        ],
        out_specs=pl.BlockSpec((None, None, SSM_WIDTH, tmt), lambda bb, i, tt: (bb, i, 0, tt)),
        out_shape=jax.ShapeDtypeStruct((b, SSM_BLOCK, SSM_WIDTH, ntau), F32),
        compiler_params=_params("parallel", "parallel", "parallel"),
        name="proj_u",
    )(wu_t, h_steps)


def _rel_bucket(rel):
    half = N_BUCKETS // 2
    exact = half // 2
    base = jnp.where(rel > 0, half, 0)
    n = jnp.abs(rel)
    nf = jnp.maximum(n, 1).astype(F32)
    large = exact + (jnp.log(nf / exact) / math.log(REL_MAX_DIST / exact) * (half - exact)).astype(jnp.int32)
    large = jnp.minimum(large, half - 1)
    return base + jnp.where(n < exact, n, large)


def _bias_tables(rel_bias):
    buckets = jnp.arange(N_BUCKETS)
    tabs = []
    for d, nq, nk in ATTN_GEOM:
        g = PLANES // d
        koff = (nk - nq) // 2
        aq = jnp.arange(g)[:, None, None, None]
        tq = jnp.arange(nq)[None, :, None, None]
        ak = jnp.arange(g)[None, None, :, None]
        tk = jnp.arange(nk)[None, None, None, :]
        rel = (g * ((tk - koff) - tq) + (ak - aq)).reshape(g * nq, g * nk)
        onehot = (_rel_bucket(rel * d)[..., None] == buckets).astype(F32)
        bias = jnp.einsum('qkb,bh->hqk', onehot, rel_bias.astype(F32), precision=lax.Precision.HIGHEST)
        bias = jnp.where((jnp.abs(rel) <= SIDE_KEYS)[None], bias, NEG)
        trow = jnp.arange(g * nk) % nk - koff
        before = (trow < 0)[None, None, :]
        after = (trow >= nq)[None, None, :]
        bias = bias * LOG2E
        tabs.append(jnp.stack([bias, jnp.where(before, NEG, bias), jnp.where(after, NEG, bias)], axis=1))
    return tabs


def _attn_kernel(q_ref, kp_ref, kc_ref, kn_ref, vp_ref, vc_ref, vn_ref, ga_ref, t1_ref, t4_ref, t16_ref, o_ref,
                 obuf, lbuf):
    t = pl.program_id(2)
    first = t == 0
    last = t == pl.num_programs(2) - 1
    k_refs = (kp_ref, kc_ref, kn_ref)
    v_refs = (vp_ref, vc_ref, vn_ref)
    ones = jnp.ones((HEAD_DIM,), BF16)

    def halo_rows(refs, plane, start, n):
        pieces = []
        r0 = -HALO_ROWS
        for ref in refs:
            r1 = r0 + ref.shape[1]
            lo, hi = max(start, r0), min(start + n, r1)
            if lo < hi:
                pieces.append(ref[plane, lo - r0:hi - r0, :])
            r0 = r1
        return pieces

    edge_bias = {}

    def logits(pi, grp):
        d, nq, nk = ATTN_GEOM[pi]
        tab_ref = (t1_ref, t4_ref, t16_ref)[pi]
        g = PLANES // d
        nblk = ATTN_ROWS // nq
        k0 = -((nk - nq) // 2)

        def rows_of(refs, r, start, n):
            pieces = [piece for a in range(g) for piece in halo_rows(refs, r + d * a, start, n)]
            return jnp.concatenate(pieces, axis=0).astype(BF16)

        def bias_of(blk):
            if (pi, blk) not in edge_bias:
                bias = tab_ref[0]
                if blk == 0:
                    bias = jnp.where(first, tab_ref[1], bias)
                if blk == nblk - 1:
                    bias = jnp.minimum(bias, jnp.where(last, tab_ref[2], tab_ref[0]))
                edge_bias[(pi, blk)] = bias
            return edge_bias[(pi, blk)]

        q = jnp.stack([jnp.concatenate([q_ref[r + d * a, blk * nq:(blk + 1) * nq, :] for a in range(g)],
                                       axis=0).astype(BF16) for r, blk in grp])
        k = jnp.stack([rows_of(k_refs, r, blk * nq + k0, nk) for r, blk in grp])
        v = jnp.stack([rows_of(v_refs, r, blk * nq + k0, nk) for r, blk in grp])
        edges = {blk if blk in (0, nblk - 1) else None for _, blk in grp}
        if len(edges) == 1:
            blk = edges.pop()
            bias = (tab_ref[0] if blk is None else bias_of(blk))[None]
        else:
            bias = jnp.stack([bias_of(blk) if blk in (0, nblk - 1) else tab_ref[0] for _, blk in grp])
        return jnp.einsum('bqd,bkd->bqk', q, k, preferred_element_type=F32) + bias, v

    def finish(pi, grp, s, v):
        d, nq, _ = ATTN_GEOM[pi]
        g = PLANES // d
        m = jnp.max(s, axis=-1, keepdims=True)
        p = jnp.exp2(s - m).astype(BF16)
        v1 = jnp.concatenate([v, jnp.broadcast_to(ones, v.shape)], axis=-1)
        acc = jnp.einsum('bqk,bkd->bqd', p, v1, preferred_element_type=F32)
        den = acc[..., HEAD_DIM:]
        o = acc[..., :HEAD_DIM] / den
        lse = m + jnp.log2(den)
        for bi, (r, blk) in enumerate(grp):
            for a in range(g):
                rows = slice(blk * nq, (blk + 1) * nq)
                obuf[pi, r + d * a, rows, :] = o[bi, a * nq:(a + 1) * nq]
                lbuf[pi, r + d * a, rows, :] = lse[bi, a * nq:(a + 1) * nq]

    work = []
    for pi, (d, nq, _) in enumerate(ATTN_GEOM):
        blocks = [(r, blk) for r in range(d) for blk in range(ATTN_ROWS // nq)]
        work += [(pi, blocks[g0:g0 + ATTN_BATCH[pi]]) for g0 in range(0, len(blocks), ATTN_BATCH[pi])]
    pending = logits(*work[0])
    for idx, (pi, grp) in enumerate(work):
        s, v = pending
        if idx + 1 < len(work):
            pending = logits(*work[idx + 1])
        finish(pi, grp, s, v)

    for i in range(PLANES):
        l0, l1, l2 = lbuf[0, i], lbuf[1, i], lbuf[2, i]
        mx = jnp.maximum(jnp.maximum(l0, l1), l2)
        e0, e1, e2 = jnp.exp2(l0 - mx), jnp.exp2(l1 - mx), jnp.exp2(l2 - mx)
        o = (e0 * obuf[0, i] + e1 * obuf[1, i] + e2 * obuf[2, i]) / (e0 + e1 + e2)
        ga = ga_ref[i]
        o_ref[i] = (o * (ga * jax.nn.sigmoid(ga))).astype(o_ref.dtype)


def _attention(p1, bias_tabs, b, s):
    nrow = s // PLANES
    p1 = p1.reshape(b, PLANES, nrow, P1_WIDTH)
    per = ATTN_ROWS // HALO_ROWS
    n_halo = nrow // HALO_ROWS

    def tile(col):
        return pl.BlockSpec((None, PLANES, ATTN_ROWS, HEAD_DIM), lambda bb, h, t: (bb, 0, t, col // HEAD_DIM + h))

    def prev(col):
        return pl.BlockSpec((None, PLANES, HALO_ROWS, HEAD_DIM),
                            lambda bb, h, t: (bb, 0, jnp.maximum(t * per - 1, 0), col // HEAD_DIM + h))

    def nxt(col):
        return pl.BlockSpec((None, PLANES, HALO_ROWS, HEAD_DIM),
                            lambda bb, h, t: (bb, 0, jnp.minimum((t + 1) * per, n_halo - 1), col // HEAD_DIM + h))

    def table(tab):
        return pl.BlockSpec((None,) + tab.shape[1:], lambda bb, h, t: (h, 0, 0, 0))

    part_buf = pltpu.VMEM((len(ATTN_GEOM), PLANES, ATTN_ROWS, HEAD_DIM), F32)
    return pl.pallas_call(
        _attn_kernel,
        grid=(b, N_HEADS, nrow // ATTN_ROWS),
        in_specs=[tile(COL_Q), prev(COL_K), tile(COL_K), nxt(COL_K), prev(COL_V), tile(COL_V), nxt(COL_V),
                  tile(COL_GA)] + [table(tab) for tab in bias_tabs],
        out_specs=pl.BlockSpec((None, PLANES, ATTN_ROWS, HEAD_DIM), lambda bb, h, t: (bb, 0, t, h)),
        out_shape=jax.ShapeDtypeStruct((b, PLANES, nrow, ATTN_WIDTH), BF16),
        scratch_shapes=[part_buf, part_buf],
        compiler_params=_params("parallel", "parallel", "parallel"),
        name="attention",
    )(p1, p1, p1, p1, p1, p1, p1, p1, *bias_tabs)


def _ssm_prep_kernel(arow_ref, acol_ref, c_ref, bw_ref, side_ref, wz_ref, t0_ref, cc_ref, side_out_ref, *, gc):
    nb, hc, p = SSM_BLOCK, SSM_GROUP, SSM_STATE
    side_out_ref[...] = side_ref[...].astype(BF16)
    blk_of_lane = lax.broadcasted_iota(jnp.int32, (hc, SSM_ROWS), 1) // hc
    blk_of_lane_p = lax.broadcasted_iota(jnp.int32, (p, SSM_ROWS), 1) // hc

    def body(gi, carry):
        lag_kernels = []
        for dirn in range(2):
            ar = arow_ref[gi, 2 * dirn:2 * dirn + 1, :]
            ai = arow_ref[gi, 2 * dirn + 1:2 * dirn + 2, :]
            cr = c_ref[gi, 2 * dirn]
            ci = c_ref[gi, 2 * dirn + 1]
            pr = jnp.ones((1, p), F32)
            pi = jnp.zeros((1, p), F32)
            ca_re, ca_im = [], []
            for _ in range(nb + 1):
                ca_re.append(cr * pr - ci * pi)
                ca_im.append(cr * pi + ci * pr)
                pr, pi = pr * ar - pi * ai, pr * ai + pi * ar
            bre = bw_ref[gi, 2 * dirn]
            bim = bw_ref[gi, 2 * dirn + 1]
            lag_kernels.append(
                jnp.dot(jnp.concatenate(ca_re[:nb], axis=0).astype(BF16), bre.astype(BF16),
                        preferred_element_type=F32)
                - jnp.dot(jnp.concatenate(ca_im[:nb], axis=0).astype(BF16), bim.astype(BF16),
                          preferred_element_type=F32))
            order = range(1, nb + 1) if dirn == 0 else range(nb, 0, -1)
            c0 = 2 * p * dirn
            cc_ref[gi, :, c0:c0 + p] = jnp.concatenate([ca_re[k] for k in order], axis=0).astype(BF16)
            cc_ref[gi, :, c0 + p:c0 + 2 * p] = (-jnp.concatenate([ca_im[k] for k in order], axis=0)).astype(BF16)
            acr = acol_ref[gi, c0:c0 + p, :]
            aci = acol_ref[gi, c0 + p:c0 + 2 * p, :]
            qr = jnp.ones((p, 1), F32)
            qi = jnp.zeros((p, 1), F32)
            powers = []
            for _ in range(nb):
                powers.append((qr, qi))
                qr, qi = qr * acr - qi * aci, qr * aci + qi * acr
            per = jnp.zeros((p, SSM_ROWS), F32)
            pei = jnp.zeros((p, SSM_ROWS), F32)
            for i in range(nb):
                e = nb - 1 - i if dirn == 0 else i
                per = jnp.where(blk_of_lane_p == i, powers[e][0], per)
                pei = jnp.where(blk_of_lane_p == i, powers[e][1], pei)
            wz_ref[gi, c0:c0 + p, :] = (per * bre - pei * bim).astype(BF16)
            wz_ref[gi, c0 + p:c0 + 2 * p, :] = (per * bim + pei * bre).astype(BF16)
        kf, kb = lag_kernels
        diag = kf[0:hc, :] + kb[0:hc, :]
        for j in range(nb):
            acc = jnp.zeros((hc, SSM_ROWS), F32)
            for i in range(nb):
                if i < j:
                    src = kf[hc * (j - i):hc * (j - i + 1), :]
                elif i > j:
                    src = kb[hc * (i - j):hc * (i - j + 1), :]
                else:
                    src = diag
                acc = jnp.where(blk_of_lane == i, src, acc)
            t0_ref[gi, hc * j:hc * (j + 1), :] = acc.astype(BF16)
        return carry

    lax.fori_loop(0, gc, body, 0)


def _ssm_weights(lam_re, lam_im, log_dt, b_re, b_im, c_re, c_im, side, gc=2):
    g, p, hc = SSM_GROUPS, SSM_STATE, SSM_GROUP
    dt = jnp.exp(log_dt.astype(F32))[..., None]
    lr = lam_re.astype(F32)
    li = lam_im.astype(F32)
    mag = jnp.exp(lr * dt)
    ab_re = mag * jnp.cos(li * dt)
    ab_im = mag * jnp.sin(li * dt)
    den = lr * lr + li * li
    f_re = ((ab_re - 1.0) * lr + ab_im * li) / den
    f_im = (ab_im * lr - (ab_re - 1.0) * li) / den
    br = b_re.astype(F32)
    bi = b_im.astype(F32)
    bb_re = f_re[..., None] * br - f_im[..., None] * bi
    bb_im = f_re[..., None] * bi + f_im[..., None] * br
    mag16 = jnp.exp(SSM_BLOCK * (lr * dt))
    a16_re = mag16 * jnp.cos(SSM_BLOCK * (li * dt))
    a16_im = mag16 * jnp.sin(SSM_BLOCK * (li * dt))

    def pack(re, im):
        return jnp.stack([re[0], im[0], re[1], im[1]], axis=1)

    arow = pack(ab_re, ab_im)
    acol = arow.reshape(g, 4 * p, 1)
    a16 = pack(a16_re, a16_im).reshape(g, 4 * p, 1)
    cmat = pack(c_re.astype(F32), c_im.astype(F32))
    bw = jnp.tile(pack(bb_re, bb_im), (1, 1, 1, SSM_BLOCK))
    wspec = pl.BlockSpec((gc, SSM_ROWS, SSM_ROWS), lambda i: (i, 0, 0))
    wshape = jax.ShapeDtypeStruct((g, SSM_ROWS, SSM_ROWS), BF16)
    nsteps = g // gc
    layer, side = side
    rows, cols = side.shape[1] // nsteps, side.shape[2]
    side_spec = pl.BlockSpec((rows, cols), lambda i: (i, 0))
    wz, t0, cc, side_bf16 = pl.pallas_call(
        functools.partial(_ssm_prep_kernel, gc=gc),
        grid=(nsteps,),
        in_specs=[pl.BlockSpec((gc, 4, p), lambda i: (i, 0, 0)),
                  pl.BlockSpec((gc, 4 * p, 1), lambda i: (i, 0, 0)),
                  pl.BlockSpec((gc, 4, hc, p), lambda i: (i, 0, 0, 0)),
                  pl.BlockSpec((gc, 4, p, SSM_ROWS), lambda i: (i, 0, 0, 0)),
                  pl.BlockSpec((None, rows, cols), lambda i: (layer, i, 0))],
        out_specs=[wspec, wspec, wspec, side_spec],
        out_shape=[wshape, wshape, wshape, jax.ShapeDtypeStruct(side.shape[1:], BF16)],
        compiler_params=_params("parallel"),
        name="ssm_prep",
    )(arow, acol, cmat, bw, side)
    return wz, t0, cc, a16, side_bf16


def _gelu_tanh(x):
    return 0.5 * x * (1.0 + jnp.tanh(math.sqrt(2.0 / math.pi) * (x + 0.044715 * (x * x * x))))


def _cmul(ar, ai, br, bi):
    return ar * br - ai * bi, ar * bi + ai * br


def _block_states(zs, coefs, nc):
    p = SSM_STATE
    ng = len(zs)
    rows = ng * p
    lane = lax.broadcasted_iota(jnp.int32, (rows, LANES), 1)
    zero = jnp.zeros((rows, LANES), F32)

    def part(arrs, r0, c=None):
        cols = slice(None) if c is None else slice(LANES * c, LANES * (c + 1))
        return jnp.concatenate([x[r0:r0 + p, cols] for x in arrs], axis=0)

    halves = []
    for dirn in range(2):
        base = 2 * p * dirn
        zr = [part(zs, base, c) for c in range(nc)]
        zi = [part(zs, base + p, c) for c in range(nc)]
        ar = jnp.broadcast_to(part(coefs, base), (rows, LANES))
        ai = jnp.broadcast_to(part(coefs, base + p), (rows, LANES))
        order = list(range(nc)) if dirn == 0 else list(range(nc - 1, -1, -1))
        ir, ii = {order[0]: zr[order[0]]}, {order[0]: zi[order[0]]}
        for prev, c in zip(order[:-1], order[1:]):
            tr, ti = _cmul(ar, ai, ir[prev], ii[prev])
            ir[c], ii[c] = zr[c] + tr, zi[c] + ti
        pows = [(ar, ai)]
        for _ in range(nc - 1):
            pows.append(_cmul(pows[-1][0], pows[-1][1], ar, ai))
        gr, gi = ir[order[-1]], ii[order[-1]]
        br, bi = pows[nc - 1]
        s = 1
        while s < LANES:
            keep = (lane >= s) if dirn == 0 else (lane < LANES - s)
            amt = s if dirn == 0 else LANES - s
            rr = jnp.where(keep, pltpu.roll(gr, amt, 1), zero)
            ri = jnp.where(keep, pltpu.roll(gi, amt, 1), zero)
            tr, ti = _cmul(br, bi, rr, ri)
            gr, gi = gr + tr, gi + ti
            br, bi = _cmul(br, bi, br, bi)
            s *= 2
        keep = (lane >= 1) if dirn == 0 else (lane < LANES - 1)
        amt = 1 if dirn == 0 else LANES - 1
        er = jnp.where(keep, pltpu.roll(gr, amt, 1), zero)
        ei = jnp.where(keep, pltpu.roll(gi, amt, 1), zero)
        hr, hi = [None] * nc, [None] * nc
        hr[order[0]], hi[order[0]] = er, ei
        for k, (prev, c) in enumerate(zip(order[:-1], order[1:])):
            tr, ti = _cmul(pows[k][0], pows[k][1], er, ei)
            hr[c], hi[c] = ir[prev] + tr, ii[prev] + ti
        halves += [jnp.concatenate(hr, axis=1), jnp.concatenate(hi, axis=1)]
    return [jnp.concatenate([h[g * p:(g + 1) * p] for h in halves], axis=0) for g in range(ng)]


def _ssm_kernel(u_ref, wz_ref, t0_ref, cc_ref, a_ref, d_ref, perm_ref, unperm_ref, y_ref, *, gc, ntau):
    nc = ntau // LANES
    rows = [slice(SSM_GROUP * gi, SSM_GROUP * (gi + 1)) for gi in range(gc)]
    zs = []
    for gi in range(gc):
        x = u_ref[:, rows[gi], :].reshape(SSM_ROWS, ntau).astype(BF16)
        xp = jnp.dot(x, perm_ref[...], preferred_element_type=F32).astype(BF16)
        zs.append(jnp.dot(wz_ref[gi], xp, preferred_element_type=F32))
    states = _block_states(zs, [a_ref[gi] for gi in range(gc)], nc)
    for gi in range(gc):
        xf = u_ref[:, rows[gi], :].reshape(SSM_ROWS, ntau)
        h = jnp.dot(states[gi].astype(BF16), unperm_ref[...], preferred_element_type=F32).astype(BF16)
        y = (jnp.dot(t0_ref[gi], xf.astype(BF16), preferred_element_type=F32)
             + jnp.dot(cc_ref[gi], h, preferred_element_type=F32))
        y = _gelu_tanh(y + d_ref[gi] * xf)
        y_ref[:, rows[gi], :] = y.astype(y_ref.dtype).reshape(SSM_BLOCK, SSM_GROUP, ntau)


def _ssm(u_t, wz, t0, cc, a16, d_rows, gc=8):
    b, _, _, ntau = u_t.shape
    nc = ntau // LANES
    j = jnp.arange(ntau)
    src = (j % LANES) * nc + j // LANES
    perm = (jnp.arange(ntau)[:, None] == src[None, :]).astype(BF16)
    wspec = pl.BlockSpec((gc, SSM_ROWS, SSM_ROWS), lambda bb, g: (g, 0, 0))
    vspec = pl.BlockSpec((gc, SSM_ROWS, 1), lambda bb, g: (g, 0, 0))
    uspec = pl.BlockSpec((None, SSM_BLOCK, SSM_GROUP * gc, ntau), lambda bb, g: (bb, 0, g, 0))
    pspec = pl.BlockSpec((ntau, ntau), lambda bb, g: (0, 0))
    return pl.pallas_call(
        functools.partial(_ssm_kernel, gc=gc, ntau=ntau),
        grid=(b, SSM_GROUPS // gc),
        in_specs=[uspec, wspec, wspec, wspec, vspec, vspec, pspec, pspec],
        out_specs=uspec,
        out_shape=jax.ShapeDtypeStruct(u_t.shape, BF16),
        compiler_params=_params("parallel", "parallel"),
        name="ssm",
    )(u_t, wz, t0, cc, a16, d_rows, perm, perm.T)


def _column_pipeline(width, chunk, issue, finish):
    chunks = [slice(c, c + chunk) for c in range(0, width, chunk)]
    pending = issue(chunks[0])
    for idx, cols in enumerate(chunks):
        current = pending
        if idx + 1 < len(chunks):
            pending = issue(chunks[idx + 1])
        finish(cols, current)


def _glu_kernel(y_ref, w1_ref, w2_ref, gb_ref, o_ref):
    pp, tmt, _ = o_ref.shape
    yt = jnp.concatenate([y_ref[p].T for p in range(pp)], axis=0)

    def issue(cols):
        return (jnp.dot(yt, w1_ref[:, cols], preferred_element_type=F32),
                jnp.dot(yt, w2_ref[:, cols], preferred_element_type=F32))

    def finish(cols, gl):
        gb = gb_ref[:, :, cols].reshape(pp * tmt, -1)
        out = (gl[0] * jax.nn.sigmoid(gl[1]) * (gb * jax.nn.sigmoid(gb))).astype(o_ref.dtype)
        o_ref[:, :, cols] = out.reshape(pp, tmt, -1)

    _column_pipeline(o_ref.shape[2], GLU_CHUNK, issue, finish)


def _glu(y_t, w_glu, p2, tokens=512, tn=2048):
    b, _, _, ntau = y_t.shape
    tmt = min(tokens, ntau)
    pp = tokens // tmt
    nj = SSM_WIDTH // tn
    p2v = p2.reshape(b, SSM_BLOCK, ntau, P2_WIDTH)
    out = pl.pallas_call(
        _glu_kernel,
        grid=(nj, b, SSM_BLOCK // pp, ntau // tmt),
        in_specs=[
            pl.BlockSpec((None, pp, SSM_WIDTH, tmt), lambda j, bb, i, tt: (bb, i, 0, tt)),
            pl.BlockSpec((SSM_WIDTH, tn), lambda j, bb, i, tt: (0, j)),
            pl.BlockSpec((SSM_WIDTH, tn), lambda j, bb, i, tt: (0, nj + j)),
            pl.BlockSpec((None, pp, tmt, tn), lambda j, bb, i, tt: (bb, i, tt, j)),
        ],
        out_specs=pl.BlockSpec((None, pp, tmt, tn), lambda j, bb, i, tt: (bb, i, tt, j)),
        out_shape=jax.ShapeDtypeStruct((b, SSM_BLOCK, ntau, SSM_WIDTH), BF16),
        compiler_params=_params("parallel", "parallel", "parallel", "parallel"),
        name="glu",
    )(y_t, w_glu, w_glu, p2v)
    return out.reshape(b * SSM_BLOCK * ntau, SSM_WIDTH)


def _merge_kernel(a_ref, y_ref, wa_ref, wb_ref, ma_ref, mb_ref, unperm_ref, o_ref):
    rows = o_ref.shape[0]
    a = a_ref[...].reshape(rows, ATTN_WIDTH)
    y = y_ref[...].reshape(rows, SSM_WIDTH)
    sub = unperm_ref.shape[0]
    n = sub // SSM_BLOCK

    def issue(cols):
        return (jnp.dot(a, wa_ref[:, cols], preferred_element_type=F32),
                jnp.dot(y, wb_ref[:, cols], preferred_element_type=F32))

    def finish(cols, z):
        ma = ma_ref[:, :, cols].reshape(rows, -1)
        mb = mb_ref[:, :, cols].reshape(rows, -1)
        merged = (jax.nn.sigmoid(ma) * z[0] + jax.nn.sigmoid(mb) * z[1]).astype(BF16)
        merged = merged.reshape(SSM_BLOCK, rows // SSM_BLOCK, -1)
        for k in range(rows // sub):
            part = merged[:, k * n:(k + 1) * n, :].reshape(sub, -1)
            o_ref[k * sub:(k + 1) * sub, cols] = jnp.dot(unperm_ref[...], part,
                                                         preferred_element_type=F32).astype(o_ref.dtype)

    _column_pipeline(o_ref.shape[1], EPILOGUE_CHUNK, issue, finish)


def _step_permutation(tm):
    n = tm // SSM_BLOCK
    r = jnp.arange(tm)
    src = SSM_BLOCK * (r % n) + r // n
    return (src[:, None] == jnp.arange(tm)[None, :]).astype(BF16)


def _merge(a, y2, w_attn_proj, w_ssm_proj, p2, b, s, tm=512, tn=1024):
    ntau = s // SSM_BLOCK
    n = tm // SSM_BLOCK
    ma0 = (COL_MA - COL_GB) // tn
    mb0 = (COL_MB - COL_GB) // tn

    def steps(arr, width):
        return arr.reshape(b, SSM_BLOCK, ntau, width)

    return pl.pallas_call(
        _merge_kernel,
        grid=(D_MODEL // tn, b, s // tm),
        in_specs=[
            pl.BlockSpec((None, SSM_BLOCK, n, ATTN_WIDTH), lambda j, bb, t: (bb, 0, t, 0)),
            pl.BlockSpec((None, SSM_BLOCK, n, SSM_WIDTH), lambda j, bb, t: (bb, 0, t, 0)),
            pl.BlockSpec((ATTN_WIDTH, tn), lambda j, bb, t: (0, j)),
            pl.BlockSpec((SSM_WIDTH, tn), lambda j, bb, t: (0, j)),
            pl.BlockSpec((None, SSM_BLOCK, n, tn), lambda j, bb, t: (bb, 0, t, ma0 + j)),
            pl.BlockSpec((None, SSM_BLOCK, n, tn), lambda j, bb, t: (bb, 0, t, mb0 + j)),
            pl.BlockSpec((PERM_TILE, PERM_TILE), lambda j, bb, t: (0, 0)),
        ],
        out_specs=pl.BlockSpec((tm, tn), lambda j, bb, t: (bb * (s // tm) + t, j)),
        out_shape=jax.ShapeDtypeStruct((b * s, D_MODEL), BF16),
        compiler_params=_params("parallel", "parallel", "parallel"),
        name="merge",
    )(steps(a, ATTN_WIDTH), steps(y2, SSM_WIDTH), w_attn_proj, w_ssm_proj,
      steps(p2, P2_WIDTH), steps(p2, P2_WIDTH), _step_permutation(PERM_TILE).T)


def _out_kernel(m_ref, w_ref, x_ref, o_ref):
    o_ref[...] = x_ref[...] + jnp.dot(m_ref[...], w_ref[...], preferred_element_type=F32)


def _out_proj(merged, w_out, x2, tm=1024, tn=1024):
    t = merged.shape[0]
    return pl.pallas_call(
        _out_kernel,
        grid=(D_MODEL // tn, t // tm),
        in_specs=[
            pl.BlockSpec((tm, D_MODEL), lambda j, i: (i, 0)),
            pl.BlockSpec((D_MODEL, tn), lambda j, i: (0, j)),
            pl.BlockSpec((tm, tn), lambda j, i: (i, j)),
        ],
        out_specs=pl.BlockSpec((tm, tn), lambda j, i: (i, j)),
        out_shape=jax.ShapeDtypeStruct((t, D_MODEL), F32),
        compiler_params=_params("parallel", "parallel"),
        name="out_proj",
    )(merged, w_out, x2)


def _layer(x, w, late):
    b, s, d = x.shape
    h_steps = _prenorm(x, w["norm_gain"])
    h2 = h_steps.reshape(b * s, d)
    side = late if late[0].dtype != BF16 else ()
    p1, rounded = _proj(h2, w["w_in"], COL_Q, P1_WIDTH, w["qk_gain_row"], 2 * ATTN_WIDTH, "proj_attn", side)
    w_glu, w_attn_proj, w_ssm_proj, w_out = rounded if side else late
    p2, _ = _proj(h2, w["w_in"], COL_GB, P2_WIDTH, None, 0, "proj_gates")
    u_t = _proj_u(h_steps, w["wu_t"])
    a = _attention(p1, w["bias_tabs"], b, s).reshape(b * s, ATTN_WIDTH)
    y_t = _ssm(u_t, w["wz"], w["t0"], w["cc"], w["a16"], w["d_rows"])
    y2 = _glu(y_t, w_glu, p2)
    merged = _merge(a, y2, w_attn_proj, w_ssm_proj, p2, b, s)
    out = _out_proj(merged, w_out, x.reshape(b * s, d)).reshape(b, s, d)
    return out, (w_glu, w_attn_proj, w_ssm_proj, w_out)


def _prepare(layer, w_in_all, rel_bias, norm_gain, q_gain, k_gain, lam_re, lam_im, log_dt, b_re, b_im, c_re, c_im,
             d_skip):
    wz, t0, cc, a16, w_in_bf16 = _ssm_weights(lam_re, lam_im, log_dt, b_re, b_im, c_re, c_im,
                                              (layer, w_in_all.astype(F32)))
    gains = jnp.concatenate([jnp.tile(q_gain.astype(F32) * (HEAD_DIM ** -0.5 * LOG2E), N_HEADS),
                             jnp.tile(k_gain.astype(F32), N_HEADS),
                             jnp.zeros((P1_WIDTH - 2 * ATTN_WIDTH,), F32)])
    d_rows = jnp.tile(d_skip.astype(F32).reshape(SSM_GROUPS, 1, SSM_GROUP), (1, SSM_BLOCK, 1))
    return {
        "norm_gain": norm_gain.astype(F32),
        "w_in": w_in_bf16,
        "wu_t": w_in_bf16[:, COL_U:COL_U + SSM_WIDTH].T,
        "qk_gain_row": gains.reshape(1, P1_WIDTH),
        "bias_tabs": _bias_tables(rel_bias),
        "wz": wz, "t0": t0, "cc": cc, "a16": a16,
        "d_rows": d_rows.reshape(SSM_GROUPS, SSM_ROWS, 1),
    }


def kernel(x_prompt, x_sample, rel_bias, norm_gain, w_in, q_gain, k_gain, lam_re, lam_im, log_dt, b_re, b_im,
           c_re, c_im, d_skip, w_glu, w_attn_proj, w_ssm_proj, w_out):
    y_prompt, y_sample = x_prompt, x_sample
    for l in range(norm_gain.shape[0]):
        w = _prepare(l, w_in, rel_bias, norm_gain[l], q_gain[l], k_gain[l], lam_re[l], lam_im[l], log_dt[l],
                     b_re[l], b_im[l], c_re[l], c_im[l], d_skip[l])
        late = tuple(m[l].astype(F32) for m in (w_glu, w_attn_proj, w_ssm_proj, w_out))
        y_prompt, late = _layer(y_prompt, w, late)
        y_sample, _ = _layer(y_sample, w, late)
    return (y_prompt, y_sample)
```

```python
import functools
import math

import jax
import jax.numpy as jnp
from jax import lax
from jax.experimental import pallas as pl
from jax.experimental.pallas import tpu as pltpu

F32 = jnp.float32
BF16 = jnp.bfloat16

D_MODEL = 4096
N_HEADS = 16
HEAD_DIM = 128
ATTN_WIDTH = N_HEADS * HEAD_DIM
SIDE_KEYS = 64
N_BUCKETS = 32
REL_MAX_DIST = 1024
NEG = -1e30
LOG2E = math.log2(math.e)
EPS = 1e-6
SSM_WIDTH = 2048
SSM_GROUP = 16
SSM_GROUPS = SSM_WIDTH // SSM_GROUP
SSM_STATE = 64
SSM_BLOCK = 16
SSM_ROWS = SSM_BLOCK * SSM_GROUP

COL_Q = 0
COL_K = ATTN_WIDTH
COL_V = 2 * ATTN_WIDTH
COL_GA = 3 * ATTN_WIDTH
COL_U = 4 * ATTN_WIDTH
COL_GB = COL_U + SSM_WIDTH
COL_MA = COL_GB + SSM_WIDTH
COL_MB = COL_MA + D_MODEL
P1_WIDTH = COL_U
P2_WIDTH = SSM_WIDTH + 2 * D_MODEL

LANES = 128
PLANES = SSM_BLOCK
PERM_TILE = 256
EPILOGUE_CHUNK = 512
GLU_CHUNK = 256
ATTN_GEOM = ((1, 8, 16), (4, 32, 64), (16, 128, 256))
ATTN_BATCH = (4, 4, 4)
ATTN_ROWS = 256
HALO_ROWS = SIDE_KEYS

VMEM_BYTES_V7X = 64 * 1024 * 1024
VMEM_LIMIT = VMEM_BYTES_V7X - 4 * 1024 * 1024


def _params(*sem):
    return pltpu.CompilerParams(dimension_semantics=sem, vmem_limit_bytes=VMEM_LIMIT)


def _prenorm_kernel(x_ref, g_ref, perm_ref, o_ref):
    x = x_ref[...]
    ms = jnp.mean(x * x, axis=-1, keepdims=True)
    h = (x * lax.rsqrt(ms + EPS) * g_ref[...]).astype(BF16)
    hp = jnp.dot(perm_ref[...], h, preferred_element_type=F32).astype(BF16)
    o_ref[...] = hp.reshape(o_ref.shape)


def _prenorm(x, gain, tm=PERM_TILE):
    b, s, d = x.shape
    return pl.pallas_call(
        _prenorm_kernel,
        grid=(b, s // tm),
        in_specs=[pl.BlockSpec((None, tm, d), lambda bb, t: (bb, t, 0)),
                  pl.BlockSpec((1, d), lambda bb, t: (0, 0)),
                  pl.BlockSpec((tm, tm), lambda bb, t: (0, 0))],
        out_specs=pl.BlockSpec((None, SSM_BLOCK, tm // SSM_BLOCK, d), lambda bb, t: (bb, 0, t, 0)),
        out_shape=jax.ShapeDtypeStruct((b, SSM_BLOCK, s // SSM_BLOCK, d), BF16),
        compiler_params=_params("parallel", "parallel"),
        name="prenorm",
    )(x, gain.reshape(1, d), _step_permutation(tm))


def _proj_kernel(h_ref, w_ref, g_ref, *refs, n_norm_tiles, n_side):
    side_in, o_ref, side_out = refs[:n_side], refs[n_side], refs[n_side + 1:]
    j = pl.program_id(0)

    for src, dst in zip(side_in, side_out):
        dst[...] = src[...].astype(BF16)

    @pl.when(j < n_norm_tiles)
    def _():
        def issue(cols):
            return jnp.dot(h_ref[...], w_ref[:, cols], preferred_element_type=F32)

        def finish(cols, acc):
            for hh in range(acc.shape[1] // HEAD_DIM):
                blk = acc[:, hh * HEAD_DIM:(hh + 1) * HEAD_DIM]
                sl = slice(cols.start + hh * HEAD_DIM, cols.start + (hh + 1) * HEAD_DIM)
                ms = jnp.mean(blk * blk, axis=-1, keepdims=True)
                o_ref[:, sl] = blk * lax.rsqrt(ms + EPS) * g_ref[:, sl]

        _column_pipeline(o_ref.shape[1], EPILOGUE_CHUNK, issue, finish)

    @pl.when(j >= n_norm_tiles)
    def _():
        o_ref[...] = jnp.dot(h_ref[...], w_ref[...], preferred_element_type=F32)


def _proj(h, w, col0, width, gain_row, n_norm_cols, name, side=(), tm=512, tn=2048):
    t, k = h.shape
    if gain_row is None:
        gain_row = jnp.zeros((1, width), F32)
    nm = t // tm
    nsteps = (width // tn) * nm

    def side_spec(mat):
        return pl.BlockSpec((mat.shape[0] // nsteps, mat.shape[1]), lambda j, i: (j * nm + i, 0))

    out = pl.pallas_call(
        functools.partial(_proj_kernel, n_norm_tiles=n_norm_cols // tn, n_side=len(side)),
        grid=(width // tn, nm),
        in_specs=[
            pl.BlockSpec((tm, k), lambda j, i: (i, 0)),
            pl.BlockSpec((k, tn), lambda j, i: (0, col0 // tn + j)),
            pl.BlockSpec((1, tn), lambda j, i: (0, j)),
        ] + [side_spec(mat) for mat in side],
        out_specs=[pl.BlockSpec((tm, tn), lambda j, i: (i, j))] + [side_spec(mat) for mat in side],
        out_shape=[jax.ShapeDtypeStruct((t, width), F32)]
        + [jax.ShapeDtypeStruct(mat.shape, BF16) for mat in side],
        compiler_params=_params("parallel", "parallel"),
        name=name,
    )(h, w, gain_row, *side)
    return out[0], tuple(out[1:])


def _proj_u_kernel(w_ref, h_ref, o_ref):
    o_ref[...] = lax.dot_general(w_ref[...], h_ref[...], (((1,), (1,)), ((), ())),
                                 preferred_element_type=F32)


def _proj_u(h_steps, wu_t, tokens=512):
    b, _, ntau, _ = h_steps.shape
    tmt = min(tokens, ntau)
    return pl.pallas_call(
        _proj_u_kernel,
        grid=(b, SSM_BLOCK, ntau // tmt),
        in_specs=[
            pl.BlockSpec((SSM_WIDTH, D_MODEL), lambda bb, i, tt: (0, 0)),
            pl.BlockSpec((None, None, tmt, D_MODEL), lambda bb, i, tt: (bb, i, tt, 0)),
        ],
        out_specs=pl.BlockSpec((None, None, SSM_WIDTH, tmt), lambda bb, i, tt: (bb, i, 0, tt)),
        out_shape=jax.ShapeDtypeStruct((b, SSM_BLOCK, SSM_WIDTH, ntau), F32),
        compiler_params=_params("parallel", "parallel", "parallel"),
        name="proj_u",
    )(wu_t, h_steps)


def _rel_bucket(rel):
    half = N_BUCKETS // 2
    exact = half // 2
    base = jnp.where(rel > 0, half, 0)
    n = jnp.abs(rel)
    nf = jnp.maximum(n, 1).astype(F32)
    large = exact + (jnp.log(nf / exact) / math.log(REL_MAX_DIST / exact) * (half - exact)).astype(jnp.int32)
    large = jnp.minimum(large, half - 1)
    return base + jnp.where(n < exact, n, large)


def _bias_tables(rel_bias):
    buckets = jnp.arange(N_BUCKETS)
    tabs = []
    for d, nq, nk in ATTN_GEOM:
        g = PLANES // d
        koff = (nk - nq) // 2
        aq = jnp.arange(g)[:, None, None, None]
        tq = jnp.arange(nq)[None, :, None, None]
        ak = jnp.arange(g)[None, None, :, None]
        tk = jnp.arange(nk)[None, None, None, :]
        rel = (g * ((tk - koff) - tq) + (ak - aq)).reshape(g * nq, g * nk)
        onehot = (_rel_bucket(rel * d)[..., None] == buckets).astype(F32)
        bias = jnp.einsum('qkb,bh->hqk', onehot, rel_bias.astype(F32), precision=lax.Precision.HIGHEST)
        bias = jnp.where((jnp.abs(rel) <= SIDE_KEYS)[None], bias, NEG)
        trow = jnp.arange(g * nk) % nk - koff
        before = (trow < 0)[None, None, :]
        after = (trow >= nq)[None, None, :]
        bias = bias * LOG2E
        tabs.append(jnp.stack([bias, jnp.where(before, NEG, bias), jnp.where(after, NEG, bias)], axis=1))
    return tabs


def _attn_kernel(q_ref, kp_ref, kc_ref, kn_ref, vp_ref, vc_ref, vn_ref, ga_ref, t1_ref, t4_ref, t16_ref, o_ref,
                 obuf, lbuf):
    t = pl.program_id(2)
    first = t == 0
    last = t == pl.num_programs(2) - 1
    k_refs = (kp_ref, kc_ref, kn_ref)
    v_refs = (vp_ref, vc_ref, vn_ref)
    ones = jnp.ones((HEAD_DIM,), BF16)

    def halo_rows(refs, plane, start, n):
        pieces = []
        r0 = -HALO_ROWS
        for ref in refs:
            r1 = r0 + ref.shape[1]
            lo, hi = max(start, r0), min(start + n, r1)
            if lo < hi:
                pieces.append(ref[plane, lo - r0:hi - r0, :])
            r0 = r1
        return pieces

    edge_bias = {}

    def logits(pi, grp):
        d, nq, nk = ATTN_GEOM[pi]
        tab_ref = (t1_ref, t4_ref, t16_ref)[pi]
        g = PLANES // d
        nblk = ATTN_ROWS // nq
        k0 = -((nk - nq) // 2)

        def rows_of(refs, r, start, n):
            pieces = [piece for a in range(g) for piece in halo_rows(refs, r + d * a, start, n)]
            return jnp.concatenate(pieces, axis=0).astype(BF16)

        def bias_of(blk):
            if (pi, blk) not in edge_bias:
                bias = tab_ref[0]
                if blk == 0:
                    bias = jnp.where(first, tab_ref[1], bias)
                if blk == nblk - 1:
                    bias = jnp.minimum(bias, jnp.where(last, tab_ref[2], tab_ref[0]))
                edge_bias[(pi, blk)] = bias
            return edge_bias[(pi, blk)]

        q = jnp.stack([jnp.concatenate([q_ref[r + d * a, blk * nq:(blk + 1) * nq, :] for a in range(g)],
                                       axis=0).astype(BF16) for r, blk in grp])
        k = jnp.stack([rows_of(k_refs, r, blk * nq + k0, nk) for r, blk in grp])
        v = jnp.stack([rows_of(v_refs, r, blk * nq + k0, nk) for r, blk in grp])
        edges = {blk if blk in (0, nblk - 1) else None for _, blk in grp}
        if len(edges) == 1:
            blk = edges.pop()
            bias = (tab_ref[0] if blk is None else bias_of(blk))[None]
        else:
            bias = jnp.stack([bias_of(blk) if blk in (0, nblk - 1) else tab_ref[0] for _, blk in grp])
        return jnp.einsum('bqd,bkd->bqk', q, k, preferred_element_type=F32) + bias, v

    def finish(pi, grp, s, v):
        d, nq, _ = ATTN_GEOM[pi]
        g = PLANES // d
        m = jnp.max(s, axis=-1, keepdims=True)
        p = jnp.exp2(s - m).astype(BF16)
        v1 = jnp.concatenate([v, jnp.broadcast_to(ones, v.shape)], axis=-1)
        acc = jnp.einsum('bqk,bkd->bqd', p, v1, preferred_element_type=F32)
        den = acc[..., HEAD_DIM:]
        o = acc[..., :HEAD_DIM] / den
        lse = m + jnp.log2(den)
        for bi, (r, blk) in enumerate(grp):
            for a in range(g):
                rows = slice(blk * nq, (blk + 1) * nq)
                obuf[pi, r + d * a, rows, :] = o[bi, a * nq:(a + 1) * nq]
                lbuf[pi, r + d * a, rows, :] = lse[bi, a * nq:(a + 1) * nq]

    work = []
    for pi, (d, nq, _) in enumerate(ATTN_GEOM):
        blocks = [(r, blk) for r in range(d) for blk in range(ATTN_ROWS // nq)]
        work += [(pi, blocks[g0:g0 + ATTN_BATCH[pi]]) for g0 in range(0, len(blocks), ATTN_BATCH[pi])]
    pending = logits(*work[0])
    for idx, (pi, grp) in enumerate(work):
        s, v = pending
        if idx + 1 < len(work):
            pending = logits(*work[idx + 1])
        finish(pi, grp, s, v)

    for i in range(PLANES):
        l0, l1, l2 = lbuf[0, i], lbuf[1, i], lbuf[2, i]
        mx = jnp.maximum(jnp.maximum(l0, l1), l2)
        e0, e1, e2 = jnp.exp2(l0 - mx), jnp.exp2(l1 - mx), jnp.exp2(l2 - mx)
        o = (e0 * obuf[0, i] + e1 * obuf[1, i] + e2 * obuf[2, i]) / (e0 + e1 + e2)
        ga = ga_ref[i]
        o_ref[i] = (o * (ga * jax.nn.sigmoid(ga))).astype(o_ref.dtype)


def _attention(p1, bias_tabs, b, s):
    nrow = s // PLANES
    p1 = p1.reshape(b, PLANES, nrow, P1_WIDTH)
    per = ATTN_ROWS // HALO_ROWS
    n_halo = nrow // HALO_ROWS

    def tile(col):
        return pl.BlockSpec((None, PLANES, ATTN_ROWS, HEAD_DIM), lambda bb, h, t: (bb, 0, t, col // HEAD_DIM + h))

    def prev(col):
        return pl.BlockSpec((None, PLANES, HALO_ROWS, HEAD_DIM),
                            lambda bb, h, t: (bb, 0, jnp.maximum(t * per - 1, 0), col // HEAD_DIM + h))

    def nxt(col):
        return pl.BlockSpec((None, PLANES, HALO_ROWS, HEAD_DIM),
                            lambda bb, h, t: (bb, 0, jnp.minimum((t + 1) * per, n_halo - 1), col // HEAD_DIM + h))

    def table(tab):
        return pl.BlockSpec((None,) + tab.shape[1:], lambda bb, h, t: (h, 0, 0, 0))

    part_buf = pltpu.VMEM((len(ATTN_GEOM), PLANES, ATTN_ROWS, HEAD_DIM), F32)
    return pl.pallas_call(
        _attn_kernel,
        grid=(b, N_HEADS, nrow // ATTN_ROWS),
        in_specs=[tile(COL_Q), prev(COL_K), tile(COL_K), nxt(COL_K), prev(COL_V), tile(COL_V), nxt(COL_V),
                  tile(COL_GA)] + [table(tab) for tab in bias_tabs],
        out_specs=pl.BlockSpec((None, PLANES, ATTN_ROWS, HEAD_DIM), lambda bb, h, t: (bb, 0, t, h)),
        out_shape=jax.ShapeDtypeStruct((b, PLANES, nrow, ATTN_WIDTH), BF16),
        scratch_shapes=[part_buf, part_buf],
        compiler_params=_params("parallel", "parallel", "parallel"),
        name="attention",
    )(p1, p1, p1, p1, p1, p1, p1, p1, *bias_tabs)


def _ssm_prep_kernel(arow_ref, acol_ref, c_ref, bw_ref, wz_ref, t0_ref, cc_ref, *, gc):
    nb, hc, p = SSM_BLOCK, SSM_GROUP, SSM_STATE
    blk_of_lane = lax.broadcasted_iota(jnp.int32, (hc, SSM_ROWS), 1) // hc
    blk_of_lane_p = lax.broadcasted_iota(jnp.int32, (p, SSM_ROWS), 1) // hc

    def body(gi, carry):
        lag_kernels = []
        for dirn in range(2):
            ar = arow_ref[gi, 2 * dirn:2 * dirn + 1, :]
            ai = arow_ref[gi, 2 * dirn + 1:2 * dirn + 2, :]
            cr = c_ref[gi, 2 * dirn]
            ci = c_ref[gi, 2 * dirn + 1]
            pr = jnp.ones((1, p), F32)
            pi = jnp.zeros((1, p), F32)
            ca_re, ca_im = [], []
            for _ in range(nb + 1):
                ca_re.append(cr * pr - ci * pi)
                ca_im.append(cr * pi + ci * pr)
                pr, pi = pr * ar - pi * ai, pr * ai + pi * ar
            bre = bw_ref[gi, 2 * dirn]
            bim = bw_ref[gi, 2 * dirn + 1]
            lag_kernels.append(
                jnp.dot(jnp.concatenate(ca_re[:nb], axis=0).astype(BF16), bre.astype(BF16),
                        preferred_element_type=F32)
                - jnp.dot(jnp.concatenate(ca_im[:nb], axis=0).astype(BF16), bim.astype(BF16),
                          preferred_element_type=F32))
            order = range(1, nb + 1) if dirn == 0 else range(nb, 0, -1)
            c0 = 2 * p * dirn
            cc_ref[gi, :, c0:c0 + p] = jnp.concatenate([ca_re[k] for k in order], axis=0).astype(BF16)
            cc_ref[gi, :, c0 + p:c0 + 2 * p] = (-jnp.concatenate([ca_im[k] for k in order], axis=0)).astype(BF16)
            acr = acol_ref[gi, c0:c0 + p, :]
            aci = acol_ref[gi, c0 + p:c0 + 2 * p, :]
            qr = jnp.ones((p, 1), F32)
            qi = jnp.zeros((p, 1), F32)
            powers = []
            for _ in range(nb):
                powers.append((qr, qi))
                qr, qi = qr * acr - qi * aci, qr * aci + qi * acr
            per = jnp.zeros((p, SSM_ROWS), F32)
            pei = jnp.zeros((p, SSM_ROWS), F32)
            for i in range(nb):
                e = nb - 1 - i if dirn == 0 else i
                per = jnp.where(blk_of_lane_p == i, powers[e][0], per)
                pei = jnp.where(blk_of_lane_p == i, powers[e][1], pei)
            wz_ref[gi, c0:c0 + p, :] = (per * bre - pei * bim).astype(BF16)
            wz_ref[gi, c0 + p:c0 + 2 * p, :] = (per * bim + pei * bre).astype(BF16)
        kf, kb = lag_kernels
        diag = kf[0:hc, :] + kb[0:hc, :]
        for j in range(nb):
            acc = jnp.zeros((hc, SSM_ROWS), F32)
            for i in range(nb):
                if i < j:
                    src = kf[hc * (j - i):hc * (j - i + 1), :]
                elif i > j:
                    src = kb[hc * (i - j):hc * (i - j + 1), :]
                else:
                    src = diag
                acc = jnp.where(blk_of_lane == i, src, acc)
            t0_ref[gi, hc * j:hc * (j + 1), :] = acc.astype(BF16)
        return carry

    lax.fori_loop(0, gc, body, 0)


def _ssm_weights(lam_re, lam_im, log_dt, b_re, b_im, c_re, c_im, gc=8):
    g, p, hc = SSM_GROUPS, SSM_STATE, SSM_GROUP
    dt = jnp.exp(log_dt.astype(F32))[..., None]
    lr = lam_re.astype(F32)
    li = lam_im.astype(F32)
    mag = jnp.exp(lr * dt)
    ab_re = mag * jnp.cos(li * dt)
    ab_im = mag * jnp.sin(li * dt)
    den = lr * lr + li * li
    f_re = ((ab_re - 1.0) * lr + ab_im * li) / den
    f_im = (ab_im * lr - (ab_re - 1.0) * li) / den
    br = b_re.astype(F32)
    bi = b_im.astype(F32)
    bb_re = f_re[..., None] * br - f_im[..., None] * bi
    bb_im = f_re[..., None] * bi + f_im[..., None] * br
    mag16 = jnp.exp(SSM_BLOCK * (lr * dt))
    a16_re = mag16 * jnp.cos(SSM_BLOCK * (li * dt))
    a16_im = mag16 * jnp.sin(SSM_BLOCK * (li * dt))

    def pack(re, im):
        return jnp.stack([re[0], im[0], re[1], im[1]], axis=1)

    arow = pack(ab_re, ab_im)
    acol = arow.reshape(g, 4 * p, 1)
    a16 = pack(a16_re, a16_im).reshape(g, 4 * p, 1)
    cmat = pack(c_re.astype(F32), c_im.astype(F32))
    bw = jnp.tile(pack(bb_re, bb_im), (1, 1, 1, SSM_BLOCK))
    wspec = pl.BlockSpec((gc, SSM_ROWS, SSM_ROWS), lambda i: (i, 0, 0))
    wshape = jax.ShapeDtypeStruct((g, SSM_ROWS, SSM_ROWS), BF16)
    wz, t0, cc = pl.pallas_call(
        functools.partial(_ssm_prep_kernel, gc=gc),
        grid=(g // gc,),
        in_specs=[pl.BlockSpec((gc, 4, p), lambda i: (i, 0, 0)),
                  pl.BlockSpec((gc, 4 * p, 1), lambda i: (i, 0, 0)),
                  pl.BlockSpec((gc, 4, hc, p), lambda i: (i, 0, 0, 0)),
                  pl.BlockSpec((gc, 4, p, SSM_ROWS), lambda i: (i, 0, 0, 0))],
        out_specs=[wspec, wspec, wspec],
        out_shape=[wshape, wshape, wshape],
        compiler_params=_params("parallel"),
        name="ssm_prep",
    )(arow, acol, cmat, bw)
    return wz, t0, cc, a16


def _gelu_tanh(x):
    return 0.5 * x * (1.0 + jnp.tanh(math.sqrt(2.0 / math.pi) * (x + 0.044715 * (x * x * x))))


def _cmul(ar, ai, br, bi):
    return ar * br - ai * bi, ar * bi + ai * br


def _block_states(zs, coefs, nc):
    p = SSM_STATE
    ng = len(zs)
    rows = ng * p
    lane = lax.broadcasted_iota(jnp.int32, (rows, LANES), 1)
    zero = jnp.zeros((rows, LANES), F32)

    def part(arrs, r0, c=None):
        cols = slice(None) if c is None else slice(LANES * c, LANES * (c + 1))
        return jnp.concatenate([x[r0:r0 + p, cols] for x in arrs], axis=0)

    halves = []
    for dirn in range(2):
        base = 2 * p * dirn
        zr = [part(zs, base, c) for c in range(nc)]
        zi = [part(zs, base + p, c) for c in range(nc)]
        ar = jnp.broadcast_to(part(coefs, base), (rows, LANES))
        ai = jnp.broadcast_to(part(coefs, base + p), (rows, LANES))
        order = list(range(nc)) if dirn == 0 else list(range(nc - 1, -1, -1))
        ir, ii = {order[0]: zr[order[0]]}, {order[0]: zi[order[0]]}
        for prev, c in zip(order[:-1], order[1:]):
            tr, ti = _cmul(ar, ai, ir[prev], ii[prev])
            ir[c], ii[c] = zr[c] + tr, zi[c] + ti
        pows = [(ar, ai)]
        for _ in range(nc - 1):
            pows.append(_cmul(pows[-1][0], pows[-1][1], ar, ai))
        gr, gi = ir[order[-1]], ii[order[-1]]
        br, bi = pows[nc - 1]
        s = 1
        while s < LANES:
            keep = (lane >= s) if dirn == 0 else (lane < LANES - s)
            amt = s if dirn == 0 else LANES - s
            rr = jnp.where(keep, pltpu.roll(gr, amt, 1), zero)
            ri = jnp.where(keep, pltpu.roll(gi, amt, 1), zero)
            tr, ti = _cmul(br, bi, rr, ri)
            gr, gi = gr + tr, gi + ti
            br, bi = _cmul(br, bi, br, bi)
            s *= 2
        keep = (lane >= 1) if dirn == 0 else (lane < LANES - 1)
        amt = 1 if dirn == 0 else LANES - 1
        er = jnp.where(keep, pltpu.roll(gr, amt, 1), zero)
        ei = jnp.where(keep, pltpu.roll(gi, amt, 1), zero)
        hr, hi = [None] * nc, [None] * nc
        hr[order[0]], hi[order[0]] = er, ei
        for k, (prev, c) in enumerate(zip(order[:-1], order[1:])):
            tr, ti = _cmul(pows[k][0], pows[k][1], er, ei)
            hr[c], hi[c] = ir[prev] + tr, ii[prev] + ti
        halves += [jnp.concatenate(hr, axis=1), jnp.concatenate(hi, axis=1)]
    return [jnp.concatenate([h[g * p:(g + 1) * p] for h in halves], axis=0) for g in range(ng)]


def _ssm_kernel(u_ref, wz_ref, t0_ref, cc_ref, a_ref, d_ref, perm_ref, unperm_ref, y_ref, *, gc, ntau):
    nc = ntau // LANES
    rows = [slice(SSM_GROUP * gi, SSM_GROUP * (gi + 1)) for gi in range(gc)]
    zs = []
    for gi in range(gc):
        x = u_ref[:, rows[gi], :].reshape(SSM_ROWS, ntau).astype(BF16)
        xp = jnp.dot(x, perm_ref[...], preferred_element_type=F32).astype(BF16)
        zs.append(jnp.dot(wz_ref[gi], xp, preferred_element_type=F32))
    states = _block_states(zs, [a_ref[gi] for gi in range(gc)], nc)
    for gi in range(gc):
        xf = u_ref[:, rows[gi], :].reshape(SSM_ROWS, ntau)
        h = jnp.dot(states[gi].astype(BF16), unperm_ref[...], preferred_element_type=F32).astype(BF16)
        y = (jnp.dot(t0_ref[gi], xf.astype(BF16), preferred_element_type=F32)
             + jnp.dot(cc_ref[gi], h, preferred_element_type=F32))
        y = _gelu_tanh(y + d_ref[gi] * xf)
        y_ref[:, rows[gi], :] = y.astype(y_ref.dtype).reshape(SSM_BLOCK, SSM_GROUP, ntau)


def _ssm(u_t, wz, t0, cc, a16, d_rows, gc=8):
    b, _, _, ntau = u_t.shape
    nc = ntau // LANES
    j = jnp.arange(ntau)
    src = (j % LANES) * nc + j // LANES
    perm = (jnp.arange(ntau)[:, None] == src[None, :]).astype(BF16)
    wspec = pl.BlockSpec((gc, SSM_ROWS, SSM_ROWS), lambda bb, g: (g, 0, 0))
    vspec = pl.BlockSpec((gc, SSM_ROWS, 1), lambda bb, g: (g, 0, 0))
    uspec = pl.BlockSpec((None, SSM_BLOCK, SSM_GROUP * gc, ntau), lambda bb, g: (bb, 0, g, 0))
    pspec = pl.BlockSpec((ntau, ntau), lambda bb, g: (0, 0))
    return pl.pallas_call(
        functools.partial(_ssm_kernel, gc=gc, ntau=ntau),
        grid=(b, SSM_GROUPS // gc),
        in_specs=[uspec, wspec, wspec, wspec, vspec, vspec, pspec, pspec],
        out_specs=uspec,
        out_shape=jax.ShapeDtypeStruct(u_t.shape, BF16),
        compiler_params=_params("parallel", "parallel"),
        name="ssm",
    )(u_t, wz, t0, cc, a16, d_rows, perm, perm.T)


def _column_pipeline(width, chunk, issue, finish):
    chunks = [slice(c, c + chunk) for c in range(0, width, chunk)]
    pending = issue(chunks[0])
    for idx, cols in enumerate(chunks):
        current = pending
        if idx + 1 < len(chunks):
            pending = issue(chunks[idx + 1])
        finish(cols, current)


def _glu_kernel(y_ref, w1_ref, w2_ref, gb_ref, o_ref):
    pp, tmt, _ = o_ref.shape
    yt = jnp.concatenate([y_ref[p].T for p in range(pp)], axis=0)

    def issue(cols):
        return (jnp.dot(yt, w1_ref[:, cols], preferred_element_type=F32),
                jnp.dot(yt, w2_ref[:, cols], preferred_element_type=F32))

    def finish(cols, gl):
        gb = gb_ref[:, :, cols].reshape(pp * tmt, -1)
        out = (gl[0] * jax.nn.sigmoid(gl[1]) * (gb * jax.nn.sigmoid(gb))).astype(o_ref.dtype)
        o_ref[:, :, cols] = out.reshape(pp, tmt, -1)

    _column_pipeline(o_ref.shape[2], GLU_CHUNK, issue, finish)


def _glu(y_t, w_glu, p2, tokens=512, tn=2048):
    b, _, _, ntau = y_t.shape
    tmt = min(tokens, ntau)
    pp = tokens // tmt
    nj = SSM_WIDTH // tn
    p2v = p2.reshape(b, SSM_BLOCK, ntau, P2_WIDTH)
    out = pl.pallas_call(
        _glu_kernel,
        grid=(nj, b, SSM_BLOCK // pp, ntau // tmt),
        in_specs=[
            pl.BlockSpec((None, pp, SSM_WIDTH, tmt), lambda j, bb, i, tt: (bb, i, 0, tt)),
            pl.BlockSpec((SSM_WIDTH, tn), lambda j, bb, i, tt: (0, j)),
            pl.BlockSpec((SSM_WIDTH, tn), lambda j, bb, i, tt: (0, nj + j)),
            pl.BlockSpec((None, pp, tmt, tn), lambda j, bb, i, tt: (bb, i, tt, j)),
        ],
        out_specs=pl.BlockSpec((None, pp, tmt, tn), lambda j, bb, i, tt: (bb, i, tt, j)),
        out_shape=jax.ShapeDtypeStruct((b, SSM_BLOCK, ntau, SSM_WIDTH), BF16),
        compiler_params=_params("parallel", "parallel", "parallel", "parallel"),
        name="glu",
    )(y_t, w_glu, w_glu, p2v)
    return out.reshape(b * SSM_BLOCK * ntau, SSM_WIDTH)


def _merge_kernel(a_ref, y_ref, wa_ref, wb_ref, ma_ref, mb_ref, unperm_ref, o_ref):
    rows = o_ref.shape[0]
    a = a_ref[...].reshape(rows, ATTN_WIDTH)
    y = y_ref[...].reshape(rows, SSM_WIDTH)
    sub = unperm_ref.shape[0]
    n = sub // SSM_BLOCK

    def issue(cols):
        return (jnp.dot(a, wa_ref[:, cols], preferred_element_type=F32),
                jnp.dot(y, wb_ref[:, cols], preferred_element_type=F32))

    def finish(cols, z):
        ma = ma_ref[:, :, cols].reshape(rows, -1)
        mb = mb_ref[:, :, cols].reshape(rows, -1)
        merged = (jax.nn.sigmoid(ma) * z[0] + jax.nn.sigmoid(mb) * z[1]).astype(BF16)
        merged = merged.reshape(SSM_BLOCK, rows // SSM_BLOCK, -1)
        for k in range(rows // sub):
            part = merged[:, k * n:(k + 1) * n, :].reshape(sub, -1)
            o_ref[k * sub:(k + 1) * sub, cols] = jnp.dot(unperm_ref[...], part,
                                                         preferred_element_type=F32).astype(o_ref.dtype)

    _column_pipeline(o_ref.shape[1], EPILOGUE_CHUNK, issue, finish)


def _step_permutation(tm):
    n = tm // SSM_BLOCK
    r = jnp.arange(tm)
    src = SSM_BLOCK * (r % n) + r // n
    return (src[:, None] == jnp.arange(tm)[None, :]).astype(BF16)


def _merge(a, y2, w_attn_proj, w_ssm_proj, p2, b, s, tm=512, tn=1024):
    ntau = s // SSM_BLOCK
    n = tm // SSM_BLOCK
    ma0 = (COL_MA - COL_GB) // tn
    mb0 = (COL_MB - COL_GB) // tn

    def steps(arr, width):
        return arr.reshape(b, SSM_BLOCK, ntau, width)

    return pl.pallas_call(
        _merge_kernel,
        grid=(D_MODEL // tn, b, s // tm),
        in_specs=[
            pl.BlockSpec((None, SSM_BLOCK, n, ATTN_WIDTH), lambda j, bb, t: (bb, 0, t, 0)),
            pl.BlockSpec((None, SSM_BLOCK, n, SSM_WIDTH), lambda j, bb, t: (bb, 0, t, 0)),
            pl.BlockSpec((ATTN_WIDTH, tn), lambda j, bb, t: (0, j)),
            pl.BlockSpec((SSM_WIDTH, tn), lambda j, bb, t: (0, j)),
            pl.BlockSpec((None, SSM_BLOCK, n, tn), lambda j, bb, t: (bb, 0, t, ma0 + j)),
            pl.BlockSpec((None, SSM_BLOCK, n, tn), lambda j, bb, t: (bb, 0, t, mb0 + j)),
            pl.BlockSpec((PERM_TILE, PERM_TILE), lambda j, bb, t: (0, 0)),
        ],
        out_specs=pl.BlockSpec((tm, tn), lambda j, bb, t: (bb * (s // tm) + t, j)),
        out_shape=jax.ShapeDtypeStruct((b * s, D_MODEL), BF16),
        compiler_params=_params("parallel", "parallel", "parallel"),
        name="merge",
    )(steps(a, ATTN_WIDTH), steps(y2, SSM_WIDTH), w_attn_proj, w_ssm_proj,
      steps(p2, P2_WIDTH), steps(p2, P2_WIDTH), _step_permutation(PERM_TILE).T)


def _out_kernel(m_ref, w_ref, x_ref, o_ref):
    o_ref[...] = x_ref[...] + jnp.dot(m_ref[...], w_ref[...], preferred_element_type=F32)


def _out_proj(merged, w_out, x2, tm=1024, tn=1024):
    t = merged.shape[0]
    return pl.pallas_call(
        _out_kernel,
        grid=(D_MODEL // tn, t // tm),
        in_specs=[
            pl.BlockSpec((tm, D_MODEL), lambda j, i: (i, 0)),
            pl.BlockSpec((D_MODEL, tn), lambda j, i: (0, j)),
            pl.BlockSpec((tm, tn), lambda j, i: (i, j)),
        ],
        out_specs=pl.BlockSpec((tm, tn), lambda j, i: (i, j)),
        out_shape=jax.ShapeDtypeStruct((t, D_MODEL), F32),
        compiler_params=_params("parallel", "parallel"),
        name="out_proj",
    )(merged, w_out, x2)


def _layer(x, w, late):
    b, s, d = x.shape
    h_steps = _prenorm(x, w["norm_gain"])
    h2 = h_steps.reshape(b * s, d)
    side = late if late[0].dtype != BF16 else ()
    p1, rounded = _proj(h2, w["w_in"], COL_Q, P1_WIDTH, w["qk_gain_row"], 2 * ATTN_WIDTH, "proj_attn", side)
    w_glu, w_attn_proj, w_ssm_proj, w_out = rounded if side else late
    p2, _ = _proj(h2, w["w_in"], COL_GB, P2_WIDTH, None, 0, "proj_gates")
    u_t = _proj_u(h_steps, w["wu_t"])
    a = _attention(p1, w["bias_tabs"], b, s).reshape(b * s, ATTN_WIDTH)
    y_t = _ssm(u_t, w["wz"], w["t0"], w["cc"], w["a16"], w["d_rows"])
    y2 = _glu(y_t, w_glu, p2)
    merged = _merge(a, y2, w_attn_proj, w_ssm_proj, p2, b, s)
    out = _out_proj(merged, w_out, x.reshape(b * s, d)).reshape(b, s, d)
    return out, (w_glu, w_attn_proj, w_ssm_proj, w_out)


def _prepare(rel_bias, norm_gain, w_in, q_gain, k_gain, lam_re, lam_im, log_dt, b_re, b_im, c_re, c_im, d_skip):
    wz, t0, cc, a16 = _ssm_weights(lam_re, lam_im, log_dt, b_re, b_im, c_re, c_im)
    gains = jnp.concatenate([jnp.tile(q_gain.astype(F32) * (HEAD_DIM ** -0.5 * LOG2E), N_HEADS),
                             jnp.tile(k_gain.astype(F32), N_HEADS),
                             jnp.zeros((P1_WIDTH - 2 * ATTN_WIDTH,), F32)])
    d_rows = jnp.tile(d_skip.astype(F32).reshape(SSM_GROUPS, 1, SSM_GROUP), (1, SSM_BLOCK, 1))
    return {
        "norm_gain": norm_gain.astype(F32),
        "w_in": w_in.astype(BF16),
        "wu_t": w_in[:, COL_U:COL_U + SSM_WIDTH].T.astype(BF16),
        "qk_gain_row": gains.reshape(1, P1_WIDTH),
        "bias_tabs": _bias_tables(rel_bias),
        "wz": wz, "t0": t0, "cc": cc, "a16": a16,
        "d_rows": d_rows.reshape(SSM_GROUPS, SSM_ROWS, 1),
    }


def kernel(x_prompt, x_sample, rel_bias, norm_gain, w_in, q_gain, k_gain, lam_re, lam_im, log_dt, b_re, b_im,
           c_re, c_im, d_skip, w_glu, w_attn_proj, w_ssm_proj, w_out):
    y_prompt, y_sample = x_prompt, x_sample
    for l in range(norm_gain.shape[0]):
        w = _prepare(rel_bias, norm_gain[l], w_in[l], q_gain[l], k_gain[l], lam_re[l], lam_im[l], log_dt[l],
                     b_re[l], b_im[l], c_re[l], c_im[l], d_skip[l])
        late = tuple(m[l].astype(F32) for m in (w_glu, w_attn_proj, w_ssm_proj, w_out))
        y_prompt, late = _layer(y_prompt, w, late)
        y_sample, _ = _layer(y_sample, w, late)
    return (y_prompt, y_sample)
```

```python
import functools
import math

import jax
import jax.numpy as jnp
from jax import lax
from jax.experimental import pallas as pl
from jax.experimental.pallas import tpu as pltpu

F32 = jnp.float32
BF16 = jnp.bfloat16

D_MODEL = 4096
N_HEADS = 16
HEAD_DIM = 128
ATTN_WIDTH = N_HEADS * HEAD_DIM
SIDE_KEYS = 64
N_BUCKETS = 32
REL_MAX_DIST = 1024
NEG = -1e30
LOG2E = math.log2(math.e)
EPS = 1e-6
SSM_WIDTH = 2048
SSM_GROUP = 16
SSM_GROUPS = SSM_WIDTH // SSM_GROUP
SSM_STATE = 64
SSM_BLOCK = 16
SSM_ROWS = SSM_BLOCK * SSM_GROUP

COL_Q = 0
COL_K = ATTN_WIDTH
COL_V = 2 * ATTN_WIDTH
COL_GA = 3 * ATTN_WIDTH
COL_U = 4 * ATTN_WIDTH
COL_GB = COL_U + SSM_WIDTH
COL_MA = COL_GB + SSM_WIDTH
COL_MB = COL_MA + D_MODEL
P1_WIDTH = COL_U
P2_WIDTH = SSM_WIDTH + 2 * D_MODEL

LANES = 128
PLANES = SSM_BLOCK
PERM_TILE = 256
EPILOGUE_CHUNK = 512
GLU_CHUNK = 256
ATTN_GEOM = ((1, 8, 16), (4, 32, 64), (16, 128, 256))
ATTN_BATCH = (4, 4, 4)
ATTN_ROWS = 256
HALO_ROWS = SIDE_KEYS

VMEM_BYTES_V7X = 64 * 1024 * 1024
VMEM_LIMIT = VMEM_BYTES_V7X - 4 * 1024 * 1024


def _params(*sem):
    return pltpu.CompilerParams(dimension_semantics=sem, vmem_limit_bytes=VMEM_LIMIT)


def _prenorm_kernel(x_ref, g_ref, perm_ref, o_ref):
    x = x_ref[...]
    ms = jnp.mean(x * x, axis=-1, keepdims=True)
    h = (x * lax.rsqrt(ms + EPS) * g_ref[...]).astype(BF16)
    hp = jnp.dot(perm_ref[...], h, preferred_element_type=F32).astype(BF16)
    o_ref[...] = hp.reshape(o_ref.shape)


def _prenorm(x, gain, tm=PERM_TILE):
    b, s, d = x.shape
    return pl.pallas_call(
        _prenorm_kernel,
        grid=(b, s // tm),
        in_specs=[pl.BlockSpec((None, tm, d), lambda bb, t: (bb, t, 0)),
                  pl.BlockSpec((1, d), lambda bb, t: (0, 0)),
                  pl.BlockSpec((tm, tm), lambda bb, t: (0, 0))],
        out_specs=pl.BlockSpec((None, SSM_BLOCK, tm // SSM_BLOCK, d), lambda bb, t: (bb, 0, t, 0)),
        out_shape=jax.ShapeDtypeStruct((b, SSM_BLOCK, s // SSM_BLOCK, d), BF16),
        compiler_params=_params("parallel", "parallel"),
        name="prenorm",
    )(x, gain.reshape(1, d), _step_permutation(tm))


def _proj_kernel(h_ref, w_ref, g_ref, *refs, n_norm_tiles, n_side):
    side_in, o_ref, side_out = refs[:n_side], refs[n_side], refs[n_side + 1:]
    j = pl.program_id(0)

    for src, dst in zip(side_in, side_out):
        dst[...] = src[...].astype(BF16)

    @pl.when(j < n_norm_tiles)
    def _():
        def issue(cols):
            return jnp.dot(h_ref[...], w_ref[:, cols], preferred_element_type=F32)

        def finish(cols, acc):
            for hh in range(acc.shape[1] // HEAD_DIM):
                blk = acc[:, hh * HEAD_DIM:(hh + 1) * HEAD_DIM]
                sl = slice(cols.start + hh * HEAD_DIM, cols.start + (hh + 1) * HEAD_DIM)
                ms = jnp.mean(blk * blk, axis=-1, keepdims=True)
                o_ref[:, sl] = blk * lax.rsqrt(ms + EPS) * g_ref[:, sl]

        _column_pipeline(o_ref.shape[1], EPILOGUE_CHUNK, issue, finish)

    @pl.when(j >= n_norm_tiles)
    def _():
        o_ref[...] = jnp.dot(h_ref[...], w_ref[...], preferred_element_type=F32)


def _proj(h, w, col0, width, gain_row, n_norm_cols, name, side=(), tm=512, tn=2048):
    t, k = h.shape
    if gain_row is None:
        gain_row = jnp.zeros((1, width), F32)
    nm = t // tm
    nsteps = (width // tn) * nm

    def side_spec(mat):
        return pl.BlockSpec((mat.shape[0] // nsteps, mat.shape[1]), lambda j, i: (j * nm + i, 0))

    out = pl.pallas_call(
        functools.partial(_proj_kernel, n_norm_tiles=n_norm_cols // tn, n_side=len(side)),
        grid=(width // tn, nm),
        in_specs=[
            pl.BlockSpec((tm, k), lambda j, i: (i, 0)),
            pl.BlockSpec((k, tn), lambda j, i: (0, col0 // tn + j)),
            pl.BlockSpec((1, tn), lambda j, i: (0, j)),
        ] + [side_spec(mat) for mat in side],
        out_specs=[pl.BlockSpec((tm, tn), lambda j, i: (i, j))] + [side_spec(mat) for mat in side],
        out_shape=[jax.ShapeDtypeStruct((t, width), F32)]
        + [jax.ShapeDtypeStruct(mat.shape, BF16) for mat in side],
        compiler_params=_params("parallel", "parallel"),
        name=name,
    )(h, w, gain_row, *side)
    return out[0], tuple(out[1:])


def _proj_u_kernel(w_ref, h_ref, o_ref):
    o_ref[...] = lax.dot_general(w_ref[...], h_ref[...], (((1,), (1,)), ((), ())),
                                 preferred_element_type=F32)


def _proj_u(h_steps, wu_t, tokens=512):
    b, _, ntau, _ = h_steps.shape
    tmt = min(tokens, ntau)
    return pl.pallas_call(
        _proj_u_kernel,
        grid=(b, SSM_BLOCK, ntau // tmt),
        in_specs=[
            pl.BlockSpec((SSM_WIDTH, D_MODEL), lambda bb, i, tt: (0, 0)),
            pl.BlockSpec((None, None, tmt, D_MODEL), lambda bb, i, tt: (bb, i, tt, 0)),
        ],
        out_specs=pl.BlockSpec((None, None, SSM_WIDTH, tmt), lambda bb, i, tt: (bb, i, 0, tt)),
        out_shape=jax.ShapeDtypeStruct((b, SSM_BLOCK, SSM_WIDTH, ntau), F32),
        compiler_params=_params("parallel", "parallel", "parallel"),
        name="proj_u",
    )(wu_t, h_steps)


def _rel_bucket(rel):
    half = N_BUCKETS // 2
    exact = half // 2
    base = jnp.where(rel > 0, half, 0)
    n = jnp.abs(rel)
    nf = jnp.maximum(n, 1).astype(F32)
    large = exact + (jnp.log(nf / exact) / math.log(REL_MAX_DIST / exact) * (half - exact)).astype(jnp.int32)
    large = jnp.minimum(large, half - 1)
    return base + jnp.where(n < exact, n, large)


def _bias_tables(rel_bias):
    buckets = jnp.arange(N_BUCKETS)
    tabs = []
    for d, nq, nk in ATTN_GEOM:
        g = PLANES // d
        koff = (nk - nq) // 2
        aq = jnp.arange(g)[:, None, None, None]
        tq = jnp.arange(nq)[None, :, None, None]
        ak = jnp.arange(g)[None, None, :, None]
        tk = jnp.arange(nk)[None, None, None, :]
        rel = (g * ((tk - koff) - tq) + (ak - aq)).reshape(g * nq, g * nk)
        onehot = (_rel_bucket(rel * d)[..., None] == buckets).astype(F32)
        bias = jnp.einsum('qkb,bh->hqk', onehot, rel_bias.astype(F32), precision=lax.Precision.HIGHEST)
        bias = jnp.where((jnp.abs(rel) <= SIDE_KEYS)[None], bias, NEG)
        trow = jnp.arange(g * nk) % nk - koff
        before = (trow < 0)[None, None, :]
        after = (trow >= nq)[None, None, :]
        bias = bias * LOG2E
        tabs.append(jnp.stack([bias, jnp.where(before, NEG, bias), jnp.where(after, NEG, bias)], axis=1))
    return tabs


def _attn_kernel(q_ref, kp_ref, kc_ref, kn_ref, vp_ref, vc_ref, vn_ref, ga_ref, t1_ref, t4_ref, t16_ref, o_ref,
                 nbuf, dbuf, mbuf):
    t = pl.program_id(2)
    first = t == 0
    last = t == pl.num_programs(2) - 1
    k_refs = (kp_ref, kc_ref, kn_ref)
    v_refs = (vp_ref, vc_ref, vn_ref)
    ones = jnp.ones((HEAD_DIM,), BF16)

    def halo_rows(refs, plane, start, n):
        pieces = []
        r0 = -HALO_ROWS
        for ref in refs:
            r1 = r0 + ref.shape[1]
            lo, hi = max(start, r0), min(start + n, r1)
            if lo < hi:
                pieces.append(ref[plane, lo - r0:hi - r0, :])
            r0 = r1
        return pieces

    edge_bias = {}

    def logits(pi, grp):
        d, nq, nk = ATTN_GEOM[pi]
        tab_ref = (t1_ref, t4_ref, t16_ref)[pi]
        g = PLANES // d
        nblk = ATTN_ROWS // nq
        k0 = -((nk - nq) // 2)

        def rows_of(refs, r, start, n):
            pieces = [piece for a in range(g) for piece in halo_rows(refs, r + d * a, start, n)]
            return jnp.concatenate(pieces, axis=0).astype(BF16)

        def bias_of(blk):
            if (pi, blk) not in edge_bias:
                bias = tab_ref[0]
                if blk == 0:
                    bias = jnp.where(first, tab_ref[1], bias)
                if blk == nblk - 1:
                    bias = jnp.minimum(bias, jnp.where(last, tab_ref[2], tab_ref[0]))
                edge_bias[(pi, blk)] = bias
            return edge_bias[(pi, blk)]

        q = jnp.stack([jnp.concatenate([q_ref[r + d * a, blk * nq:(blk + 1) * nq, :] for a in range(g)],
                                       axis=0).astype(BF16) for r, blk in grp])
        k = jnp.stack([rows_of(k_refs, r, blk * nq + k0, nk) for r, blk in grp])
        v = jnp.stack([rows_of(v_refs, r, blk * nq + k0, nk) for r, blk in grp])
        edges = {blk if blk in (0, nblk - 1) else None for _, blk in grp}
        if len(edges) == 1:
            blk = edges.pop()
            bias = (tab_ref[0] if blk is None else bias_of(blk))[None]
        else:
            bias = jnp.stack([bias_of(blk) if blk in (0, nblk - 1) else tab_ref[0] for _, blk in grp])
        return jnp.einsum('bqd,bkd->bqk', q, k, preferred_element_type=F32) + bias, v

    def finish(pi, grp, s, v):
        d, nq, _ = ATTN_GEOM[pi]
        g = PLANES // d
        m = jnp.max(s, axis=-1, keepdims=True)
        p = jnp.exp2(s - m).astype(BF16)
        v1 = jnp.concatenate([v, jnp.broadcast_to(ones, v.shape)], axis=-1)
        acc = jnp.einsum('bqk,bkd->bqd', p, v1, preferred_element_type=F32)
        mb = jnp.broadcast_to(m, acc.shape[:2] + (HEAD_DIM,))
        for bi, (r, blk) in enumerate(grp):
            for a in range(g):
                rows = slice(blk * nq, (blk + 1) * nq)
                nbuf[pi, r + d * a, rows, :] = acc[bi, a * nq:(a + 1) * nq, :HEAD_DIM]
                dbuf[pi, r + d * a, rows, :] = acc[bi, a * nq:(a + 1) * nq, HEAD_DIM:]
                mbuf[pi, r + d * a, rows, :] = mb[bi, a * nq:(a + 1) * nq]

    work = []
    for pi, (d, nq, _) in enumerate(ATTN_GEOM):
        blocks = [(r, blk) for r in range(d) for blk in range(ATTN_ROWS // nq)]
        work += [(pi, blocks[g0:g0 + ATTN_BATCH[pi]]) for g0 in range(0, len(blocks), ATTN_BATCH[pi])]
    pending = logits(*work[0])
    for idx, (pi, grp) in enumerate(work):
        s, v = pending
        if idx + 1 < len(work):
            pending = logits(*work[idx + 1])
        finish(pi, grp, s, v)

    for i in range(PLANES):
        m0, m1, m2 = mbuf[0, i], mbuf[1, i], mbuf[2, i]
        mx = jnp.maximum(jnp.maximum(m0, m1), m2)
        e0, e1, e2 = jnp.exp2(m0 - mx), jnp.exp2(m1 - mx), jnp.exp2(m2 - mx)
        o = ((e0 * nbuf[0, i] + e1 * nbuf[1, i] + e2 * nbuf[2, i])
             / (e0 * dbuf[0, i] + e1 * dbuf[1, i] + e2 * dbuf[2, i]))
        ga = ga_ref[i]
        o_ref[i] = (o * (ga * jax.nn.sigmoid(ga))).astype(o_ref.dtype)


def _attention(p1, bias_tabs, b, s):
    nrow = s // PLANES
    p1 = p1.reshape(b, PLANES, nrow, P1_WIDTH)
    per = ATTN_ROWS // HALO_ROWS
    n_halo = nrow // HALO_ROWS

    def tile(col):
        return pl.BlockSpec((None, PLANES, ATTN_ROWS, HEAD_DIM), lambda bb, h, t: (bb, 0, t, col // HEAD_DIM + h))

    def prev(col):
        return pl.BlockSpec((None, PLANES, HALO_ROWS, HEAD_DIM),
                            lambda bb, h, t: (bb, 0, jnp.maximum(t * per - 1, 0), col // HEAD_DIM + h))

    def nxt(col):
        return pl.BlockSpec((None, PLANES, HALO_ROWS, HEAD_DIM),
                            lambda bb, h, t: (bb, 0, jnp.minimum((t + 1) * per, n_halo - 1), col // HEAD_DIM + h))

    def table(tab):
        return pl.BlockSpec((None,) + tab.shape[1:], lambda bb, h, t: (h, 0, 0, 0))

    part_buf = pltpu.VMEM((len(ATTN_GEOM), PLANES, ATTN_ROWS, HEAD_DIM), F32)
    return pl.pallas_call(
        _attn_kernel,
        grid=(b, N_HEADS, nrow // ATTN_ROWS),
        in_specs=[tile(COL_Q), prev(COL_K), tile(COL_K), nxt(COL_K), prev(COL_V), tile(COL_V), nxt(COL_V),
                  tile(COL_GA)] + [table(tab) for tab in bias_tabs],
        out_specs=pl.BlockSpec((None, PLANES, ATTN_ROWS, HEAD_DIM), lambda bb, h, t: (bb, 0, t, h)),
        out_shape=jax.ShapeDtypeStruct((b, PLANES, nrow, ATTN_WIDTH), BF16),
        scratch_shapes=[part_buf, part_buf, part_buf],
        compiler_params=_params("parallel", "parallel", "parallel"),
        name="attention",
    )(p1, p1, p1, p1, p1, p1, p1, p1, *bias_tabs)


def _ssm_prep_kernel(arow_ref, acol_ref, c_ref, bw_ref, wz_ref, t0_ref, cc_ref, *, gc):
    nb, hc, p = SSM_BLOCK, SSM_GROUP, SSM_STATE
    blk_of_lane = lax.broadcasted_iota(jnp.int32, (hc, SSM_ROWS), 1) // hc
    blk_of_lane_p = lax.broadcasted_iota(jnp.int32, (p, SSM_ROWS), 1) // hc

    def body(gi, carry):
        lag_kernels = []
        for dirn in range(2):
            ar = arow_ref[gi, 2 * dirn:2 * dirn + 1, :]
            ai = arow_ref[gi, 2 * dirn + 1:2 * dirn + 2, :]
            cr = c_ref[gi, 2 * dirn]
            ci = c_ref[gi, 2 * dirn + 1]
            pr = jnp.ones((1, p), F32)
            pi = jnp.zeros((1, p), F32)
            ca_re, ca_im = [], []
            for _ in range(nb + 1):
                ca_re.append(cr * pr - ci * pi)
                ca_im.append(cr * pi + ci * pr)
                pr, pi = pr * ar - pi * ai, pr * ai + pi * ar
            bre = bw_ref[gi, 2 * dirn]
            bim = bw_ref[gi, 2 * dirn + 1]
            lag_kernels.append(
                jnp.dot(jnp.concatenate(ca_re[:nb], axis=0).astype(BF16), bre.astype(BF16),
                        preferred_element_type=F32)
                - jnp.dot(jnp.concatenate(ca_im[:nb], axis=0).astype(BF16), bim.astype(BF16),
                          preferred_element_type=F32))
            order = range(1, nb + 1) if dirn == 0 else range(nb, 0, -1)
            c0 = 2 * p * dirn
            cc_ref[gi, :, c0:c0 + p] = jnp.concatenate([ca_re[k] for k in order], axis=0).astype(BF16)
            cc_ref[gi, :, c0 + p:c0 + 2 * p] = (-jnp.concatenate([ca_im[k] for k in order], axis=0)).astype(BF16)
            acr = acol_ref[gi, c0:c0 + p, :]
            aci = acol_ref[gi, c0 + p:c0 + 2 * p, :]
            qr = jnp.ones((p, 1), F32)
            qi = jnp.zeros((p, 1), F32)
            powers = []
            for _ in range(nb):
                powers.append((qr, qi))
                qr, qi = qr * acr - qi * aci, qr * aci + qi * acr
            per = jnp.zeros((p, SSM_ROWS), F32)
            pei = jnp.zeros((p, SSM_ROWS), F32)
            for i in range(nb):
                e = nb - 1 - i if dirn == 0 else i
                per = jnp.where(blk_of_lane_p == i, powers[e][0], per)
                pei = jnp.where(blk_of_lane_p == i, powers[e][1], pei)
            wz_ref[gi, c0:c0 + p, :] = (per * bre - pei * bim).astype(BF16)
            wz_ref[gi, c0 + p:c0 + 2 * p, :] = (per * bim + pei * bre).astype(BF16)
        kf, kb = lag_kernels
        diag = kf[0:hc, :] + kb[0:hc, :]
        for j in range(nb):
            acc = jnp.zeros((hc, SSM_ROWS), F32)
            for i in range(nb):
                if i < j:
                    src = kf[hc * (j - i):hc * (j - i + 1), :]
                elif i > j:
                    src = kb[hc * (i - j):hc * (i - j + 1), :]
                else:
                    src = diag
                acc = jnp.where(blk_of_lane == i, src, acc)
            t0_ref[gi, hc * j:hc * (j + 1), :] = acc.astype(BF16)
        return carry

    lax.fori_loop(0, gc, body, 0)


def _ssm_weights(lam_re, lam_im, log_dt, b_re, b_im, c_re, c_im, gc=8):
    g, p, hc = SSM_GROUPS, SSM_STATE, SSM_GROUP
    dt = jnp.exp(log_dt.astype(F32))[..., None]
    lr = lam_re.astype(F32)
    li = lam_im.astype(F32)
    mag = jnp.exp(lr * dt)
    ab_re = mag * jnp.cos(li * dt)
    ab_im = mag * jnp.sin(li * dt)
    den = lr * lr + li * li
    f_re = ((ab_re - 1.0) * lr + ab_im * li) / den
    f_im = (ab_im * lr - (ab_re - 1.0) * li) / den
    br = b_re.astype(F32)
    bi = b_im.astype(F32)
    bb_re = f_re[..., None] * br - f_im[..., None] * bi
    bb_im = f_re[..., None] * bi + f_im[..., None] * br
    mag16 = jnp.exp(SSM_BLOCK * (lr * dt))
    a16_re = mag16 * jnp.cos(SSM_BLOCK * (li * dt))
    a16_im = mag16 * jnp.sin(SSM_BLOCK * (li * dt))

    def pack(re, im):
        return jnp.stack([re[0], im[0], re[1], im[1]], axis=1)

    arow = pack(ab_re, ab_im)
    acol = arow.reshape(g, 4 * p, 1)
    a16 = pack(a16_re, a16_im).reshape(g, 4 * p, 1)
    cmat = pack(c_re.astype(F32), c_im.astype(F32))
    bw = jnp.tile(pack(bb_re, bb_im), (1, 1, 1, SSM_BLOCK))
    wspec = pl.BlockSpec((gc, SSM_ROWS, SSM_ROWS), lambda i: (i, 0, 0))
    wshape = jax.ShapeDtypeStruct((g, SSM_ROWS, SSM_ROWS), BF16)
    wz, t0, cc = pl.pallas_call(
        functools.partial(_ssm_prep_kernel, gc=gc),
        grid=(g // gc,),
        in_specs=[pl.BlockSpec((gc, 4, p), lambda i: (i, 0, 0)),
                  pl.BlockSpec((gc, 4 * p, 1), lambda i: (i, 0, 0)),
                  pl.BlockSpec((gc, 4, hc, p), lambda i: (i, 0, 0, 0)),
                  pl.BlockSpec((gc, 4, p, SSM_ROWS), lambda i: (i, 0, 0, 0))],
        out_specs=[wspec, wspec, wspec],
        out_shape=[wshape, wshape, wshape],
        compiler_params=_params("parallel"),
        name="ssm_prep",
    )(arow, acol, cmat, bw)
    return wz, t0, cc, a16


def _gelu_tanh(x):
    return 0.5 * x * (1.0 + jnp.tanh(math.sqrt(2.0 / math.pi) * (x + 0.044715 * (x * x * x))))


def _cmul(ar, ai, br, bi):
    return ar * br - ai * bi, ar * bi + ai * br


def _block_states(zs, coefs, nc):
    p = SSM_STATE
    ng = len(zs)
    rows = ng * p
    lane = lax.broadcasted_iota(jnp.int32, (rows, LANES), 1)
    zero = jnp.zeros((rows, LANES), F32)

    def part(arrs, r0, c=None):
        cols = slice(None) if c is None else slice(LANES * c, LANES * (c + 1))
        return jnp.concatenate([x[r0:r0 + p, cols] for x in arrs], axis=0)

    halves = []
    for dirn in range(2):
        base = 2 * p * dirn
        zr = [part(zs, base, c) for c in range(nc)]
        zi = [part(zs, base + p, c) for c in range(nc)]
        ar = jnp.broadcast_to(part(coefs, base), (rows, LANES))
        ai = jnp.broadcast_to(part(coefs, base + p), (rows, LANES))
        order = list(range(nc)) if dirn == 0 else list(range(nc - 1, -1, -1))
        ir, ii = {order[0]: zr[order[0]]}, {order[0]: zi[order[0]]}
        for prev, c in zip(order[:-1], order[1:]):
            tr, ti = _cmul(ar, ai, ir[prev], ii[prev])
            ir[c], ii[c] = zr[c] + tr, zi[c] + ti
        pows = [(ar, ai)]
        for _ in range(nc - 1):
            pows.append(_cmul(pows[-1][0], pows[-1][1], ar, ai))
        gr, gi = ir[order[-1]], ii[order[-1]]
        br, bi = pows[nc - 1]
        s = 1
        while s < LANES:
            keep = (lane >= s) if dirn == 0 else (lane < LANES - s)
            amt = s if dirn == 0 else LANES - s
            rr = jnp.where(keep, pltpu.roll(gr, amt, 1), zero)
            ri = jnp.where(keep, pltpu.roll(gi, amt, 1), zero)
            tr, ti = _cmul(br, bi, rr, ri)
            gr, gi = gr + tr, gi + ti
            br, bi = _cmul(br, bi, br, bi)
            s *= 2
        keep = (lane >= 1) if dirn == 0 else (lane < LANES - 1)
        amt = 1 if dirn == 0 else LANES - 1
        er = jnp.where(keep, pltpu.roll(gr, amt, 1), zero)
        ei = jnp.where(keep, pltpu.roll(gi, amt, 1), zero)
        hr, hi = [None] * nc, [None] * nc
        hr[order[0]], hi[order[0]] = er, ei
        for k, (prev, c) in enumerate(zip(order[:-1], order[1:])):
            tr, ti = _cmul(pows[k][0], pows[k][1], er, ei)
            hr[c], hi[c] = ir[prev] + tr, ii[prev] + ti
        halves += [jnp.concatenate(hr, axis=1), jnp.concatenate(hi, axis=1)]
    return [jnp.concatenate([h[g * p:(g + 1) * p] for h in halves], axis=0) for g in range(ng)]


def _ssm_kernel(u_ref, wz_ref, t0_ref, cc_ref, a_ref, d_ref, perm_ref, unperm_ref, y_ref, *, gc, ntau):
    nc = ntau // LANES
    rows = [slice(SSM_GROUP * gi, SSM_GROUP * (gi + 1)) for gi in range(gc)]
    zs = []
    for gi in range(gc):
        x = u_ref[:, rows[gi], :].reshape(SSM_ROWS, ntau).astype(BF16)
        xp = jnp.dot(x, perm_ref[...], preferred_element_type=F32).astype(BF16)
        zs.append(jnp.dot(wz_ref[gi], xp, preferred_element_type=F32))
    states = _block_states(zs, [a_ref[gi] for gi in range(gc)], nc)
    for gi in range(gc):
        xf = u_ref[:, rows[gi], :].reshape(SSM_ROWS, ntau)
        h = jnp.dot(states[gi].astype(BF16), unperm_ref[...], preferred_element_type=F32).astype(BF16)
        y = (jnp.dot(t0_ref[gi], xf.astype(BF16), preferred_element_type=F32)
             + jnp.dot(cc_ref[gi], h, preferred_element_type=F32))
        y = _gelu_tanh(y + d_ref[gi] * xf)
        y_ref[:, rows[gi], :] = y.astype(y_ref.dtype).reshape(SSM_BLOCK, SSM_GROUP, ntau)


def _ssm(u_t, wz, t0, cc, a16, d_rows, gc=8):
    b, _, _, ntau = u_t.shape
    nc = ntau // LANES
    j = jnp.arange(ntau)
    src = (j % LANES) * nc + j // LANES
    perm = (jnp.arange(ntau)[:, None] == src[None, :]).astype(BF16)
    wspec = pl.BlockSpec((gc, SSM_ROWS, SSM_ROWS), lambda bb, g: (g, 0, 0))
    vspec = pl.BlockSpec((gc, SSM_ROWS, 1), lambda bb, g: (g, 0, 0))
    uspec = pl.BlockSpec((None, SSM_BLOCK, SSM_GROUP * gc, ntau), lambda bb, g: (bb, 0, g, 0))
    pspec = pl.BlockSpec((ntau, ntau), lambda bb, g: (0, 0))
    return pl.pallas_call(
        functools.partial(_ssm_kernel, gc=gc, ntau=ntau),
        grid=(b, SSM_GROUPS // gc),
        in_specs=[uspec, wspec, wspec, wspec, vspec, vspec, pspec, pspec],
        out_specs=uspec,
        out_shape=jax.ShapeDtypeStruct(u_t.shape, BF16),
        compiler_params=_params("parallel", "parallel"),
        name="ssm",
    )(u_t, wz, t0, cc, a16, d_rows, perm, perm.T)


def _column_pipeline(width, chunk, issue, finish):
    chunks = [slice(c, c + chunk) for c in range(0, width, chunk)]
    pending = issue(chunks[0])
    for idx, cols in enumerate(chunks):
        current = pending
        if idx + 1 < len(chunks):
            pending = issue(chunks[idx + 1])
        finish(cols, current)


def _glu_kernel(y_ref, w1_ref, w2_ref, gb_ref, o_ref):
    pp, tmt, _ = o_ref.shape
    yt = jnp.concatenate([y_ref[p].T for p in range(pp)], axis=0)

    def issue(cols):
        return (jnp.dot(yt, w1_ref[:, cols], preferred_element_type=F32),
                jnp.dot(yt, w2_ref[:, cols], preferred_element_type=F32))

    def finish(cols, gl):
        gb = gb_ref[:, :, cols].reshape(pp * tmt, -1)
        out = (gl[0] * jax.nn.sigmoid(gl[1]) * (gb * jax.nn.sigmoid(gb))).astype(o_ref.dtype)
        o_ref[:, :, cols] = out.reshape(pp, tmt, -1)

    _column_pipeline(o_ref.shape[2], GLU_CHUNK, issue, finish)


def _glu(y_t, w_glu, p2, tokens=512, tn=2048):
    b, _, _, ntau = y_t.shape
    tmt = min(tokens, ntau)
    pp = tokens // tmt
    nj = SSM_WIDTH // tn
    p2v = p2.reshape(b, SSM_BLOCK, ntau, P2_WIDTH)
    out = pl.pallas_call(
        _glu_kernel,
        grid=(nj, b, SSM_BLOCK // pp, ntau // tmt),
        in_specs=[
            pl.BlockSpec((None, pp, SSM_WIDTH, tmt), lambda j, bb, i, tt: (bb, i, 0, tt)),
            pl.BlockSpec((SSM_WIDTH, tn), lambda j, bb, i, tt: (0, j)),
            pl.BlockSpec((SSM_WIDTH, tn), lambda j, bb, i, tt: (0, nj + j)),
            pl.BlockSpec((None, pp, tmt, tn), lambda j, bb, i, tt: (bb, i, tt, j)),
        ],
        out_specs=pl.BlockSpec((None, pp, tmt, tn), lambda j, bb, i, tt: (bb, i, tt, j)),
        out_shape=jax.ShapeDtypeStruct((b, SSM_BLOCK, ntau, SSM_WIDTH), BF16),
        compiler_params=_params("parallel", "parallel", "parallel", "parallel"),
        name="glu",
    )(y_t, w_glu, w_glu, p2v)
    return out.reshape(b * SSM_BLOCK * ntau, SSM_WIDTH)


def _merge_kernel(a_ref, y_ref, wa_ref, wb_ref, ma_ref, mb_ref, unperm_ref, o_ref):
    rows = o_ref.shape[0]
    a = a_ref[...].reshape(rows, ATTN_WIDTH)
    y = y_ref[...].reshape(rows, SSM_WIDTH)
    sub = unperm_ref.shape[0]
    n = sub // SSM_BLOCK

    def issue(cols):
        return (jnp.dot(a, wa_ref[:, cols], preferred_element_type=F32),
                jnp.dot(y, wb_ref[:, cols], preferred_element_type=F32))

    def finish(cols, z):
        ma = ma_ref[:, :, cols].reshape(rows, -1)
        mb = mb_ref[:, :, cols].reshape(rows, -1)
        merged = (jax.nn.sigmoid(ma) * z[0] + jax.nn.sigmoid(mb) * z[1]).astype(BF16)
        merged = merged.reshape(SSM_BLOCK, rows // SSM_BLOCK, -1)
        for k in range(rows // sub):
            part = merged[:, k * n:(k + 1) * n, :].reshape(sub, -1)
            o_ref[k * sub:(k + 1) * sub, cols] = jnp.dot(unperm_ref[...], part,
                                                         preferred_element_type=F32).astype(o_ref.dtype)

    _column_pipeline(o_ref.shape[1], EPILOGUE_CHUNK, issue, finish)


def _step_permutation(tm):
    n = tm // SSM_BLOCK
    r = jnp.arange(tm)
    src = SSM_BLOCK * (r % n) + r // n
    return (src[:, None] == jnp.arange(tm)[None, :]).astype(BF16)


def _merge(a, y2, w_attn_proj, w_ssm_proj, p2, b, s, tm=512, tn=1024):
    ntau = s // SSM_BLOCK
    n = tm // SSM_BLOCK
    ma0 = (COL_MA - COL_GB) // tn
    mb0 = (COL_MB - COL_GB) // tn

    def steps(arr, width):
        return arr.reshape(b, SSM_BLOCK, ntau, width)

    return pl.pallas_call(
        _merge_kernel,
        grid=(D_MODEL // tn, b, s // tm),
        in_specs=[
            pl.BlockSpec((None, SSM_BLOCK, n, ATTN_WIDTH), lambda j, bb, t: (bb, 0, t, 0)),
            pl.BlockSpec((None, SSM_BLOCK, n, SSM_WIDTH), lambda j, bb, t: (bb, 0, t, 0)),
            pl.BlockSpec((ATTN_WIDTH, tn), lambda j, bb, t: (0, j)),
            pl.BlockSpec((SSM_WIDTH, tn), lambda j, bb, t: (0, j)),
            pl.BlockSpec((None, SSM_BLOCK, n, tn), lambda j, bb, t: (bb, 0, t, ma0 + j)),
            pl.BlockSpec((None, SSM_BLOCK, n, tn), lambda j, bb, t: (bb, 0, t, mb0 + j)),
            pl.BlockSpec((PERM_TILE, PERM_TILE), lambda j, bb, t: (0, 0)),
        ],
        out_specs=pl.BlockSpec((tm, tn), lambda j, bb, t: (bb * (s // tm) + t, j)),
        out_shape=jax.ShapeDtypeStruct((b * s, D_MODEL), BF16),
        compiler_params=_params("parallel", "parallel", "parallel"),
        name="merge",
    )(steps(a, ATTN_WIDTH), steps(y2, SSM_WIDTH), w_attn_proj, w_ssm_proj,
      steps(p2, P2_WIDTH), steps(p2, P2_WIDTH), _step_permutation(PERM_TILE).T)


def _out_kernel(m_ref, w_ref, x_ref, o_ref):
    o_ref[...] = x_ref[...] + jnp.dot(m_ref[...], w_ref[...], preferred_element_type=F32)


def _out_proj(merged, w_out, x2, tm=1024, tn=1024):
    t = merged.shape[0]
    return pl.pallas_call(
        _out_kernel,
        grid=(D_MODEL // tn, t // tm),
        in_specs=[
            pl.BlockSpec((tm, D_MODEL), lambda j, i: (i, 0)),
            pl.BlockSpec((D_MODEL, tn), lambda j, i: (0, j)),
            pl.BlockSpec((tm, tn), lambda j, i: (i, j)),
        ],
        out_specs=pl.BlockSpec((tm, tn), lambda j, i: (i, j)),
        out_shape=jax.ShapeDtypeStruct((t, D_MODEL), F32),
        compiler_params=_params("parallel", "parallel"),
        name="out_proj",
    )(merged, w_out, x2)


def _layer(x, w, late):
    b, s, d = x.shape
    h_steps = _prenorm(x, w["norm_gain"])
    h2 = h_steps.reshape(b * s, d)
    side = late if late[0].dtype != BF16 else ()
    p1, rounded = _proj(h2, w["w_in"], COL_Q, P1_WIDTH, w["qk_gain_row"], 2 * ATTN_WIDTH, "proj_attn", side)
    w_glu, w_attn_proj, w_ssm_proj, w_out = rounded if side else late
    p2, _ = _proj(h2, w["w_in"], COL_GB, P2_WIDTH, None, 0, "proj_gates")
    u_t = _proj_u(h_steps, w["wu_t"])
    a = _attention(p1, w["bias_tabs"], b, s).reshape(b * s, ATTN_WIDTH)
    y_t = _ssm(u_t, w["wz"], w["t0"], w["cc"], w["a16"], w["d_rows"])
    y2 = _glu(y_t, w_glu, p2)
    merged = _merge(a, y2, w_attn_proj, w_ssm_proj, p2, b, s)
    out = _out_proj(merged, w_out, x.reshape(b * s, d)).reshape(b, s, d)
    return out, (w_glu, w_attn_proj, w_ssm_proj, w_out)


def _prepare(rel_bias, norm_gain, w_in, q_gain, k_gain, lam_re, lam_im, log_dt, b_re, b_im, c_re, c_im, d_skip):
    wz, t0, cc, a16 = _ssm_weights(lam_re, lam_im, log_dt, b_re, b_im, c_re, c_im)
    gains = jnp.concatenate([jnp.tile(q_gain.astype(F32) * (HEAD_DIM ** -0.5 * LOG2E), N_HEADS),
                             jnp.tile(k_gain.astype(F32), N_HEADS),
                             jnp.zeros((P1_WIDTH - 2 * ATTN_WIDTH,), F32)])
    d_rows = jnp.tile(d_skip.astype(F32).reshape(SSM_GROUPS, 1, SSM_GROUP), (1, SSM_BLOCK, 1))
    return {
        "norm_gain": norm_gain.astype(F32),
        "w_in": w_in.astype(BF16),
        "wu_t": w_in[:, COL_U:COL_U + SSM_WIDTH].T.astype(BF16),
        "qk_gain_row": gains.reshape(1, P1_WIDTH),
        "bias_tabs": _bias_tables(rel_bias),
        "wz": wz, "t0": t0, "cc": cc, "a16": a16,
        "d_rows": d_rows.reshape(SSM_GROUPS, SSM_ROWS, 1),
    }


def kernel(x_prompt, x_sample, rel_bias, norm_gain, w_in, q_gain, k_gain, lam_re, lam_im, log_dt, b_re, b_im,
           c_re, c_im, d_skip, w_glu, w_attn_proj, w_ssm_proj, w_out):
    y_prompt, y_sample = x_prompt, x_sample
    for l in range(norm_gain.shape[0]):
        w = _prepare(rel_bias, norm_gain[l], w_in[l], q_gain[l], k_gain[l], lam_re[l], lam_im[l], log_dt[l],
                     b_re[l], b_im[l], c_re[l], c_im[l], d_skip[l])
        late = tuple(m[l].astype(F32) for m in (w_glu, w_attn_proj, w_ssm_proj, w_out))
        y_prompt, late = _layer(y_prompt, w, late)
        y_sample, _ = _layer(y_sample, w, late)
    return (y_prompt, y_sample)
```

```python
import functools
import math

import jax
import jax.numpy as jnp
from jax import lax
from jax.experimental import pallas as pl
from jax.experimental.pallas import tpu as pltpu

F32 = jnp.float32
BF16 = jnp.bfloat16

D_MODEL = 4096
N_HEADS = 16
HEAD_DIM = 128
ATTN_WIDTH = N_HEADS * HEAD_DIM
SIDE_KEYS = 64
N_BUCKETS = 32
REL_MAX_DIST = 1024
NEG = -1e30
LOG2E = math.log2(math.e)
EPS = 1e-6
SSM_WIDTH = 2048
SSM_GROUP = 16
SSM_GROUPS = SSM_WIDTH // SSM_GROUP
SSM_STATE = 64
SSM_BLOCK = 16
SSM_ROWS = SSM_BLOCK * SSM_GROUP

COL_Q = 0
COL_K = ATTN_WIDTH
COL_V = 2 * ATTN_WIDTH
COL_GA = 3 * ATTN_WIDTH
COL_U = 4 * ATTN_WIDTH
COL_GB = COL_U + SSM_WIDTH
COL_MA = COL_GB + SSM_WIDTH
COL_MB = COL_MA + D_MODEL
P1_WIDTH = COL_U
P2_WIDTH = SSM_WIDTH + 2 * D_MODEL

LANES = 128
PLANES = SSM_BLOCK
PERM_TILE = 256
EPILOGUE_CHUNK = 512
GLU_CHUNK = 256
ATTN_GEOM = ((1, 8, 16), (4, 32, 64), (16, 128, 256))
ATTN_BATCH = (16, 16, 16)
ATTN_ROWS = 256
HALO_ROWS = SIDE_KEYS

VMEM_BYTES_V7X = 64 * 1024 * 1024
VMEM_LIMIT = VMEM_BYTES_V7X - 4 * 1024 * 1024


def _params(*sem):
    return pltpu.CompilerParams(dimension_semantics=sem, vmem_limit_bytes=VMEM_LIMIT)


def _prenorm_kernel(x_ref, g_ref, perm_ref, o_ref):
    x = x_ref[...]
    ms = jnp.mean(x * x, axis=-1, keepdims=True)
    h = (x * lax.rsqrt(ms + EPS) * g_ref[...]).astype(BF16)
    hp = jnp.dot(perm_ref[...], h, preferred_element_type=F32).astype(BF16)
    o_ref[...] = hp.reshape(o_ref.shape)


def _prenorm(x, gain, tm=PERM_TILE):
    b, s, d = x.shape
    return pl.pallas_call(
        _prenorm_kernel,
        grid=(b, s // tm),
        in_specs=[pl.BlockSpec((None, tm, d), lambda bb, t: (bb, t, 0)),
                  pl.BlockSpec((1, d), lambda bb, t: (0, 0)),
                  pl.BlockSpec((tm, tm), lambda bb, t: (0, 0))],
        out_specs=pl.BlockSpec((None, SSM_BLOCK, tm // SSM_BLOCK, d), lambda bb, t: (bb, 0, t, 0)),
        out_shape=jax.ShapeDtypeStruct((b, SSM_BLOCK, s // SSM_BLOCK, d), BF16),
        compiler_params=_params("parallel", "parallel"),
        name="prenorm",
    )(x, gain.reshape(1, d), _step_permutation(tm))


def _proj_kernel(h_ref, w_ref, g_ref, *refs, n_norm_tiles, n_side):
    side_in, o_ref, side_out = refs[:n_side], refs[n_side], refs[n_side + 1:]
    j = pl.program_id(0)

    for src, dst in zip(side_in, side_out):
        dst[...] = src[...].astype(BF16)

    @pl.when(j < n_norm_tiles)
    def _():
        def issue(cols):
            return jnp.dot(h_ref[...], w_ref[:, cols], preferred_element_type=F32)

        def finish(cols, acc):
            for hh in range(acc.shape[1] // HEAD_DIM):
                blk = acc[:, hh * HEAD_DIM:(hh + 1) * HEAD_DIM]
                sl = slice(cols.start + hh * HEAD_DIM, cols.start + (hh + 1) * HEAD_DIM)
                ms = jnp.mean(blk * blk, axis=-1, keepdims=True)
                o_ref[:, sl] = blk * lax.rsqrt(ms + EPS) * g_ref[:, sl]

        _column_pipeline(o_ref.shape[1], EPILOGUE_CHUNK, issue, finish)

    @pl.when(j >= n_norm_tiles)
    def _():
        o_ref[...] = jnp.dot(h_ref[...], w_ref[...], preferred_element_type=F32)


def _proj(h, w, col0, width, gain_row, n_norm_cols, name, side=(), tm=512, tn=2048):
    t, k = h.shape
    if gain_row is None:
        gain_row = jnp.zeros((1, width), F32)
    nm = t // tm
    nsteps = (width // tn) * nm

    def side_spec(mat):
        return pl.BlockSpec((mat.shape[0] // nsteps, mat.shape[1]), lambda j, i: (j * nm + i, 0))

    out = pl.pallas_call(
        functools.partial(_proj_kernel, n_norm_tiles=n_norm_cols // tn, n_side=len(side)),
        grid=(width // tn, nm),
        in_specs=[
            pl.BlockSpec((tm, k), lambda j, i: (i, 0)),
            pl.BlockSpec((k, tn), lambda j, i: (0, col0 // tn + j)),
            pl.BlockSpec((1, tn), lambda j, i: (0, j)),
        ] + [side_spec(mat) for mat in side],
        out_specs=[pl.BlockSpec((tm, tn), lambda j, i: (i, j))] + [side_spec(mat) for mat in side],
        out_shape=[jax.ShapeDtypeStruct((t, width), F32)]
        + [jax.ShapeDtypeStruct(mat.shape, BF16) for mat in side],
        compiler_params=_params("parallel", "parallel"),
        name=name,
    )(h, w, gain_row, *side)
    return out[0], tuple(out[1:])


def _proj_u_kernel(w_ref, h_ref, o_ref):
    o_ref[...] = lax.dot_general(w_ref[...], h_ref[...], (((1,), (1,)), ((), ())),
                                 preferred_element_type=F32)


def _proj_u(h_steps, wu_t, tokens=512):
    b, _, ntau, _ = h_steps.shape
    tmt = min(tokens, ntau)
    return pl.pallas_call(
        _proj_u_kernel,
        grid=(b, SSM_BLOCK, ntau // tmt),
        in_specs=[
            pl.BlockSpec((SSM_WIDTH, D_MODEL), lambda bb, i, tt: (0, 0)),
            pl.BlockSpec((None, None, tmt, D_MODEL), lambda bb, i, tt: (bb, i, tt, 0)),
        ],
        out_specs=pl.BlockSpec((None, None, SSM_WIDTH, tmt), lambda bb, i, tt: (bb, i, 0, tt)),
        out_shape=jax.ShapeDtypeStruct((b, SSM_BLOCK, SSM_WIDTH, ntau), F32),
        compiler_params=_params("parallel", "parallel", "parallel"),
        name="proj_u",
    )(wu_t, h_steps)


def _rel_bucket(rel):
    half = N_BUCKETS // 2
    exact = half // 2
    base = jnp.where(rel > 0, half, 0)
    n = jnp.abs(rel)
    nf = jnp.maximum(n, 1).astype(F32)
    large = exact + (jnp.log(nf / exact) / math.log(REL_MAX_DIST / exact) * (half - exact)).astype(jnp.int32)
    large = jnp.minimum(large, half - 1)
    return base + jnp.where(n < exact, n, large)


def _bias_tables(rel_bias):
    buckets = jnp.arange(N_BUCKETS)
    tabs = []
    for d, nq, nk in ATTN_GEOM:
        g = PLANES // d
        koff = (nk - nq) // 2
        aq = jnp.arange(g)[:, None, None, None]
        tq = jnp.arange(nq)[None, :, None, None]
        ak = jnp.arange(g)[None, None, :, None]
        tk = jnp.arange(nk)[None, None, None, :]
        rel = (g * ((tk - koff) - tq) + (ak - aq)).reshape(g * nq, g * nk)
        onehot = (_rel_bucket(rel * d)[..., None] == buckets).astype(F32)
        bias = jnp.einsum('qkb,bh->hqk', onehot, rel_bias.astype(F32), precision=lax.Precision.HIGHEST)
        bias = jnp.where((jnp.abs(rel) <= SIDE_KEYS)[None], bias, NEG)
        trow = jnp.arange(g * nk) % nk - koff
        before = (trow < 0)[None, None, :]
        after = (trow >= nq)[None, None, :]
        bias = bias * LOG2E
        tabs.append(jnp.stack([bias, jnp.where(before, NEG, bias), jnp.where(after, NEG, bias)], axis=1))
    return tabs


def _attn_kernel(q_ref, kp_ref, kc_ref, kn_ref, vp_ref, vc_ref, vn_ref, ga_ref, t1_ref, t4_ref, t16_ref, o_ref,
                 nbuf, dbuf, mbuf):
    t = pl.program_id(2)
    first = t == 0
    last = t == pl.num_programs(2) - 1
    k_refs = (kp_ref, kc_ref, kn_ref)
    v_refs = (vp_ref, vc_ref, vn_ref)
    ones = jnp.ones((HEAD_DIM,), BF16)

    def halo_rows(refs, plane, start, n):
        pieces = []
        r0 = -HALO_ROWS
        for ref in refs:
            r1 = r0 + ref.shape[1]
            lo, hi = max(start, r0), min(start + n, r1)
            if lo < hi:
                pieces.append(ref[plane, lo - r0:hi - r0, :])
            r0 = r1
        return pieces

    edge_bias = {}

    def logits(pi, grp):
        d, nq, nk = ATTN_GEOM[pi]
        tab_ref = (t1_ref, t4_ref, t16_ref)[pi]
        g = PLANES // d
        nblk = ATTN_ROWS // nq
        k0 = -((nk - nq) // 2)

        def rows_of(refs, r, start, n):
            pieces = [piece for a in range(g) for piece in halo_rows(refs, r + d * a, start, n)]
            return jnp.concatenate(pieces, axis=0).astype(BF16)

        def bias_of(blk):
            if (pi, blk) not in edge_bias:
                bias = tab_ref[0]
                if blk == 0:
                    bias = jnp.where(first, tab_ref[1], bias)
                if blk == nblk - 1:
                    bias = jnp.minimum(bias, jnp.where(last, tab_ref[2], tab_ref[0]))
                edge_bias[(pi, blk)] = bias
            return edge_bias[(pi, blk)]

        q = jnp.stack([jnp.concatenate([q_ref[r + d * a, blk * nq:(blk + 1) * nq, :] for a in range(g)],
                                       axis=0).astype(BF16) for r, blk in grp])
        k = jnp.stack([rows_of(k_refs, r, blk * nq + k0, nk) for r, blk in grp])
        v = jnp.stack([rows_of(v_refs, r, blk * nq + k0, nk) for r, blk in grp])
        edges = {blk if blk in (0, nblk - 1) else None for _, blk in grp}
        if len(edges) == 1:
            blk = edges.pop()
            bias = (tab_ref[0] if blk is None else bias_of(blk))[None]
        else:
            bias = jnp.stack([bias_of(blk) if blk in (0, nblk - 1) else tab_ref[0] for _, blk in grp])
        return jnp.einsum('bqd,bkd->bqk', q, k, preferred_element_type=F32) + bias, v

    def finish(pi, grp, s, v):
        d, nq, _ = ATTN_GEOM[pi]
        g = PLANES // d
        m = jnp.max(s, axis=-1, keepdims=True)
        p = jnp.exp2(s - m).astype(BF16)
        v1 = jnp.concatenate([v, jnp.broadcast_to(ones, v.shape)], axis=-1)
        acc = jnp.einsum('bqk,bkd->bqd', p, v1, preferred_element_type=F32)
        mb = jnp.broadcast_to(m, acc.shape[:2] + (HEAD_DIM,))
        for bi, (r, blk) in enumerate(grp):
            for a in range(g):
                rows = slice(blk * nq, (blk + 1) * nq)
                nbuf[pi, r + d * a, rows, :] = acc[bi, a * nq:(a + 1) * nq, :HEAD_DIM]
                dbuf[pi, r + d * a, rows, :] = acc[bi, a * nq:(a + 1) * nq, HEAD_DIM:]
                mbuf[pi, r + d * a, rows, :] = mb[bi, a * nq:(a + 1) * nq]

    work = []
    for pi, (d, nq, _) in enumerate(ATTN_GEOM):
        blocks = [(r, blk) for r in range(d) for blk in range(ATTN_ROWS // nq)]
        work += [(pi, blocks[g0:g0 + ATTN_BATCH[pi]]) for g0 in range(0, len(blocks), ATTN_BATCH[pi])]
    pending = logits(*work[0])
    for idx, (pi, grp) in enumerate(work):
        s, v = pending
        if idx + 1 < len(work):
            pending = logits(*work[idx + 1])
        finish(pi, grp, s, v)

    for i in range(PLANES):
        m0, m1, m2 = mbuf[0, i], mbuf[1, i], mbuf[2, i]
        mx = jnp.maximum(jnp.maximum(m0, m1), m2)
        e0, e1, e2 = jnp.exp2(m0 - mx), jnp.exp2(m1 - mx), jnp.exp2(m2 - mx)
        o = ((e0 * nbuf[0, i] + e1 * nbuf[1, i] + e2 * nbuf[2, i])
             / (e0 * dbuf[0, i] + e1 * dbuf[1, i] + e2 * dbuf[2, i]))
        ga = ga_ref[i]
        o_ref[i] = (o * (ga * jax.nn.sigmoid(ga))).astype(o_ref.dtype)


def _attention(p1, bias_tabs, b, s):
    nrow = s // PLANES
    p1 = p1.reshape(b, PLANES, nrow, P1_WIDTH)
    per = ATTN_ROWS // HALO_ROWS
    n_halo = nrow // HALO_ROWS

    def tile(col):
        return pl.BlockSpec((None, PLANES, ATTN_ROWS, HEAD_DIM), lambda bb, h, t: (bb, 0, t, col // HEAD_DIM + h))

    def prev(col):
        return pl.BlockSpec((None, PLANES, HALO_ROWS, HEAD_DIM),
                            lambda bb, h, t: (bb, 0, jnp.maximum(t * per - 1, 0), col // HEAD_DIM + h))

    def nxt(col):
        return pl.BlockSpec((None, PLANES, HALO_ROWS, HEAD_DIM),
                            lambda bb, h, t: (bb, 0, jnp.minimum((t + 1) * per, n_halo - 1), col // HEAD_DIM + h))

    def table(tab):
        return pl.BlockSpec((None,) + tab.shape[1:], lambda bb, h, t: (h, 0, 0, 0))

    part_buf = pltpu.VMEM((len(ATTN_GEOM), PLANES, ATTN_ROWS, HEAD_DIM), F32)
    return pl.pallas_call(
        _attn_kernel,
        grid=(b, N_HEADS, nrow // ATTN_ROWS),
        in_specs=[tile(COL_Q), prev(COL_K), tile(COL_K), nxt(COL_K), prev(COL_V), tile(COL_V), nxt(COL_V),
                  tile(COL_GA)] + [table(tab) for tab in bias_tabs],
        out_specs=pl.BlockSpec((None, PLANES, ATTN_ROWS, HEAD_DIM), lambda bb, h, t: (bb, 0, t, h)),
        out_shape=jax.ShapeDtypeStruct((b, PLANES, nrow, ATTN_WIDTH), BF16),
        scratch_shapes=[part_buf, part_buf, part_buf],
        compiler_params=_params("parallel", "parallel", "parallel"),
        name="attention",
    )(p1, p1, p1, p1, p1, p1, p1, p1, *bias_tabs)


def _ssm_prep_kernel(arow_ref, acol_ref, c_ref, bw_ref, wz_ref, t0_ref, cc_ref, *, gc):
    nb, hc, p = SSM_BLOCK, SSM_GROUP, SSM_STATE
    blk_of_lane = lax.broadcasted_iota(jnp.int32, (hc, SSM_ROWS), 1) // hc
    blk_of_lane_p = lax.broadcasted_iota(jnp.int32, (p, SSM_ROWS), 1) // hc

    def body(gi, carry):
        lag_kernels = []
        for dirn in range(2):
            ar = arow_ref[gi, 2 * dirn:2 * dirn + 1, :]
            ai = arow_ref[gi, 2 * dirn + 1:2 * dirn + 2, :]
            cr = c_ref[gi, 2 * dirn]
            ci = c_ref[gi, 2 * dirn + 1]
            pr = jnp.ones((1, p), F32)
            pi = jnp.zeros((1, p), F32)
            ca_re, ca_im = [], []
            for _ in range(nb + 1):
                ca_re.append(cr * pr - ci * pi)
                ca_im.append(cr * pi + ci * pr)
                pr, pi = pr * ar - pi * ai, pr * ai + pi * ar
            bre = bw_ref[gi, 2 * dirn]
            bim = bw_ref[gi, 2 * dirn + 1]
            lag_kernels.append(
                jnp.dot(jnp.concatenate(ca_re[:nb], axis=0).astype(BF16), bre.astype(BF16),
                        preferred_element_type=F32)
                - jnp.dot(jnp.concatenate(ca_im[:nb], axis=0).astype(BF16), bim.astype(BF16),
                          preferred_element_type=F32))
            order = range(1, nb + 1) if dirn == 0 else range(nb, 0, -1)
            c0 = 2 * p * dirn
            cc_ref[gi, :, c0:c0 + p] = jnp.concatenate([ca_re[k] for k in order], axis=0).astype(BF16)
            cc_ref[gi, :, c0 + p:c0 + 2 * p] = (-jnp.concatenate([ca_im[k] for k in order], axis=0)).astype(BF16)
            acr = acol_ref[gi, c0:c0 + p, :]
            aci = acol_ref[gi, c0 + p:c0 + 2 * p, :]
            qr = jnp.ones((p, 1), F32)
            qi = jnp.zeros((p, 1), F32)
            powers = []
            for _ in range(nb):
                powers.append((qr, qi))
                qr, qi = qr * acr - qi * aci, qr * aci + qi * acr
            per = jnp.zeros((p, SSM_ROWS), F32)
            pei = jnp.zeros((p, SSM_ROWS), F32)
            for i in range(nb):
                e = nb - 1 - i if dirn == 0 else i
                per = jnp.where(blk_of_lane_p == i, powers[e][0], per)
                pei = jnp.where(blk_of_lane_p == i, powers[e][1], pei)
            wz_ref[gi, c0:c0 + p, :] = (per * bre - pei * bim).astype(BF16)
            wz_ref[gi, c0 + p:c0 + 2 * p, :] = (per * bim + pei * bre).astype(BF16)
        kf, kb = lag_kernels
        diag = kf[0:hc, :] + kb[0:hc, :]
        for j in range(nb):
            acc = jnp.zeros((hc, SSM_ROWS), F32)
            for i in range(nb):
                if i < j:
                    src = kf[hc * (j - i):hc * (j - i + 1), :]
                elif i > j:
                    src = kb[hc * (i - j):hc * (i - j + 1), :]
                else:
                    src = diag
                acc = jnp.where(blk_of_lane == i, src, acc)
            t0_ref[gi, hc * j:hc * (j + 1), :] = acc.astype(BF16)
        return carry

    lax.fori_loop(0, gc, body, 0)


def _ssm_weights(lam_re, lam_im, log_dt, b_re, b_im, c_re, c_im, gc=8):
    g, p, hc = SSM_GROUPS, SSM_STATE, SSM_GROUP
    dt = jnp.exp(log_dt.astype(F32))[..., None]
    lr = lam_re.astype(F32)
    li = lam_im.astype(F32)
    mag = jnp.exp(lr * dt)
    ab_re = mag * jnp.cos(li * dt)
    ab_im = mag * jnp.sin(li * dt)
    den = lr * lr + li * li
    f_re = ((ab_re - 1.0) * lr + ab_im * li) / den
    f_im = (ab_im * lr - (ab_re - 1.0) * li) / den
    br = b_re.astype(F32)
    bi = b_im.astype(F32)
    bb_re = f_re[..., None] * br - f_im[..., None] * bi
    bb_im = f_re[..., None] * bi + f_im[..., None] * br
    mag16 = jnp.exp(SSM_BLOCK * (lr * dt))
    a16_re = mag16 * jnp.cos(SSM_BLOCK * (li * dt))
    a16_im = mag16 * jnp.sin(SSM_BLOCK * (li * dt))

    def pack(re, im):
        return jnp.stack([re[0], im[0], re[1], im[1]], axis=1)

    arow = pack(ab_re, ab_im)
    acol = arow.reshape(g, 4 * p, 1)
    a16 = pack(a16_re, a16_im).reshape(g, 4 * p, 1)
    cmat = pack(c_re.astype(F32), c_im.astype(F32))
    bw = jnp.tile(pack(bb_re, bb_im), (1, 1, 1, SSM_BLOCK))
    wspec = pl.BlockSpec((gc, SSM_ROWS, SSM_ROWS), lambda i: (i, 0, 0))
    wshape = jax.ShapeDtypeStruct((g, SSM_ROWS, SSM_ROWS), BF16)
    wz, t0, cc = pl.pallas_call(
        functools.partial(_ssm_prep_kernel, gc=gc),
        grid=(g // gc,),
        in_specs=[pl.BlockSpec((gc, 4, p), lambda i: (i, 0, 0)),
                  pl.BlockSpec((gc, 4 * p, 1), lambda i: (i, 0, 0)),
                  pl.BlockSpec((gc, 4, hc, p), lambda i: (i, 0, 0, 0)),
                  pl.BlockSpec((gc, 4, p, SSM_ROWS), lambda i: (i, 0, 0, 0))],
        out_specs=[wspec, wspec, wspec],
        out_shape=[wshape, wshape, wshape],
        compiler_params=_params("parallel"),
        name="ssm_prep",
    )(arow, acol, cmat, bw)
    return wz, t0, cc, a16


def _gelu_tanh(x):
    return 0.5 * x * (1.0 + jnp.tanh(math.sqrt(2.0 / math.pi) * (x + 0.044715 * (x * x * x))))


def _cmul(ar, ai, br, bi):
    return ar * br - ai * bi, ar * bi + ai * br


def _block_states(zs, coefs, nc):
    p = SSM_STATE
    ng = len(zs)
    rows = ng * p
    lane = lax.broadcasted_iota(jnp.int32, (rows, LANES), 1)
    zero = jnp.zeros((rows, LANES), F32)

    def part(arrs, r0, c=None):
        cols = slice(None) if c is None else slice(LANES * c, LANES * (c + 1))
        return jnp.concatenate([x[r0:r0 + p, cols] for x in arrs], axis=0)

    halves = []
    for dirn in range(2):
        base = 2 * p * dirn
        zr = [part(zs, base, c) for c in range(nc)]
        zi = [part(zs, base + p, c) for c in range(nc)]
        ar = jnp.broadcast_to(part(coefs, base), (rows, LANES))
        ai = jnp.broadcast_to(part(coefs, base + p), (rows, LANES))
        order = list(range(nc)) if dirn == 0 else list(range(nc - 1, -1, -1))
        ir, ii = {order[0]: zr[order[0]]}, {order[0]: zi[order[0]]}
        for prev, c in zip(order[:-1], order[1:]):
            tr, ti = _cmul(ar, ai, ir[prev], ii[prev])
            ir[c], ii[c] = zr[c] + tr, zi[c] + ti
        pows = [(ar, ai)]
        for _ in range(nc - 1):
            pows.append(_cmul(pows[-1][0], pows[-1][1], ar, ai))
        gr, gi = ir[order[-1]], ii[order[-1]]
        br, bi = pows[nc - 1]
        s = 1
        while s < LANES:
            keep = (lane >= s) if dirn == 0 else (lane < LANES - s)
            amt = s if dirn == 0 else LANES - s
            rr = jnp.where(keep, pltpu.roll(gr, amt, 1), zero)
            ri = jnp.where(keep, pltpu.roll(gi, amt, 1), zero)
            tr, ti = _cmul(br, bi, rr, ri)
            gr, gi = gr + tr, gi + ti
            br, bi = _cmul(br, bi, br, bi)
            s *= 2
        keep = (lane >= 1) if dirn == 0 else (lane < LANES - 1)
        amt = 1 if dirn == 0 else LANES - 1
        er = jnp.where(keep, pltpu.roll(gr, amt, 1), zero)
        ei = jnp.where(keep, pltpu.roll(gi, amt, 1), zero)
        hr, hi = [None] * nc, [None] * nc
        hr[order[0]], hi[order[0]] = er, ei
        for k, (prev, c) in enumerate(zip(order[:-1], order[1:])):
            tr, ti = _cmul(pows[k][0], pows[k][1], er, ei)
            hr[c], hi[c] = ir[prev] + tr, ii[prev] + ti
        halves += [jnp.concatenate(hr, axis=1), jnp.concatenate(hi, axis=1)]
    return [jnp.concatenate([h[g * p:(g + 1) * p] for h in halves], axis=0) for g in range(ng)]


def _ssm_kernel(u_ref, wz_ref, t0_ref, cc_ref, a_ref, d_ref, perm_ref, unperm_ref, y_ref, *, gc, ntau):
    nc = ntau // LANES
    rows = [slice(SSM_GROUP * gi, SSM_GROUP * (gi + 1)) for gi in range(gc)]
    zs = []
    for gi in range(gc):
        x = u_ref[:, rows[gi], :].reshape(SSM_ROWS, ntau).astype(BF16)
        xp = jnp.dot(x, perm_ref[...], preferred_element_type=F32).astype(BF16)
        zs.append(jnp.dot(wz_ref[gi], xp, preferred_element_type=F32))
    states = _block_states(zs, [a_ref[gi] for gi in range(gc)], nc)
    for gi in range(gc):
        xf = u_ref[:, rows[gi], :].reshape(SSM_ROWS, ntau)
        h = jnp.dot(states[gi].astype(BF16), unperm_ref[...], preferred_element_type=F32).astype(BF16)
        y = (jnp.dot(t0_ref[gi], xf.astype(BF16), preferred_element_type=F32)
             + jnp.dot(cc_ref[gi], h, preferred_element_type=F32))
        y = _gelu_tanh(y + d_ref[gi] * xf)
        y_ref[:, rows[gi], :] = y.astype(y_ref.dtype).reshape(SSM_BLOCK, SSM_GROUP, ntau)


def _ssm(u_t, wz, t0, cc, a16, d_rows, gc=16):
    b, _, _, ntau = u_t.shape
    nc = ntau // LANES
    j = jnp.arange(ntau)
    src = (j % LANES) * nc + j // LANES
    perm = (jnp.arange(ntau)[:, None] == src[None, :]).astype(BF16)
    wspec = pl.BlockSpec((gc, SSM_ROWS, SSM_ROWS), lambda bb, g: (g, 0, 0))
    vspec = pl.BlockSpec((gc, SSM_ROWS, 1), lambda bb, g: (g, 0, 0))
    uspec = pl.BlockSpec((None, SSM_BLOCK, SSM_GROUP * gc, ntau), lambda bb, g: (bb, 0, g, 0))
    pspec = pl.BlockSpec((ntau, ntau), lambda bb, g: (0, 0))
    return pl.pallas_call(
        functools.partial(_ssm_kernel, gc=gc, ntau=ntau),
        grid=(b, SSM_GROUPS // gc),
        in_specs=[uspec, wspec, wspec, wspec, vspec, vspec, pspec, pspec],
        out_specs=uspec,
        out_shape=jax.ShapeDtypeStruct(u_t.shape, BF16),
        compiler_params=_params("parallel", "parallel"),
        name="ssm",
    )(u_t, wz, t0, cc, a16, d_rows, perm, perm.T)


def _column_pipeline(width, chunk, issue, finish):
    chunks = [slice(c, c + chunk) for c in range(0, width, chunk)]
    pending = issue(chunks[0])
    for idx, cols in enumerate(chunks):
        current = pending
        if idx + 1 < len(chunks):
            pending = issue(chunks[idx + 1])
        finish(cols, current)


def _glu_kernel(y_ref, w1_ref, w2_ref, gb_ref, o_ref):
    pp, tmt, _ = o_ref.shape
    yt = jnp.concatenate([y_ref[p].T for p in range(pp)], axis=0)

    def issue(cols):
        return (jnp.dot(yt, w1_ref[:, cols], preferred_element_type=F32),
                jnp.dot(yt, w2_ref[:, cols], preferred_element_type=F32))

    def finish(cols, gl):
        gb = gb_ref[:, :, cols].reshape(pp * tmt, -1)
        out = (gl[0] * jax.nn.sigmoid(gl[1]) * (gb * jax.nn.sigmoid(gb))).astype(o_ref.dtype)
        o_ref[:, :, cols] = out.reshape(pp, tmt, -1)

    _column_pipeline(o_ref.shape[2], GLU_CHUNK, issue, finish)


def _glu(y_t, w_glu, p2, tokens=512, tn=2048):
    b, _, _, ntau = y_t.shape
    tmt = min(tokens, ntau)
    pp = tokens // tmt
    nj = SSM_WIDTH // tn
    p2v = p2.reshape(b, SSM_BLOCK, ntau, P2_WIDTH)
    out = pl.pallas_call(
        _glu_kernel,
        grid=(nj, b, SSM_BLOCK // pp, ntau // tmt),
        in_specs=[
            pl.BlockSpec((None, pp, SSM_WIDTH, tmt), lambda j, bb, i, tt: (bb, i, 0, tt)),
            pl.BlockSpec((SSM_WIDTH, tn), lambda j, bb, i, tt: (0, j)),
            pl.BlockSpec((SSM_WIDTH, tn), lambda j, bb, i, tt: (0, nj + j)),
            pl.BlockSpec((None, pp, tmt, tn), lambda j, bb, i, tt: (bb, i, tt, j)),
        ],
        out_specs=pl.BlockSpec((None, pp, tmt, tn), lambda j, bb, i, tt: (bb, i, tt, j)),
        out_shape=jax.ShapeDtypeStruct((b, SSM_BLOCK, ntau, SSM_WIDTH), BF16),
        compiler_params=_params("parallel", "parallel", "parallel", "parallel"),
        name="glu",
    )(y_t, w_glu, w_glu, p2v)
    return out.reshape(b * SSM_BLOCK * ntau, SSM_WIDTH)


def _merge_kernel(a_ref, y_ref, wa_ref, wb_ref, ma_ref, mb_ref, unperm_ref, o_ref):
    rows = o_ref.shape[0]
    a = a_ref[...].reshape(rows, ATTN_WIDTH)
    y = y_ref[...].reshape(rows, SSM_WIDTH)
    sub = unperm_ref.shape[0]
    n = sub // SSM_BLOCK

    def issue(cols):
        return (jnp.dot(a, wa_ref[:, cols], preferred_element_type=F32),
                jnp.dot(y, wb_ref[:, cols], preferred_element_type=F32))

    def finish(cols, z):
        ma = ma_ref[:, :, cols].reshape(rows, -1)
        mb = mb_ref[:, :, cols].reshape(rows, -1)
        merged = (jax.nn.sigmoid(ma) * z[0] + jax.nn.sigmoid(mb) * z[1]).astype(BF16)
        merged = merged.reshape(SSM_BLOCK, rows // SSM_BLOCK, -1)
        for k in range(rows // sub):
            part = merged[:, k * n:(k + 1) * n, :].reshape(sub, -1)
            o_ref[k * sub:(k + 1) * sub, cols] = jnp.dot(unperm_ref[...], part,
                                                         preferred_element_type=F32).astype(o_ref.dtype)

    _column_pipeline(o_ref.shape[1], EPILOGUE_CHUNK, issue, finish)


def _step_permutation(tm):
    n = tm // SSM_BLOCK
    r = jnp.arange(tm)
    src = SSM_BLOCK * (r % n) + r // n
    return (src[:, None] == jnp.arange(tm)[None, :]).astype(BF16)


def _merge(a, y2, w_attn_proj, w_ssm_proj, p2, b, s, tm=512, tn=1024):
    ntau = s // SSM_BLOCK
    n = tm // SSM_BLOCK
    ma0 = (COL_MA - COL_GB) // tn
    mb0 = (COL_MB - COL_GB) // tn

    def steps(arr, width):
        return arr.reshape(b, SSM_BLOCK, ntau, width)

    return pl.pallas_call(
        _merge_kernel,
        grid=(D_MODEL // tn, b, s // tm),
        in_specs=[
            pl.BlockSpec((None, SSM_BLOCK, n, ATTN_WIDTH), lambda j, bb, t: (bb, 0, t, 0)),
            pl.BlockSpec((None, SSM_BLOCK, n, SSM_WIDTH), lambda j, bb, t: (bb, 0, t, 0)),
            pl.BlockSpec((ATTN_WIDTH, tn), lambda j, bb, t: (0, j)),
            pl.BlockSpec((SSM_WIDTH, tn), lambda j, bb, t: (0, j)),
            pl.BlockSpec((None, SSM_BLOCK, n, tn), lambda j, bb, t: (bb, 0, t, ma0 + j)),
            pl.BlockSpec((None, SSM_BLOCK, n, tn), lambda j, bb, t: (bb, 0, t, mb0 + j)),
            pl.BlockSpec((PERM_TILE, PERM_TILE), lambda j, bb, t: (0, 0)),
        ],
        out_specs=pl.BlockSpec((tm, tn), lambda j, bb, t: (bb * (s // tm) + t, j)),
        out_shape=jax.ShapeDtypeStruct((b * s, D_MODEL), BF16),
        compiler_params=_params("parallel", "parallel", "parallel"),
        name="merge",
    )(steps(a, ATTN_WIDTH), steps(y2, SSM_WIDTH), w_attn_proj, w_ssm_proj,
      steps(p2, P2_WIDTH), steps(p2, P2_WIDTH), _step_permutation(PERM_TILE).T)


def _out_kernel(m_ref, w_ref, x_ref, o_ref):
    o_ref[...] = x_ref[...] + jnp.dot(m_ref[...], w_ref[...], preferred_element_type=F32)


def _out_proj(merged, w_out, x2, tm=1024, tn=1024):
    t = merged.shape[0]
    return pl.pallas_call(
        _out_kernel,
        grid=(D_MODEL // tn, t // tm),
        in_specs=[
            pl.BlockSpec((tm, D_MODEL), lambda j, i: (i, 0)),
            pl.BlockSpec((D_MODEL, tn), lambda j, i: (0, j)),
            pl.BlockSpec((tm, tn), lambda j, i: (i, j)),
        ],
        out_specs=pl.BlockSpec((tm, tn), lambda j, i: (i, j)),
        out_shape=jax.ShapeDtypeStruct((t, D_MODEL), F32),
        compiler_params=_params("parallel", "parallel"),
        name="out_proj",
    )(merged, w_out, x2)


def _layer(x, w, late):
    b, s, d = x.shape
    h_steps = _prenorm(x, w["norm_gain"])
    h2 = h_steps.reshape(b * s, d)
    side = late if late[0].dtype != BF16 else ()
    p1, rounded = _proj(h2, w["w_in"], COL_Q, P1_WIDTH, w["qk_gain_row"], 2 * ATTN_WIDTH, "proj_attn", side)
    w_glu, w_attn_proj, w_ssm_proj, w_out = rounded if side else late
    p2, _ = _proj(h2, w["w_in"], COL_GB, P2_WIDTH, None, 0, "proj_gates")
    u_t = _proj_u(h_steps, w["wu_t"])
    a = _attention(p1, w["bias_tabs"], b, s).reshape(b * s, ATTN_WIDTH)
    y_t = _ssm(u_t, w["wz"], w["t0"], w["cc"], w["a16"], w["d_rows"])
    y2 = _glu(y_t, w_glu, p2)
    merged = _merge(a, y2, w_attn_proj, w_ssm_proj, p2, b, s)
    out = _out_proj(merged, w_out, x.reshape(b * s, d)).reshape(b, s, d)
    return out, (w_glu, w_attn_proj, w_ssm_proj, w_out)


def _prepare(rel_bias, norm_gain, w_in, q_gain, k_gain, lam_re, lam_im, log_dt, b_re, b_im, c_re, c_im, d_skip):
    wz, t0, cc, a16 = _ssm_weights(lam_re, lam_im, log_dt, b_re, b_im, c_re, c_im)
    gains = jnp.concatenate([jnp.tile(q_gain.astype(F32) * (HEAD_DIM ** -0.5 * LOG2E), N_HEADS),
                             jnp.tile(k_gain.astype(F32), N_HEADS),
                             jnp.zeros((P1_WIDTH - 2 * ATTN_WIDTH,), F32)])
    d_rows = jnp.tile(d_skip.astype(F32).reshape(SSM_GROUPS, 1, SSM_GROUP), (1, SSM_BLOCK, 1))
    return {
        "norm_gain": norm_gain.astype(F32),
        "w_in": w_in.astype(BF16),
        "wu_t": w_in[:, COL_U:COL_U + SSM_WIDTH].T.astype(BF16),
        "qk_gain_row": gains.reshape(1, P1_WIDTH),
        "bias_tabs": _bias_tables(rel_bias),
        "wz": wz, "t0": t0, "cc": cc, "a16": a16,
        "d_rows": d_rows.reshape(SSM_GROUPS, SSM_ROWS, 1),
    }


def kernel(x_prompt, x_sample, rel_bias, norm_gain, w_in, q_gain, k_gain, lam_re, lam_im, log_dt, b_re, b_im,
           c_re, c_im, d_skip, w_glu, w_attn_proj, w_ssm_proj, w_out):
    y_prompt, y_sample = x_prompt, x_sample
    for l in range(norm_gain.shape[0]):
        w = _prepare(rel_bias, norm_gain[l], w_in[l], q_gain[l], k_gain[l], lam_re[l], lam_im[l], log_dt[l],
                     b_re[l], b_im[l], c_re[l], c_im[l], d_skip[l])
        late = tuple(m[l].astype(F32) for m in (w_glu, w_attn_proj, w_ssm_proj, w_out))
        y_prompt, late = _layer(y_prompt, w, late)
        y_sample, _ = _layer(y_sample, w, late)
    return (y_prompt, y_sample)
```

```python
import functools
import math

import jax
import jax.numpy as jnp
from jax import lax
from jax.experimental import pallas as pl
from jax.experimental.pallas import tpu as pltpu

F32 = jnp.float32
BF16 = jnp.bfloat16

D_MODEL = 4096
N_HEADS = 16
HEAD_DIM = 128
ATTN_WIDTH = N_HEADS * HEAD_DIM
SIDE_KEYS = 64
N_BUCKETS = 32
REL_MAX_DIST = 1024
NEG = -1e30
LOG2E = math.log2(math.e)
EPS = 1e-6
SSM_WIDTH = 2048
SSM_GROUP = 16
SSM_GROUPS = SSM_WIDTH // SSM_GROUP
SSM_STATE = 64
SSM_BLOCK = 16
SSM_ROWS = SSM_BLOCK * SSM_GROUP

COL_Q = 0
COL_K = ATTN_WIDTH
COL_V = 2 * ATTN_WIDTH
COL_GA = 3 * ATTN_WIDTH
COL_U = 4 * ATTN_WIDTH
COL_GB = COL_U + SSM_WIDTH
COL_MA = COL_GB + SSM_WIDTH
COL_MB = COL_MA + D_MODEL
P1_WIDTH = COL_U
P2_WIDTH = SSM_WIDTH + 2 * D_MODEL
P_WIDTH = P1_WIDTH + P2_WIDTH

LANES = 128
PLANES = SSM_BLOCK
PERM_TILE = 256
SIDE_STEPS = 128
EPILOGUE_CHUNK = 512
GLU_CHUNK = 256
ATTN_GEOM = ((1, 8, 16), (4, 32, 64), (16, 128, 256))
ATTN_BATCH = (16, 16, 16)
ATTN_ROWS = 256
HALO_ROWS = SIDE_KEYS

VMEM_BYTES_V7X = 64 * 1024 * 1024
VMEM_LIMIT = VMEM_BYTES_V7X - 4 * 1024 * 1024


def _params(*sem):
    return pltpu.CompilerParams(dimension_semantics=sem, vmem_limit_bytes=VMEM_LIMIT)


def _prenorm_kernel(x_ref, g_ref, perm_ref, o_ref):
    x = x_ref[...]
    ms = jnp.mean(x * x, axis=-1, keepdims=True)
    h = (x * lax.rsqrt(ms + EPS) * g_ref[...]).astype(BF16)
    hp = jnp.dot(perm_ref[...], h, preferred_element_type=F32).astype(BF16)
    o_ref[...] = hp.reshape(o_ref.shape)


def _prenorm(x, gain, tm=PERM_TILE):
    b, s, d = x.shape
    return pl.pallas_call(
        _prenorm_kernel,
        grid=(b, s // tm),
        in_specs=[pl.BlockSpec((None, tm, d), lambda bb, t: (bb, t, 0)),
                  pl.BlockSpec((1, d), lambda bb, t: (0, 0)),
                  pl.BlockSpec((tm, tm), lambda bb, t: (0, 0))],
        out_specs=pl.BlockSpec((None, SSM_BLOCK, tm // SSM_BLOCK, d), lambda bb, t: (bb, 0, t, 0)),
        out_shape=jax.ShapeDtypeStruct((b, SSM_BLOCK, s // SSM_BLOCK, d), BF16),
        compiler_params=_params("parallel", "parallel"),
        name="prenorm",
    )(x, gain.reshape(1, d), _step_permutation(tm))


def _proj_kernel(h_ref, w_ref, g_ref, *refs, n_norm_tiles, n_side):
    side_in, o_ref, side_out = refs[:n_side], refs[n_side], refs[n_side + 1:]
    j = pl.program_id(0)

    for src, dst in zip(side_in, side_out):
        dst[...] = src[...].astype(BF16)

    @pl.when(j < n_norm_tiles)
    def _():
        def issue(cols):
            return jnp.dot(h_ref[...], w_ref[:, cols], preferred_element_type=F32)

        def finish(cols, acc):
            for hh in range(acc.shape[1] // HEAD_DIM):
                blk = acc[:, hh * HEAD_DIM:(hh + 1) * HEAD_DIM]
                sl = slice(cols.start + hh * HEAD_DIM, cols.start + (hh + 1) * HEAD_DIM)
                ms = jnp.mean(blk * blk, axis=-1, keepdims=True)
                o_ref[:, sl] = blk * lax.rsqrt(ms + EPS) * g_ref[:, sl]

        _column_pipeline(o_ref.shape[1], EPILOGUE_CHUNK, issue, finish)

    @pl.when(j >= n_norm_tiles)
    def _():
        o_ref[...] = jnp.dot(h_ref[...], w_ref[...], preferred_element_type=F32)


def _proj(h, w, skip0, width, gain_row, n_norm_cols, name, side=(), tm=512, tn=2048):
    t, k = h.shape
    if gain_row is None:
        gain_row = jnp.zeros((1, width), F32)
    nm = t // tm
    nsteps = (width // tn) * nm

    side_steps = 1 << (min(nsteps, SIDE_STEPS).bit_length() - 1)

    def side_spec(mat):
        return pl.BlockSpec((mat.shape[0] // side_steps, mat.shape[1]),
                            lambda j, i: (jnp.minimum(j * nm + i, side_steps - 1), 0))

    out = pl.pallas_call(
        functools.partial(_proj_kernel, n_norm_tiles=n_norm_cols // tn, n_side=len(side)),
        grid=(width // tn, nm),
        in_specs=[
            pl.BlockSpec((tm, k), lambda j, i: (i, 0)),
            pl.BlockSpec((k, tn), lambda j, i: (0, j + (j >= skip0 // tn).astype(jnp.int32))),
            pl.BlockSpec((1, tn), lambda j, i: (0, j)),
        ] + [side_spec(mat) for mat in side],
        out_specs=[pl.BlockSpec((tm, tn), lambda j, i: (i, j))] + [side_spec(mat) for mat in side],
        out_shape=[jax.ShapeDtypeStruct((t, width), F32)]
        + [jax.ShapeDtypeStruct(mat.shape, BF16) for mat in side],
        compiler_params=_params("arbitrary", "arbitrary"),
        name=name,
    )(h, w, gain_row, *side)
    return out[0], tuple(out[1:])


def _proj_u_kernel(w_ref, h_ref, o_ref):
    o_ref[...] = lax.dot_general(w_ref[...], h_ref[...], (((1,), (1,)), ((), ())),
                                 preferred_element_type=F32)


def _proj_u(h_steps, wu_t, tokens=512):
    b, _, ntau, _ = h_steps.shape
    tmt = min(tokens, ntau)
    return pl.pallas_call(
        _proj_u_kernel,
        grid=(b, SSM_BLOCK, ntau // tmt),
        in_specs=[
            pl.BlockSpec((SSM_WIDTH, D_MODEL), lambda bb, i, tt: (0, 0)),
            pl.BlockSpec((None, None, tmt, D_MODEL), lambda bb, i, tt: (bb, i, tt, 0)),
        ],
        out_specs=pl.BlockSpec((None, None, SSM_WIDTH, tmt), lambda bb, i, tt: (bb, i, 0, tt)),
        out_shape=jax.ShapeDtypeStruct((b, SSM_BLOCK, SSM_WIDTH, ntau), F32),
        compiler_params=_params("parallel", "parallel", "parallel"),
        name="proj_u",
    )(wu_t, h_steps)


def _rel_bucket(rel):
    half = N_BUCKETS // 2
    exact = half // 2
    base = jnp.where(rel > 0, half, 0)
    n = jnp.abs(rel)
    nf = jnp.maximum(n, 1).astype(F32)
    large = exact + (jnp.log(nf / exact) / math.log(REL_MAX_DIST / exact) * (half - exact)).astype(jnp.int32)
    large = jnp.minimum(large, half - 1)
    return base + jnp.where(n < exact, n, large)


def _bias_tables(rel_bias):
    buckets = jnp.arange(N_BUCKETS)
    tabs = []
    for d, nq, nk in ATTN_GEOM:
        g = PLANES // d
        koff = (nk - nq) // 2
        aq = jnp.arange(g)[:, None, None, None]
        tq = jnp.arange(nq)[None, :, None, None]
        ak = jnp.arange(g)[None, None, :, None]
        tk = jnp.arange(nk)[None, None, None, :]
        rel = (g * ((tk - koff) - tq) + (ak - aq)).reshape(g * nq, g * nk)
        onehot = (_rel_bucket(rel * d)[..., None] == buckets).astype(F32)
        bias = jnp.einsum('qkb,bh->hqk', onehot, rel_bias.astype(F32), precision=lax.Precision.HIGHEST)
        bias = jnp.where((jnp.abs(rel) <= SIDE_KEYS)[None], bias, NEG)
        trow = jnp.arange(g * nk) % nk - koff
        before = (trow < 0)[None, None, :]
        after = (trow >= nq)[None, None, :]
        bias = bias * LOG2E
        tabs.append(jnp.stack([bias, jnp.where(before, NEG, bias), jnp.where(after, NEG, bias)], axis=1))
    return tabs


def _attn_kernel(q_ref, kp_ref, kc_ref, kn_ref, vp_ref, vc_ref, vn_ref, ga_ref, t1_ref, t4_ref, t16_ref, o_ref,
                 nbuf, dbuf, mbuf):
    t = pl.program_id(2)
    first = t == 0
    last = t == pl.num_programs(2) - 1
    k_refs = (kp_ref, kc_ref, kn_ref)
    v_refs = (vp_ref, vc_ref, vn_ref)
    ones = jnp.ones((HEAD_DIM,), BF16)

    def halo_rows(refs, plane, start, n):
        pieces = []
        r0 = -HALO_ROWS
        for ref in refs:
            r1 = r0 + ref.shape[1]
            lo, hi = max(start, r0), min(start + n, r1)
            if lo < hi:
                pieces.append(ref[plane, lo - r0:hi - r0, :])
            r0 = r1
        return pieces

    edge_bias = {}

    def logits(pi, grp):
        d, nq, nk = ATTN_GEOM[pi]
        tab_ref = (t1_ref, t4_ref, t16_ref)[pi]
        g = PLANES // d
        nblk = ATTN_ROWS // nq
        k0 = -((nk - nq) // 2)

        def rows_of(refs, r, start, n):
            pieces = [piece for a in range(g) for piece in halo_rows(refs, r + d * a, start, n)]
            return jnp.concatenate(pieces, axis=0).astype(BF16)

        def bias_of(blk):
            if (pi, blk) not in edge_bias:
                bias = tab_ref[0]
                if blk == 0:
                    bias = jnp.where(first, tab_ref[1], bias)
                if blk == nblk - 1:
                    bias = jnp.minimum(bias, jnp.where(last, tab_ref[2], tab_ref[0]))
                edge_bias[(pi, blk)] = bias
            return edge_bias[(pi, blk)]

        q = jnp.stack([jnp.concatenate([q_ref[r + d * a, blk * nq:(blk + 1) * nq, :] for a in range(g)],
                                       axis=0).astype(BF16) for r, blk in grp])
        k = jnp.stack([rows_of(k_refs, r, blk * nq + k0, nk) for r, blk in grp])
        v = jnp.stack([rows_of(v_refs, r, blk * nq + k0, nk) for r, blk in grp])
        edges = {blk if blk in (0, nblk - 1) else None for _, blk in grp}
        if len(edges) == 1:
            blk = edges.pop()
            bias = (tab_ref[0] if blk is None else bias_of(blk))[None]
        else:
            bias = jnp.stack([bias_of(blk) if blk in (0, nblk - 1) else tab_ref[0] for _, blk in grp])
        return jnp.einsum('bqd,bkd->bqk', q, k, preferred_element_type=F32) + bias, v

    def finish(pi, grp, s, v):
        d, nq, _ = ATTN_GEOM[pi]
        g = PLANES // d
        m = jnp.max(s, axis=-1, keepdims=True)
        p = jnp.exp2(s - m).astype(BF16)
        v1 = jnp.concatenate([v, jnp.broadcast_to(ones, v.shape)], axis=-1)
        acc = jnp.einsum('bqk,bkd->bqd', p, v1, preferred_element_type=F32)
        mb = jnp.broadcast_to(m, acc.shape[:2] + (HEAD_DIM,))
        for bi, (r, blk) in enumerate(grp):
            for a in range(g):
                rows = slice(blk * nq, (blk + 1) * nq)
                nbuf[pi, r + d * a, rows, :] = acc[bi, a * nq:(a + 1) * nq, :HEAD_DIM]
                dbuf[pi, r + d * a, rows, :] = acc[bi, a * nq:(a + 1) * nq, HEAD_DIM:]
                mbuf[pi, r + d * a, rows, :] = mb[bi, a * nq:(a + 1) * nq]

    work = []
    for pi, (d, nq, _) in enumerate(ATTN_GEOM):
        blocks = [(r, blk) for r in range(d) for blk in range(ATTN_ROWS // nq)]
        work += [(pi, blocks[g0:g0 + ATTN_BATCH[pi]]) for g0 in range(0, len(blocks), ATTN_BATCH[pi])]
    pending = logits(*work[0])
    for idx, (pi, grp) in enumerate(work):
        s, v = pending
        if idx + 1 < len(work):
            pending = logits(*work[idx + 1])
        finish(pi, grp, s, v)

    for i in range(PLANES):
        m0, m1, m2 = mbuf[0, i], mbuf[1, i], mbuf[2, i]
        mx = jnp.maximum(jnp.maximum(m0, m1), m2)
        e0, e1, e2 = jnp.exp2(m0 - mx), jnp.exp2(m1 - mx), jnp.exp2(m2 - mx)
        o = ((e0 * nbuf[0, i] + e1 * nbuf[1, i] + e2 * nbuf[2, i])
             / (e0 * dbuf[0, i] + e1 * dbuf[1, i] + e2 * dbuf[2, i]))
        ga = ga_ref[i]
        o_ref[i] = (o * (ga * jax.nn.sigmoid(ga))).astype(o_ref.dtype)


def _attention(p1, bias_tabs, b, s):
    nrow = s // PLANES
    p1 = p1.reshape(b, PLANES, nrow, P_WIDTH)
    per = ATTN_ROWS // HALO_ROWS
    n_halo = nrow // HALO_ROWS

    def tile(col):
        return pl.BlockSpec((None, PLANES, ATTN_ROWS, HEAD_DIM), lambda bb, h, t: (bb, 0, t, col // HEAD_DIM + h))

    def prev(col):
        return pl.BlockSpec((None, PLANES, HALO_ROWS, HEAD_DIM),
                            lambda bb, h, t: (bb, 0, jnp.maximum(t * per - 1, 0), col // HEAD_DIM + h))

    def nxt(col):
        return pl.BlockSpec((None, PLANES, HALO_ROWS, HEAD_DIM),
                            lambda bb, h, t: (bb, 0, jnp.minimum((t + 1) * per, n_halo - 1), col // HEAD_DIM + h))

    def table(tab):
        return pl.BlockSpec((None,) + tab.shape[1:], lambda bb, h, t: (h, 0, 0, 0))

    part_buf = pltpu.VMEM((len(ATTN_GEOM), PLANES, ATTN_ROWS, HEAD_DIM), F32)
    return pl.pallas_call(
        _attn_kernel,
        grid=(b, N_HEADS, nrow // ATTN_ROWS),
        in_specs=[tile(COL_Q), prev(COL_K), tile(COL_K), nxt(COL_K), prev(COL_V), tile(COL_V), nxt(COL_V),
                  tile(COL_GA)] + [table(tab) for tab in bias_tabs],
        out_specs=pl.BlockSpec((None, PLANES, ATTN_ROWS, HEAD_DIM), lambda bb, h, t: (bb, 0, t, h)),
        out_shape=jax.ShapeDtypeStruct((b, PLANES, nrow, ATTN_WIDTH), BF16),
        scratch_shapes=[part_buf, part_buf, part_buf],
        compiler_params=_params("parallel", "parallel", "parallel"),
        name="attention",
    )(p1, p1, p1, p1, p1, p1, p1, p1, *bias_tabs)


def _ssm_prep_kernel(arow_ref, acol_ref, c_ref, bw_ref, wz_ref, t0_ref, cc_ref, *, gc):
    nb, hc, p = SSM_BLOCK, SSM_GROUP, SSM_STATE
    blk_of_lane = lax.broadcasted_iota(jnp.int32, (hc, SSM_ROWS), 1) // hc
    blk_of_lane_p = lax.broadcasted_iota(jnp.int32, (p, SSM_ROWS), 1) // hc

    def body(gi, carry):
        lag_kernels = []
        for dirn in range(2):
            ar = arow_ref[gi, 2 * dirn:2 * dirn + 1, :]
            ai = arow_ref[gi, 2 * dirn + 1:2 * dirn + 2, :]
            cr = c_ref[gi, 2 * dirn]
            ci = c_ref[gi, 2 * dirn + 1]
            pr = jnp.ones((1, p), F32)
            pi = jnp.zeros((1, p), F32)
            ca_re, ca_im = [], []
            for _ in range(nb + 1):
                ca_re.append(cr * pr - ci * pi)
                ca_im.append(cr * pi + ci * pr)
                pr, pi = pr * ar - pi * ai, pr * ai + pi * ar
            bre = bw_ref[gi, 2 * dirn]
            bim = bw_ref[gi, 2 * dirn + 1]
            lag_kernels.append(
                jnp.dot(jnp.concatenate(ca_re[:nb], axis=0).astype(BF16), bre.astype(BF16),
                        preferred_element_type=F32)
                - jnp.dot(jnp.concatenate(ca_im[:nb], axis=0).astype(BF16), bim.astype(BF16),
                          preferred_element_type=F32))
            order = range(1, nb + 1) if dirn == 0 else range(nb, 0, -1)
            c0 = 2 * p * dirn
            cc_ref[gi, :, c0:c0 + p] = jnp.concatenate([ca_re[k] for k in order], axis=0).astype(BF16)
            cc_ref[gi, :, c0 + p:c0 + 2 * p] = (-jnp.concatenate([ca_im[k] for k in order], axis=0)).astype(BF16)
            acr = acol_ref[gi, c0:c0 + p, :]
            aci = acol_ref[gi, c0 + p:c0 + 2 * p, :]
            qr = jnp.ones((p, 1), F32)
            qi = jnp.zeros((p, 1), F32)
            powers = []
            for _ in range(nb):
                powers.append((qr, qi))
                qr, qi = qr * acr - qi * aci, qr * aci + qi * acr
            per = jnp.zeros((p, SSM_ROWS), F32)
            pei = jnp.zeros((p, SSM_ROWS), F32)
            for i in range(nb):
                e = nb - 1 - i if dirn == 0 else i
                per = jnp.where(blk_of_lane_p == i, powers[e][0], per)
                pei = jnp.where(blk_of_lane_p == i, powers[e][1], pei)
            wz_ref[gi, c0:c0 + p, :] = (per * bre - pei * bim).astype(BF16)
            wz_ref[gi, c0 + p:c0 + 2 * p, :] = (per * bim + pei * bre).astype(BF16)
        kf, kb = lag_kernels
        diag = kf[0:hc, :] + kb[0:hc, :]
        for j in range(nb):
            acc = jnp.zeros((hc, SSM_ROWS), F32)
            for i in range(nb):
                if i < j:
                    src = kf[hc * (j - i):hc * (j - i + 1), :]
                elif i > j:
                    src = kb[hc * (i - j):hc * (i - j + 1), :]
                else:
                    src = diag
                acc = jnp.where(blk_of_lane == i, src, acc)
            t0_ref[gi, hc * j:hc * (j + 1), :] = acc.astype(BF16)
        return carry

    lax.fori_loop(0, gc, body, 0)


def _ssm_weights(lam_re, lam_im, log_dt, b_re, b_im, c_re, c_im, gc=8):
    g, p, hc = SSM_GROUPS, SSM_STATE, SSM_GROUP
    dt = jnp.exp(log_dt.astype(F32))[..., None]
    lr = lam_re.astype(F32)
    li = lam_im.astype(F32)
    mag = jnp.exp(lr * dt)
    ab_re = mag * jnp.cos(li * dt)
    ab_im = mag * jnp.sin(li * dt)
    den = lr * lr + li * li
    f_re = ((ab_re - 1.0) * lr + ab_im * li) / den
    f_im = (ab_im * lr - (ab_re - 1.0) * li) / den
    br = b_re.astype(F32)
    bi = b_im.astype(F32)
    bb_re = f_re[..., None] * br - f_im[..., None] * bi
    bb_im = f_re[..., None] * bi + f_im[..., None] * br
    mag16 = jnp.exp(SSM_BLOCK * (lr * dt))
    a16_re = mag16 * jnp.cos(SSM_BLOCK * (li * dt))
    a16_im = mag16 * jnp.sin(SSM_BLOCK * (li * dt))

    def pack(re, im):
        return jnp.stack([re[0], im[0], re[1], im[1]], axis=1)

    arow = pack(ab_re, ab_im)
    acol = arow.reshape(g, 4 * p, 1)
    a16 = pack(a16_re, a16_im).reshape(g, 4 * p, 1)
    cmat = pack(c_re.astype(F32), c_im.astype(F32))
    bw = jnp.tile(pack(bb_re, bb_im), (1, 1, 1, SSM_BLOCK))
    wspec = pl.BlockSpec((gc, SSM_ROWS, SSM_ROWS), lambda i: (i, 0, 0))
    wshape = jax.ShapeDtypeStruct((g, SSM_ROWS, SSM_ROWS), BF16)
    wz, t0, cc = pl.pallas_call(
        functools.partial(_ssm_prep_kernel, gc=gc),
        grid=(g // gc,),
        in_specs=[pl.BlockSpec((gc, 4, p), lambda i: (i, 0, 0)),
                  pl.BlockSpec((gc, 4 * p, 1), lambda i: (i, 0, 0)),
                  pl.BlockSpec((gc, 4, hc, p), lambda i: (i, 0, 0, 0)),
                  pl.BlockSpec((gc, 4, p, SSM_ROWS), lambda i: (i, 0, 0, 0))],
        out_specs=[wspec, wspec, wspec],
        out_shape=[wshape, wshape, wshape],
        compiler_params=_params("parallel"),
        name="ssm_prep",
    )(arow, acol, cmat, bw)
    return wz, t0, cc, a16


def _gelu_tanh(x):
    return 0.5 * x * (1.0 + jnp.tanh(math.sqrt(2.0 / math.pi) * (x + 0.044715 * (x * x * x))))


def _cmul(ar, ai, br, bi):
    return ar * br - ai * bi, ar * bi + ai * br


def _block_states(zs, coefs, nc):
    p = SSM_STATE
    ng = len(zs)
    rows = ng * p
    lane = lax.broadcasted_iota(jnp.int32, (rows, LANES), 1)
    zero = jnp.zeros((rows, LANES), F32)

    def part(arrs, r0, c=None):
        cols = slice(None) if c is None else slice(LANES * c, LANES * (c + 1))
        return jnp.concatenate([x[r0:r0 + p, cols] for x in arrs], axis=0)

    halves = []
    for dirn in range(2):
        base = 2 * p * dirn
        zr = [part(zs, base, c) for c in range(nc)]
        zi = [part(zs, base + p, c) for c in range(nc)]
        ar = jnp.broadcast_to(part(coefs, base), (rows, LANES))
        ai = jnp.broadcast_to(part(coefs, base + p), (rows, LANES))
        order = list(range(nc)) if dirn == 0 else list(range(nc - 1, -1, -1))
        ir, ii = {order[0]: zr[order[0]]}, {order[0]: zi[order[0]]}
        for prev, c in zip(order[:-1], order[1:]):
            tr, ti = _cmul(ar, ai, ir[prev], ii[prev])
            ir[c], ii[c] = zr[c] + tr, zi[c] + ti
        pows = [(ar, ai)]
        for _ in range(nc - 1):
            pows.append(_cmul(pows[-1][0], pows[-1][1], ar, ai))
        gr, gi = ir[order[-1]], ii[order[-1]]
        br, bi = pows[nc - 1]
        s = 1
        while s < LANES:
            keep = (lane >= s) if dirn == 0 else (lane < LANES - s)
            amt = s if dirn == 0 else LANES - s
            rr = jnp.where(keep, pltpu.roll(gr, amt, 1), zero)
            ri = jnp.where(keep, pltpu.roll(gi, amt, 1), zero)
            tr, ti = _cmul(br, bi, rr, ri)
            gr, gi = gr + tr, gi + ti
            br, bi = _cmul(br, bi, br, bi)
            s *= 2
        keep = (lane >= 1) if dirn == 0 else (lane < LANES - 1)
        amt = 1 if dirn == 0 else LANES - 1
        er = jnp.where(keep, pltpu.roll(gr, amt, 1), zero)
        ei = jnp.where(keep, pltpu.roll(gi, amt, 1), zero)
        hr, hi = [None] * nc, [None] * nc
        hr[order[0]], hi[order[0]] = er, ei
        for k, (prev, c) in enumerate(zip(order[:-1], order[1:])):
            tr, ti = _cmul(pows[k][0], pows[k][1], er, ei)
            hr[c], hi[c] = ir[prev] + tr, ii[prev] + ti
        halves += [jnp.concatenate(hr, axis=1), jnp.concatenate(hi, axis=1)]
    return [jnp.concatenate([h[g * p:(g + 1) * p] for h in halves], axis=0) for g in range(ng)]


def _ssm_kernel(u_ref, wz_ref, t0_ref, cc_ref, a_ref, d_ref, perm_ref, unperm_ref, y_ref, *, gc, ntau):
    nc = ntau // LANES
    rows = [slice(SSM_GROUP * gi, SSM_GROUP * (gi + 1)) for gi in range(gc)]
    zs = []
    for gi in range(gc):
        x = u_ref[:, rows[gi], :].reshape(SSM_ROWS, ntau).astype(BF16)
        xp = jnp.dot(x, perm_ref[...], preferred_element_type=F32).astype(BF16)
        zs.append(jnp.dot(wz_ref[gi], xp, preferred_element_type=F32))
    states = _block_states(zs, [a_ref[gi] for gi in range(gc)], nc)
    for gi in range(gc):
        xf = u_ref[:, rows[gi], :].reshape(SSM_ROWS, ntau)
        h = jnp.dot(states[gi].astype(BF16), unperm_ref[...], preferred_element_type=F32).astype(BF16)
        y = (jnp.dot(t0_ref[gi], xf.astype(BF16), preferred_element_type=F32)
             + jnp.dot(cc_ref[gi], h, preferred_element_type=F32))
        y = _gelu_tanh(y + d_ref[gi] * xf)
        y_ref[:, rows[gi], :] = y.astype(y_ref.dtype).reshape(SSM_BLOCK, SSM_GROUP, ntau)


def _ssm(u_t, wz, t0, cc, a16, d_rows, gc=16):
    b, _, _, ntau = u_t.shape
    nc = ntau // LANES
    j = jnp.arange(ntau)
    src = (j % LANES) * nc + j // LANES
    perm = (jnp.arange(ntau)[:, None] == src[None, :]).astype(BF16)
    wspec = pl.BlockSpec((gc, SSM_ROWS, SSM_ROWS), lambda bb, g: (g, 0, 0))
    vspec = pl.BlockSpec((gc, SSM_ROWS, 1), lambda bb, g: (g, 0, 0))
    uspec = pl.BlockSpec((None, SSM_BLOCK, SSM_GROUP * gc, ntau), lambda bb, g: (bb, 0, g, 0))
    pspec = pl.BlockSpec((ntau, ntau), lambda bb, g: (0, 0))
    return pl.pallas_call(
        functools.partial(_ssm_kernel, gc=gc, ntau=ntau),
        grid=(b, SSM_GROUPS // gc),
        in_specs=[uspec, wspec, wspec, wspec, vspec, vspec, pspec, pspec],
        out_specs=uspec,
        out_shape=jax.ShapeDtypeStruct(u_t.shape, BF16),
        compiler_params=_params("parallel", "parallel"),
        name="ssm",
    )(u_t, wz, t0, cc, a16, d_rows, perm, perm.T)


def _column_pipeline(width, chunk, issue, finish):
    chunks = [slice(c, c + chunk) for c in range(0, width, chunk)]
    pending = issue(chunks[0])
    for idx, cols in enumerate(chunks):
        current = pending
        if idx + 1 < len(chunks):
            pending = issue(chunks[idx + 1])
        finish(cols, current)


def _glu_kernel(y_ref, w1_ref, w2_ref, gb_ref, o_ref):
    pp, tmt, _ = o_ref.shape
    yt = jnp.concatenate([y_ref[p].T for p in range(pp)], axis=0)

    def issue(cols):
        return (jnp.dot(yt, w1_ref[:, cols], preferred_element_type=F32),
                jnp.dot(yt, w2_ref[:, cols], preferred_element_type=F32))

    def finish(cols, gl):
        gb = gb_ref[:, :, cols].reshape(pp * tmt, -1)
        out = (gl[0] * jax.nn.sigmoid(gl[1]) * (gb * jax.nn.sigmoid(gb))).astype(o_ref.dtype)
        o_ref[:, :, cols] = out.reshape(pp, tmt, -1)

    _column_pipeline(o_ref.shape[2], GLU_CHUNK, issue, finish)


def _glu(y_t, w_glu, p2, tokens=512, tn=2048):
    b, _, _, ntau = y_t.shape
    tmt = min(tokens, ntau)
    pp = tokens // tmt
    nj = SSM_WIDTH // tn
    p2v = p2.reshape(b, SSM_BLOCK, ntau, P_WIDTH)
    gb0 = P1_WIDTH // tn
    out = pl.pallas_call(
        _glu_kernel,
        grid=(nj, b, SSM_BLOCK // pp, ntau // tmt),
        in_specs=[
            pl.BlockSpec((None, pp, SSM_WIDTH, tmt), lambda j, bb, i, tt: (bb, i, 0, tt)),
            pl.BlockSpec((SSM_WIDTH, tn), lambda j, bb, i, tt: (0, j)),
            pl.BlockSpec((SSM_WIDTH, tn), lambda j, bb, i, tt: (0, nj + j)),
            pl.BlockSpec((None, pp, tmt, tn), lambda j, bb, i, tt: (bb, i, tt, gb0 + j)),
        ],
        out_specs=pl.BlockSpec((None, pp, tmt, tn), lambda j, bb, i, tt: (bb, i, tt, j)),
        out_shape=jax.ShapeDtypeStruct((b, SSM_BLOCK, ntau, SSM_WIDTH), BF16),
        compiler_params=_params("parallel", "parallel", "parallel", "parallel"),
        name="glu",
    )(y_t, w_glu, w_glu, p2v)
    return out.reshape(b * SSM_BLOCK * ntau, SSM_WIDTH)


def _merge_kernel(a_ref, y_ref, wa_ref, wb_ref, ma_ref, mb_ref, unperm_ref, o_ref):
    rows = o_ref.shape[0]
    a = a_ref[...].reshape(rows, ATTN_WIDTH)
    y = y_ref[...].reshape(rows, SSM_WIDTH)
    sub = unperm_ref.shape[0]
    n = sub // SSM_BLOCK

    def issue(cols):
        return (jnp.dot(a, wa_ref[:, cols], preferred_element_type=F32),
                jnp.dot(y, wb_ref[:, cols], preferred_element_type=F32))

    def finish(cols, z):
        ma = ma_ref[:, :, cols].reshape(rows, -1)
        mb = mb_ref[:, :, cols].reshape(rows, -1)
        merged = (jax.nn.sigmoid(ma) * z[0] + jax.nn.sigmoid(mb) * z[1]).astype(BF16)
        merged = merged.reshape(SSM_BLOCK, rows // SSM_BLOCK, -1)
        for k in range(rows // sub):
            part = merged[:, k * n:(k + 1) * n, :].reshape(sub, -1)
            o_ref[k * sub:(k + 1) * sub, cols] = jnp.dot(unperm_ref[...], part,
                                                         preferred_element_type=F32).astype(o_ref.dtype)

    _column_pipeline(o_ref.shape[1], EPILOGUE_CHUNK, issue, finish)


def _step_permutation(tm):
    n = tm // SSM_BLOCK
    r = jnp.arange(tm)
    src = SSM_BLOCK * (r % n) + r // n
    return (src[:, None] == jnp.arange(tm)[None, :]).astype(BF16)


def _merge(a, y2, w_attn_proj, w_ssm_proj, p2, b, s, tm=512, tn=1024):
    ntau = s // SSM_BLOCK
    n = tm // SSM_BLOCK
    ma0 = (P1_WIDTH + COL_MA - COL_GB) // tn
    mb0 = (P1_WIDTH + COL_MB - COL_GB) // tn

    def steps(arr, width):
        return arr.reshape(b, SSM_BLOCK, ntau, width)

    return pl.pallas_call(
        _merge_kernel,
        grid=(D_MODEL // tn, b, s // tm),
        in_specs=[
            pl.BlockSpec((None, SSM_BLOCK, n, ATTN_WIDTH), lambda j, bb, t: (bb, 0, t, 0)),
            pl.BlockSpec((None, SSM_BLOCK, n, SSM_WIDTH), lambda j, bb, t: (bb, 0, t, 0)),
            pl.BlockSpec((ATTN_WIDTH, tn), lambda j, bb, t: (0, j)),
            pl.BlockSpec((SSM_WIDTH, tn), lambda j, bb, t: (0, j)),
            pl.BlockSpec((None, SSM_BLOCK, n, tn), lambda j, bb, t: (bb, 0, t, ma0 + j)),
            pl.BlockSpec((None, SSM_BLOCK, n, tn), lambda j, bb, t: (bb, 0, t, mb0 + j)),
            pl.BlockSpec((PERM_TILE, PERM_TILE), lambda j, bb, t: (0, 0)),
        ],
        out_specs=pl.BlockSpec((tm, tn), lambda j, bb, t: (bb * (s // tm) + t, j)),
        out_shape=jax.ShapeDtypeStruct((b * s, D_MODEL), BF16),
        compiler_params=_params("parallel", "parallel", "parallel"),
        name="merge",
    )(steps(a, ATTN_WIDTH), steps(y2, SSM_WIDTH), w_attn_proj, w_ssm_proj,
      steps(p2, P_WIDTH), steps(p2, P_WIDTH), _step_permutation(PERM_TILE).T)


def _out_kernel(m_ref, w_ref, x_ref, o_ref):
    o_ref[...] = x_ref[...] + jnp.dot(m_ref[...], w_ref[...], preferred_element_type=F32)


def _out_proj(merged, w_out, x2, tm=1024, tn=1024):
    t = merged.shape[0]
    return pl.pallas_call(
        _out_kernel,
        grid=(D_MODEL // tn, t // tm),
        in_specs=[
            pl.BlockSpec((tm, D_MODEL), lambda j, i: (i, 0)),
            pl.BlockSpec((D_MODEL, tn), lambda j, i: (0, j)),
            pl.BlockSpec((tm, tn), lambda j, i: (i, j)),
        ],
        out_specs=pl.BlockSpec((tm, tn), lambda j, i: (i, j)),
        out_shape=jax.ShapeDtypeStruct((t, D_MODEL), F32),
        compiler_params=_params("parallel", "parallel"),
        name="out_proj",
    )(merged, w_out, x2)


def _layer(x, w, late):
    b, s, d = x.shape
    h_steps = _prenorm(x, w["norm_gain"])
    h2 = h_steps.reshape(b * s, d)
    side = late if late[0].dtype != BF16 else ()
    p1, rounded = _proj(h2, w["w_in"], COL_U, P_WIDTH, w["qk_gain_row"], 2 * ATTN_WIDTH, "proj", side)
    w_glu, w_attn_proj, w_ssm_proj, w_out = rounded if side else late
    p2 = p1
    u_t = _proj_u(h_steps, w["wu_t"])
    a = _attention(p1, w["bias_tabs"], b, s).reshape(b * s, ATTN_WIDTH)
    y_t = _ssm(u_t, w["wz"], w["t0"], w["cc"], w["a16"], w["d_rows"])
    y2 = _glu(y_t, w_glu, p2)
    merged = _merge(a, y2, w_attn_proj, w_ssm_proj, p2, b, s)
    out = _out_proj(merged, w_out, x.reshape(b * s, d)).reshape(b, s, d)
    return out, (w_glu, w_attn_proj, w_ssm_proj, w_out)


def _prepare(rel_bias, norm_gain, w_in, q_gain, k_gain, lam_re, lam_im, log_dt, b_re, b_im, c_re, c_im, d_skip):
    wz, t0, cc, a16 = _ssm_weights(lam_re, lam_im, log_dt, b_re, b_im, c_re, c_im)
    gains = jnp.concatenate([jnp.tile(q_gain.astype(F32) * (HEAD_DIM ** -0.5 * LOG2E), N_HEADS),
                             jnp.tile(k_gain.astype(F32), N_HEADS),
                             jnp.zeros((P_WIDTH - 2 * ATTN_WIDTH,), F32)])
    d_rows = jnp.tile(d_skip.astype(F32).reshape(SSM_GROUPS, 1, SSM_GROUP), (1, SSM_BLOCK, 1))
    return {
        "norm_gain": norm_gain.astype(F32),
        "w_in": w_in.astype(BF16),
        "wu_t": w_in[:, COL_U:COL_U + SSM_WIDTH].T.astype(BF16),
        "qk_gain_row": gains.reshape(1, P_WIDTH),
        "bias_tabs": _bias_tables(rel_bias),
        "wz": wz, "t0": t0, "cc": cc, "a16": a16,
        "d_rows": d_rows.reshape(SSM_GROUPS, SSM_ROWS, 1),
    }


def kernel(x_prompt, x_sample, rel_bias, norm_gain, w_in, q_gain, k_gain, lam_re, lam_im, log_dt, b_re, b_im,
           c_re, c_im, d_skip, w_glu, w_attn_proj, w_ssm_proj, w_out):
    y_prompt, y_sample = x_prompt, x_sample
    for l in range(norm_gain.shape[0]):
        w = _prepare(rel_bias, norm_gain[l], w_in[l], q_gain[l], k_gain[l], lam_re[l], lam_im[l], log_dt[l],
                     b_re[l], b_im[l], c_re[l], c_im[l], d_skip[l])
        late = tuple(m[l].astype(F32) for m in (w_glu, w_attn_proj, w_ssm_proj, w_out))
        y_prompt, late = _layer(y_prompt, w, late)
        y_sample, _ = _layer(y_sample, w, late)
    return (y_prompt, y_sample)
```
